```python
import math
import jax
import jax.numpy as jnp
from jax import lax
import numpy as np

D_MODEL = 1024
BATCH = 2
SEQ = 16384
DEPTH = 2

SSD_EXPAND = 1
SSD_INNER = SSD_EXPAND * D_MODEL
SSD_HEAD_DIM = 64
SSD_HEADS = SSD_INNER // SSD_HEAD_DIM
SSD_GROUPS = 4
SSD_HEADS_PER_GROUP = SSD_HEADS // SSD_GROUPS
SSD_STATE = 128
SSD_CONV = 4
SSD_CHUNK = 128
SSD_CONV_DIM = SSD_INNER + 2 * SSD_GROUPS * SSD_STATE

SB_HEADS = 4
SB_HEAD_DIM = 128
SB_WIDTH = SB_HEADS * SB_HEAD_DIM
SB_BLOCK = 128
SB_SUB = 32

POOL_WINDOWS = (2, 4, 8, 16)
POOL_GROUPS = 4
POOL_WIDTH = D_MODEL // 2
POOL_GROUP_DIM = POOL_WIDTH // POOL_GROUPS

N_BRANCHES = 3

FFN_DENSE = 2816
N_EXPERTS = 8
TOP_K = 2
FFN_EXPERT = 1792
MOE_BLOCK = 512

EPS = 1e-6

COL_Z = SSD_INNER
COL_XBC = COL_Z + SSD_CONV_DIM
COL_DT = COL_XBC + SSD_HEADS
COL_Q = COL_DT + SB_WIDTH
COL_K = COL_Q + SB_WIDTH
COL_V = COL_K + SB_WIDTH
COL_POOL = COL_V + POOL_WIDTH
IN_WIDTH = COL_POOL + N_BRANCHES * D_MODEL

kernel_name = "hybrid_ssd_stickbreak_pool_moe"


def _rms(xf):
    return xf * lax.rsqrt(jnp.mean(xf * xf, axis=-1, keepdims=True) + EPS)


def rms_norm(x, g):
    y = _rms(x.astype(jnp.float32)) * g.astype(jnp.float32)
    return y.astype(x.dtype)


def causal_depthwise_conv(u, w, b):
    ch = u.shape[-1]
    out = lax.conv_general_dilated(
        u, w[:, None, :], window_strides=(1,), padding=[(SSD_CONV - 1, 0)],
        dimension_numbers=("NWC", "WIO", "NWC"), feature_group_count=ch)
    return out + b


def ssd_branch(z, xbc, dt_raw, conv_w, conv_b, dt_bias, a_log, d_skip, norm_g):
    f32 = jnp.float32
    bsz, seq, _ = z.shape
    nc, L = seq // SSD_CHUNK, SSD_CHUNK
    G, R, P, N = SSD_GROUPS, SSD_HEADS_PER_GROUP, SSD_HEAD_DIM, SSD_STATE
    xbc = jax.nn.silu(causal_depthwise_conv(xbc.astype(f32), conv_w.astype(f32), conv_b.astype(f32)))
    xs, b_in, c_in = jnp.split(xbc, [SSD_INNER, SSD_INNER + G * N], axis=-1)
    dt = jax.nn.softplus(dt_raw.astype(f32) + dt_bias.astype(f32))
    da = dt * (-jnp.exp(a_log.astype(f32)))
    x_dt = (xs.reshape(bsz, seq, SSD_HEADS, P) * dt[..., None]).reshape(bsz, nc, L, G, R, P)
    bm = b_in.reshape(bsz, nc, L, G, N)
    cm = c_in.reshape(bsz, nc, L, G, N)
    a_cs = jnp.cumsum(da.reshape(bsz, nc, L, G, R).transpose(0, 1, 3, 4, 2), axis=-1)

    causal = jnp.tril(jnp.ones((L, L), dtype=bool))
    seg = a_cs[..., :, None] - a_cs[..., None, :]
    decay = jnp.exp(jnp.where(causal, seg, -jnp.inf))
    cb = jnp.einsum("bclgn,bcsgn->bcgls", cm, bm)
    y_diag = jnp.einsum("bcgrls,bcsgrp->bclgrp", cb[:, :, :, None] * decay, x_dt)

    def chunk_step(state, inp):
        b_c, c_c, x_c, acs_c = inp
        y_off = jnp.einsum("blgn,bgrpn,bgrl->blgrp", c_c, state, jnp.exp(acs_c))
        to_end = jnp.exp(acs_c[..., -1:] - acs_c)
        new_state = (state * jnp.exp(acs_c[..., -1])[..., None, None]
                     + jnp.einsum("blgn,bgrl,blgrp->bgrpn", b_c, to_end, x_c))
        return new_state, y_off

    state0 = jnp.zeros((bsz, G, R, P, N), f32)
    _, y_off = lax.scan(chunk_step, state0,
                        (jnp.moveaxis(bm, 1, 0), jnp.moveaxis(cm, 1, 0),
                         jnp.moveaxis(x_dt, 1, 0), jnp.moveaxis(a_cs, 1, 0)))
    y = (y_diag + jnp.moveaxis(y_off, 0, 1)).reshape(bsz, seq, SSD_HEADS, P)
    y = y + xs.reshape(bsz, seq, SSD_HEADS, P) * d_skip.astype(f32)[:, None]
    y = y.reshape(bsz, seq, SSD_INNER) * jax.nn.silu(z.astype(f32))
    y = _rms(y.reshape(bsz, seq, G, SSD_INNER // G)).reshape(bsz, seq, SSD_INNER)
    return y * norm_g.astype(f32)


def stick_breaking_attention(q, k, v, q_norm_g, k_norm_g):
    f32 = jnp.float32
    bsz, seq, _ = q.shape
    H, Dh = SB_HEADS, SB_HEAD_DIM
    scale = 1.0 / math.sqrt(Dh)
    qh = (rms_norm(q.reshape(bsz, seq, H, Dh).astype(f32), q_norm_g) * scale).transpose(0, 2, 1, 3)
    kh = rms_norm(k.reshape(bsz, seq, H, Dh).astype(f32), k_norm_g).transpose(0, 2, 1, 3)
    vh = v.reshape(bsz, seq, H, Dh).astype(f32).transpose(0, 2, 1, 3)
    sub_ids = jnp.arange(SB_SUB)
    tri_sub = (sub_ids[:, None] > sub_ids[None, :]).astype(f32)
    q_local = jnp.arange(SB_BLOCK)
    outs = []
    for i in range(seq // SB_BLOCK):
        lk = (i + 1) * SB_BLOCK
        nk = lk // SB_SUB
        z = jnp.einsum("bhqd,bhkd->bhqk", qh[:, :, i * SB_BLOCK:lk], kh[:, :, :lk])
        mask = jnp.arange(lk)[None, :] < (i * SB_BLOCK + q_local)[:, None]
        log_beta = jnp.minimum(z, 0.0) - jnp.log1p(jnp.exp(-jnp.abs(z)))
        log_rest = jnp.where(mask, log_beta - z, 0.0)
        log_rest = log_rest.reshape(bsz, H, SB_BLOCK, nk, SB_SUB)
        local = jnp.einsum("bhqkj,js->bhqks", log_rest, tri_sub)
        blk_ids = jnp.arange(nk)
        tri_blk = (blk_ids[:, None] > blk_ids[None, :]).astype(f32)
        later = jnp.einsum("bhqk,kl->bhql", jnp.sum(log_rest, axis=-1), tri_blk)
        suffix = (local + later[..., None]).reshape(bsz, H, SB_BLOCK, lk)
        weights = jnp.where(mask, jnp.exp(log_beta + suffix), 0.0)
        outs.append(jnp.einsum("bhqk,bhkd->bhqd", weights, vh[:, :, :lk]))
    out = jnp.concatenate(outs, axis=2)
    return out.transpose(0, 2, 1, 3).reshape(bsz, seq, SB_WIDTH)


def multiscale_pool(u, pool_w, pool_scale):
    f32 = jnp.float32
    bsz, seq, _ = u.shape
    grp = u.astype(f32).reshape(bsz, seq, POOL_GROUPS, POOL_GROUP_DIM)
    csum = jnp.cumsum(grp, axis=1)
    t = jnp.arange(seq)
    outs = []
    for gi, win in enumerate(POOL_WINDOWS):
        c = csum[:, :, gi]
        prev = jnp.pad(c, ((0, 0), (win, 0), (0, 0)))[:, :seq]
        count = jnp.minimum(t + 1, win).astype(f32)[None, :, None]
        outs.append((c - prev) / count - grp[:, :, gi])
    pooled = jnp.stack(outs, axis=2)
    mixed = jnp.einsum("bsgi,gio->bsgo", pooled, pool_w.astype(f32))
    return mixed.reshape(bsz, seq, POOL_WIDTH) * pool_scale.astype(f32)


def hybrid_mixer(h, w_in, conv_w, conv_b, dt_bias, a_log, d_skip, ssd_norm_g, q_norm_g, k_norm_g,
                 pool_w, pool_scale, w_br_ssd, w_br_sb, w_br_pool, w_out):
    bsz, seq, _ = h.shape
    proj = h @ w_in
    z, xbc, dt_raw, q, k, v, u, gates = jnp.split(
        proj, [COL_Z, COL_XBC, COL_DT, COL_Q, COL_K, COL_V, COL_POOL], axis=-1)
    y_ssd = ssd_branch(z, xbc, dt_raw, conv_w, conv_b, dt_bias, a_log, d_skip, ssd_norm_g).astype(h.dtype)
    y_sb = stick_breaking_attention(q, k, v, q_norm_g, k_norm_g).astype(h.dtype)
    y_pool = multiscale_pool(u, pool_w, pool_scale).astype(h.dtype)
    g = jax.nn.sigmoid(gates.astype(jnp.float32)).reshape(bsz, seq, N_BRANCHES, D_MODEL)
    merged = (g[:, :, 0] * (y_ssd @ w_br_ssd).astype(jnp.float32)
              + g[:, :, 1] * (y_sb @ w_br_sb).astype(jnp.float32)
              + g[:, :, 2] * (y_pool @ w_br_pool).astype(jnp.float32))
    return merged.astype(h.dtype) @ w_out


def swiglu(h, w_gate, w_up, w_down):
    return (jax.nn.silu(h @ w_gate) * (h @ w_up)) @ w_down


def moe_swiglu(h, router_w, e_gate, e_up, e_down):
    f32 = jnp.float32
    bsz, seq, d = h.shape
    tokens = h.reshape(-1, d)
    n_tok = tokens.shape[0]
    logits = tokens.astype(f32) @ router_w.astype(f32)
    top_val, top_idx = lax.top_k(logits, TOP_K)
    top_w = jax.nn.softmax(top_val, axis=-1)
    n_assign = n_tok * TOP_K
    cap = -(-n_assign // MOE_BLOCK) * MOE_BLOCK + N_EXPERTS * MOE_BLOCK
    n_blocks = cap // MOE_BLOCK
    expert_flat = top_idx.reshape(-1).astype(jnp.int32)
    weight_flat = top_w.reshape(-1)
    token_flat = jnp.arange(n_assign, dtype=jnp.int32) // TOP_K
    order = jnp.argsort(expert_flat)
    sorted_exp = expert_flat[order]
    counts = jnp.zeros((N_EXPERTS,), jnp.int32).at[expert_flat].add(1)
    padded = (counts + MOE_BLOCK - 1) // MOE_BLOCK * MOE_BLOCK
    starts = jnp.cumsum(counts) - counts
    pends = jnp.cumsum(padded)
    pstarts = pends - padded
    dest = pstarts[sorted_exp] + jnp.arange(n_assign, dtype=jnp.int32) - starts[sorted_exp]
    slot_tok = jnp.full((cap,), n_tok, jnp.int32).at[dest].set(token_flat[order])
    slot_w = jnp.zeros((cap,), f32).at[dest].set(weight_flat[order])
    block_exp = jnp.minimum(jnp.searchsorted(pends, jnp.arange(n_blocks) * MOE_BLOCK, side="right"),
                            N_EXPERTS - 1).astype(jnp.int32)
    x_pad = jnp.concatenate([tokens, jnp.zeros((1, d), tokens.dtype)], axis=0)
    xb = x_pad[slot_tok].reshape(n_blocks, MOE_BLOCK, d)

    def expert_block(args):
        xblk, e = args
        return swiglu(xblk, e_gate[e], e_up[e], e_down[e])

    yb = lax.map(expert_block, (xb, block_exp)).reshape(cap, d)
    out = jnp.zeros((n_tok + 1, d), f32).at[slot_tok].add(yb.astype(f32) * slot_w[:, None])
    return out[:n_tok].astype(h.dtype).reshape(bsz, seq, d)


def setup_inputs(seed: int = 0) -> dict:
    key = jax.random.key(seed)
    ks = jax.random.split(key, 32)
    n_dense = (DEPTH + 1) // 2
    n_moe = DEPTH // 2

    def nrm(k, shape, fan_in):
        return jax.random.normal(k, shape, jnp.float32) * (fan_in ** -0.5)

    def gain(k, shape):
        return 1.0 + 0.02 * jax.random.normal(k, shape, jnp.float32)

    dt0 = jnp.exp(jax.random.uniform(ks[5], (DEPTH, SSD_HEADS), jnp.float32,
                                     minval=math.log(1e-3), maxval=math.log(1e-1)))
    dt0 = jnp.maximum(dt0, 1e-4)
    dt_bias = dt0 + jnp.log(-jnp.expm1(-dt0))
    a_log = jnp.log(jax.random.uniform(ks[6], (DEPTH, SSD_HEADS), jnp.float32, minval=1.0, maxval=16.0))
    return {
        "x": jax.random.normal(ks[0], (BATCH, SEQ, D_MODEL), jnp.float32),
        "mix_norm_g": gain(ks[1], (DEPTH, D_MODEL)),
        "w_in": nrm(ks[2], (DEPTH, D_MODEL, IN_WIDTH), D_MODEL),
        "conv_w": nrm(ks[3], (DEPTH, SSD_CONV, SSD_CONV_DIM), SSD_CONV),
        "conv_b": 0.01 * jax.random.normal(ks[4], (DEPTH, SSD_CONV_DIM), jnp.float32),
        "dt_bias": dt_bias,
        "a_log": a_log,
        "d_skip": gain(ks[7], (DEPTH, SSD_HEADS)),
        "ssd_norm_g": gain(ks[8], (DEPTH, SSD_INNER)),
        "q_norm_g": gain(ks[9], (DEPTH, SB_HEAD_DIM)),
        "k_norm_g": gain(ks[10], (DEPTH, SB_HEAD_DIM)),
        "pool_w": nrm(ks[11], (DEPTH, POOL_GROUPS, POOL_GROUP_DIM, POOL_GROUP_DIM), POOL_GROUP_DIM),
        "pool_scale": gain(ks[12], (DEPTH, POOL_WIDTH)),
        "w_br_ssd": nrm(ks[13], (DEPTH, SSD_INNER, D_MODEL), SSD_INNER),
        "w_br_sb": nrm(ks[14], (DEPTH, SB_WIDTH, D_MODEL), SB_WIDTH),
        "w_br_pool": nrm(ks[15], (DEPTH, POOL_WIDTH, D_MODEL), POOL_WIDTH),
        "w_out": nrm(ks[16], (DEPTH, D_MODEL, D_MODEL), D_MODEL),
        "ffn_norm_g": gain(ks[17], (DEPTH, D_MODEL)),
        "ffn_w_gate": nrm(ks[18], (n_dense, D_MODEL, FFN_DENSE), D_MODEL),
        "ffn_w_up": nrm(ks[19], (n_dense, D_MODEL, FFN_DENSE), D_MODEL),
        "ffn_w_down": nrm(ks[20], (n_dense, FFN_DENSE, D_MODEL), FFN_DENSE),
        "router_w": nrm(ks[21], (n_moe, D_MODEL, N_EXPERTS), D_MODEL),
        "moe_w_gate": nrm(ks[22], (n_moe, N_EXPERTS, D_MODEL, FFN_EXPERT), D_MODEL),
        "moe_w_up": nrm(ks[23], (n_moe, N_EXPERTS, D_MODEL, FFN_EXPERT), D_MODEL),
        "moe_w_down": nrm(ks[24], (n_moe, N_EXPERTS, FFN_EXPERT, D_MODEL), FFN_EXPERT),
    }


def reference(x, mix_norm_g, w_in, conv_w, conv_b, dt_bias, a_log, d_skip, ssd_norm_g, q_norm_g,
              k_norm_g, pool_w, pool_scale, w_br_ssd, w_br_sb, w_br_pool, w_out, ffn_norm_g,
              ffn_w_gate, ffn_w_up, ffn_w_down, router_w, moe_w_gate, moe_w_up, moe_w_down):
    for layer in range(DEPTH):
        h = rms_norm(x, mix_norm_g[layer])
        x = x + hybrid_mixer(h, w_in[layer], conv_w[layer], conv_b[layer], dt_bias[layer], a_log[layer],
                             d_skip[layer], ssd_norm_g[layer], q_norm_g[layer], k_norm_g[layer],
                             pool_w[layer], pool_scale[layer], w_br_ssd[layer], w_br_sb[layer],
                             w_br_pool[layer], w_out[layer])
        h = rms_norm(x, ffn_norm_g[layer])
        i = layer // 2
        if layer % 2 == 0:
            x = x + swiglu(h, ffn_w_gate[i], ffn_w_up[i], ffn_w_down[i])
        else:
            x = x + moe_swiglu(h, router_w[i], moe_w_gate[i], moe_w_up[i], moe_w_down[i])
    return x
```

```python
import functools
import math

import jax
import jax.numpy as jnp
from jax import lax
from jax.experimental import pallas as pl
from jax.experimental.pallas import tpu as pltpu

F32 = jnp.float32
BF16 = jnp.bfloat16

D_MODEL = 1024
EPS = 1e-6

SSD_INNER = 1024
SSD_HEAD_DIM = 64
SSD_HEADS = 16
SSD_GROUPS = 4
SSD_HEADS_PER_GROUP = 4
SSD_STATE = 128
SSD_CONV = 4
SSD_CHUNK = 128
SSD_CONV_DIM = SSD_INNER + 2 * SSD_GROUPS * SSD_STATE

SB_HEADS = 4
SB_HEAD_DIM = 128
SB_WIDTH = SB_HEADS * SB_HEAD_DIM
SB_BLOCK = 128

POOL_WINDOWS = (2, 4, 8, 16)
POOL_GROUPS = 4
POOL_WIDTH = 512
POOL_GROUP_DIM = 128
POOL_HALO = 16

N_BRANCHES = 3
FFN_DENSE = 2816
N_EXPERTS = 8
TOP_K = 2
FFN_EXPERT = 1792
MOE_BLOCK = 512

COL_Z = SSD_INNER
COL_XBC = COL_Z + SSD_CONV_DIM
COL_DT = COL_XBC + SSD_HEADS
COL_Q = COL_DT + SB_WIDTH
COL_K = COL_Q + SB_WIDTH
COL_V = COL_K + SB_WIDTH
COL_POOL = COL_V + POOL_WIDTH

P_XBC = 0
P_Z = 2048
P_Q = 3072
P_K = 3584
P_V = 4096
P_U = 4608
P_GATES = 5120
P_WIDTH = 8192

LANES = 128
VMEM_LIMIT = 56 * 1024 * 1024
FFN_CHUNK = 256
SB_STOP = -110.0


def _params(*sem):
    return pltpu.CompilerParams(dimension_semantics=sem, vmem_limit_bytes=VMEM_LIMIT)


def _const_spec(shape):
    nd = len(shape)
    return pl.BlockSpec(shape, lambda *_: (0,) * nd)


def _split3(x):
    a = x.astype(BF16)
    r = x - a.astype(F32)
    b = r.astype(BF16)
    c = (r - b.astype(F32)).astype(BF16)
    return a, b, c


def _dot(a, b):
    return jnp.dot(a, b, preferred_element_type=F32)


def _dot_nt(a, b):
    return lax.dot_general(a, b, (((1,), (1,)), ((), ())), preferred_element_type=F32)


def _dot_f32_by_01(x, e01):
    a, b, c = _split3(x)
    return _dot(a, e01) + _dot(b, e01) + _dot(c, e01)


def _dot_01_by_f32(t01, x):
    a, b, c = _split3(x)
    return _dot(t01, a) + _dot(t01, b) + _dot(t01, c)


def _rms(x):
    return x * lax.rsqrt(jnp.mean(x * x, axis=-1, keepdims=True) + EPS)


def _silu(x):
    return x * jax.nn.sigmoid(x)


def _softplus(x):
    return jnp.maximum(x, 0.0) + jnp.log1p(jnp.exp(-jnp.abs(x)))


def _inproj_body(x_ref, g_ref, w_ref, wdt_ref, o_ref, dt_ref, xn_ref):
    @pl.when(pl.program_id(1) == 0)
    def _():
        xb = (_rms(x_ref[...]) * g_ref[...]).astype(BF16)
        xn_ref[...] = xb
        dt_ref[...] = _dot(xb, wdt_ref[...])

    o_ref[...] = _dot(xn_ref[...], w_ref[...])


def _inproj(x2, g, w_main, w_dt):
    t = x2.shape[0]
    tm = min(1024, t)
    tn = 1024
    return pl.pallas_call(
        _inproj_body,
        grid=(t // tm, P_WIDTH // tn),
        in_specs=[
            pl.BlockSpec((tm, D_MODEL), lambda i, j: (i, 0)),
            _const_spec((1, D_MODEL)),
            pl.BlockSpec((D_MODEL, tn), lambda i, j: (0, j)),
            _const_spec((D_MODEL, LANES)),
        ],
        out_specs=[
            pl.BlockSpec((tm, tn), lambda i, j: (i, j)),
            pl.BlockSpec((tm, LANES), lambda i, j: (i, 0)),
        ],
        out_shape=[
            jax.ShapeDtypeStruct((t, P_WIDTH), F32),
            jax.ShapeDtypeStruct((t, LANES), F32),
        ],
        scratch_shapes=[pltpu.VMEM((tm, D_MODEL), BF16)],
        compiler_params=_params("parallel", "arbitrary"),
        name="inproj",
    )(x2, g, w_main, w_dt)


def _qkv_body(q_ref, k_ref, v_ref, qg_ref, kg_ref, qo_ref, ko_ref, vo_ref):
    scale = 1.0 / math.sqrt(SB_HEAD_DIM)
    for h in range(SB_HEADS):
        sl = slice(h * SB_HEAD_DIM, (h + 1) * SB_HEAD_DIM)
        qo_ref[0, h] = ((_rms(q_ref[:, sl]) * qg_ref[...]) * scale).astype(BF16)
        ko_ref[0, h] = (_rms(k_ref[:, sl]) * kg_ref[...]).astype(BF16)
        vo_ref[0, h] = v_ref[:, sl].astype(BF16)


def _qkv_prep(proj, qg, kg, bsz, seq):
    ts = min(512, seq)
    ns = seq // ts
    col = lambda c: pl.BlockSpec((ts, SB_WIDTH), lambda b, s: (b * ns + s, c))
    out_spec = pl.BlockSpec((1, SB_HEADS, ts, SB_HEAD_DIM), lambda b, s: (b, 0, s, 0))
    out_shape = jax.ShapeDtypeStruct((bsz, SB_HEADS, seq, SB_HEAD_DIM), BF16)
    return pl.pallas_call(
        _qkv_body,
        grid=(bsz, ns),
        in_specs=[col(P_Q // SB_WIDTH), col(P_K // SB_WIDTH), col(P_V // SB_WIDTH),
                  _const_spec((1, SB_HEAD_DIM)), _const_spec((1, SB_HEAD_DIM))],
        out_specs=[out_spec, out_spec, out_spec],
        out_shape=[out_shape, out_shape, out_shape],
        compiler_params=_params("parallel", "parallel"),
        name="qkv_prep",
    )(proj, proj, proj, qg, kg)


def _sb_body(q_ref, k_ref, v_ref, uo_ref, o_ref):
    tq = SB_BLOCK
    qi = pl.program_id(2)
    q = q_ref[0, 0]
    uo = uo_ref[...]

    def scores(kb):
        start = pl.multiple_of(kb * tq, tq)
        k = k_ref[0, 0, pl.ds(start, tq), :]
        v = v_ref[0, 0, pl.ds(start, tq), :]
        z = _dot_nt(q, k)
        log_beta = jnp.minimum(z, 0.0) - jnp.log1p(jnp.exp(-jnp.abs(z)))
        return z, log_beta, v

    def accumulate(log_beta, log_rest, v, carry, acc, mask):
        hi = log_rest.astype(BF16)
        lo = (log_rest - hi.astype(F32)).astype(BF16)
        sr = _dot(hi, uo) + _dot(lo, uo)
        w = jnp.exp(log_beta + sr[:, :tq] + carry)
        if mask is not None:
            w = jnp.where(mask, w, 0.0)
        acc = acc + _dot(w.astype(BF16), v)
        return carry + sr[:, tq:], acc

    row = lax.broadcasted_iota(jnp.int32, (tq, tq), 0)
    col = lax.broadcasted_iota(jnp.int32, (tq, tq), 1)
    mask = col < row
    z, log_beta, v = scores(qi)
    zeros = jnp.zeros((tq, tq), F32)
    carry, acc = accumulate(log_beta, jnp.where(mask, log_beta - z, 0.0), v, zeros, zeros, mask)

    def cond(state):
        kb, carry, _ = state
        return jnp.logical_and(kb >= 0, jnp.max(carry) > SB_STOP)

    def body(state):
        kb, carry, acc = state
        z, log_beta, v = scores(kb)
        carry, acc = accumulate(log_beta, log_beta - z, v, carry, acc, None)
        return kb - 1, carry, acc

    _, _, acc = lax.while_loop(cond, body, (qi - 1, carry, acc))
    o_ref[0] = acc.astype(o_ref.dtype)


def _sb_attention(qn, kn, vn):
    bsz, _, seq, _ = qn.shape
    tq = SB_BLOCK
    r = jnp.arange(tq)
    upper = (r[:, None] > r[None, :])
    uo = jnp.concatenate([upper, jnp.ones((tq, tq), bool)], axis=1).astype(BF16)
    kv_spec = pl.BlockSpec((1, 1, seq, SB_HEAD_DIM), lambda b, h, i: (b, h, 0, 0))
    return pl.pallas_call(
        _sb_body,
        grid=(bsz, SB_HEADS, seq // tq),
        in_specs=[pl.BlockSpec((1, 1, tq, SB_HEAD_DIM), lambda b, h, i: (b, h, i, 0)),
                  kv_spec, kv_spec, _const_spec((tq, 2 * tq))],
        out_specs=pl.BlockSpec((1, tq, SB_HEAD_DIM), lambda b, h, i: (b, i, h)),
        out_shape=jax.ShapeDtypeStruct((bsz, seq, SB_WIDTH), BF16),
        compiler_params=_params("parallel", "parallel", "arbitrary"),
        name="sb_attention",
    )(qn, kn, vn, uo)


def _ssd_body(xbc_ref, z_ref, dt_ref, cw_ref, cb_ref, dtb_ref, alog_ref, dskip_ref, ng_ref, e_ref, tri_ref,
              o_ref, buf_ref, st_ref, y_ref):
    L = SSD_CHUNK
    G, R, P, N = SSD_GROUPS, SSD_HEADS_PER_GROUP, SSD_HEAD_DIM, SSD_STATE
    tail = 8

    @pl.when(pl.program_id(1) == 0)
    def _():
        buf_ref[0:tail, :] = jnp.zeros((tail, SSD_CONV_DIM), F32)
        st_ref[...] = jnp.zeros_like(st_ref)

    buf_ref[tail:tail + L, :] = xbc_ref[...]

    def conv_silu(lo, hi):
        acc = cb_ref[:, lo:hi]
        for k in range(SSD_CONV):
            off = tail - (SSD_CONV - 1) + k
            acc = acc + cw_ref[k:k + 1, lo:hi] * buf_ref[off:off + L, lo:hi]
        return _silu(acc)

    xs = conv_silu(0, SSD_INNER)
    b_in = conv_silu(SSD_INNER, SSD_INNER + G * N)
    c_in = conv_silu(SSD_INNER + G * N, SSD_CONV_DIM)
    buf_ref[0:tail, :] = buf_ref[L:L + tail, :]

    e01 = e_ref[...]
    dt = _softplus(dt_ref[...] + dtb_ref[...])
    da = dt * (-jnp.exp(alog_ref[...]))
    a_cs = _dot_01_by_f32(tri_ref[...], da)
    a_cs_t = a_cs.T
    a_full = _dot_f32_by_01(a_cs, e01)
    dt_full = _dot_f32_by_01(dt, e01)
    a_last = a_full[L - 1:L, :]
    ea_full = jnp.exp(a_full)
    x_dt = xs * dt_full
    xw = (x_dt * jnp.exp(a_last - a_full)).astype(BF16)
    ea_last = jnp.exp(a_last)

    row = lax.broadcasted_iota(jnp.int32, (L, L), 0)
    col = lax.broadcasted_iota(jnp.int32, (L, L), 1)
    causal = col <= row
    lane = lax.broadcasted_iota(jnp.int32, (L, LANES), 1)
    first_head = lane < P

    for g in range(G):
        gs = slice(g * N, (g + 1) * N)
        cg = c_in[:, gs].astype(BF16)
        bg_f32 = b_in[:, gs]
        cb = _dot_nt(cg, bg_f32.astype(BF16))
        for pair in range(R // 2):
            ms = []
            for r in range(2):
                h = g * R + pair * 2 + r
                seg = a_cs[:, h:h + 1] - a_cs_t[h:h + 1, :]
                decay = jnp.exp(jnp.where(causal, seg, -jnp.inf))
                ms.append((cb * decay).astype(BF16))
            c0 = (g * R + pair * 2) * P
            xp = x_dt[:, c0:c0 + LANES]
            rhs = jnp.concatenate([jnp.where(first_head, xp, 0.0).astype(BF16),
                                   jnp.where(first_head, 0.0, xp).astype(BF16)], axis=0)
            y_ref[:, c0:c0 + LANES] = _dot(jnp.concatenate(ms, axis=1), rhs)
        cs = slice(g * R * P, (g + 1) * R * P)
        state = st_ref[g]
        y_ref[:, cs] += _dot(cg, state.astype(BF16)) * ea_full[:, cs]
        st_ref[g] = state * ea_last[:, cs] + _dot(bg_f32.T.astype(BF16), xw[:, cs])

    y = y_ref[...] + xs * dskip_ref[...]
    y = y * _silu(z_ref[...])
    gw = SSD_INNER // G
    for g in range(G):
        cs = slice(g * gw, (g + 1) * gw)
        o_ref[:, cs] = (_rms(y[:, cs]) * ng_ref[:, cs]).astype(o_ref.dtype)


def _ssd(proj, dt_raw, conv_w, conv_b, dt_bias, a_log, d_skip, norm_g, bsz, seq):
    L = SSD_CHUNK
    nc = seq // L
    t = bsz * seq
    pad = LANES - SSD_HEADS
    heads = jnp.arange(LANES)
    cols = jnp.arange(SSD_INNER) // SSD_HEAD_DIM
    e01 = (heads[:, None] == cols[None, :]).astype(BF16)
    r = jnp.arange(L)
    tri = (r[None, :] <= r[:, None]).astype(BF16)
    rowvec = lambda v: v.reshape(1, -1).astype(F32)
    return pl.pallas_call(
        _ssd_body,
        grid=(bsz, nc),
        in_specs=[
            pl.BlockSpec((L, SSD_CONV_DIM), lambda b, c: (b * nc + c, P_XBC // SSD_CONV_DIM)),
            pl.BlockSpec((L, SSD_INNER), lambda b, c: (b * nc + c, P_Z // SSD_INNER)),
            pl.BlockSpec((L, LANES), lambda b, c: (b * nc + c, 0)),
            _const_spec((SSD_CONV, SSD_CONV_DIM)),
            _const_spec((1, SSD_CONV_DIM)),
            _const_spec((1, LANES)),
            _const_spec((1, LANES)),
            _const_spec((1, SSD_INNER)),
            _const_spec((1, SSD_INNER)),
            _const_spec((LANES, SSD_INNER)),
            _const_spec((L, L)),
        ],
        out_specs=pl.BlockSpec((L, SSD_INNER), lambda b, c: (b * nc + c, 0)),
        out_shape=jax.ShapeDtypeStruct((t, SSD_INNER), BF16),
        scratch_shapes=[
            pltpu.VMEM((L + 8, SSD_CONV_DIM), F32),
            pltpu.VMEM((SSD_GROUPS, SSD_STATE, SSD_HEADS_PER_GROUP * SSD_HEAD_DIM), F32),
            pltpu.VMEM((L, SSD_INNER), F32),
        ],
        compiler_params=_params("parallel", "arbitrary"),
        name="ssd",
    )(proj, proj, dt_raw, conv_w.astype(F32), rowvec(conv_b),
      jnp.pad(rowvec(dt_bias), ((0, 0), (0, pad))), jnp.pad(rowvec(a_log), ((0, 0), (0, pad))),
      rowvec(jnp.repeat(d_skip, SSD_HEAD_DIM)), rowvec(norm_g), e01, tri)


def _merge_body(x_ref, yssd_ref, ysb_ref, u_ref, halo_ref, g0_ref, g1_ref, g2_ref,
                wssd_ref, wsb_ref, wpool_ref, wout_ref, pw_ref, ps_ref, o_ref, ext_ref, *, seq):
    tm = x_ref.shape[0]
    H = POOL_HALO
    start = (pl.program_id(0) * tm) % seq
    halo = halo_ref[...]
    ext_ref[0:H, :] = jnp.where(start == 0, jnp.zeros_like(halo), halo)
    ext_ref[H:H + tm, :] = u_ref[...]
    pos = start + lax.broadcasted_iota(jnp.int32, (tm, 1), 0)

    merged = jax.nn.sigmoid(g0_ref[...]) * _dot(yssd_ref[...], wssd_ref[...])
    merged += jax.nn.sigmoid(g1_ref[...]) * _dot(ysb_ref[...], wsb_ref[...])

    ypool = jnp.zeros((tm, D_MODEL), F32)
    for gi, win in enumerate(POOL_WINDOWS):
        cs = slice(gi * POOL_GROUP_DIM, (gi + 1) * POOL_GROUP_DIM)
        cur = ext_ref[H:H + tm, cs]
        wsum = cur
        for k in range(1, win):
            wsum = wsum + ext_ref[H - k:H - k + tm, cs]
        count = jnp.minimum(pos + 1, win).astype(F32)
        pooled = wsum / count - cur
        mixed = _dot(pooled.astype(BF16), pw_ref[gi]) * ps_ref[:, cs]
        ypool += _dot(mixed.astype(BF16), wpool_ref[cs, :])
    merged += jax.nn.sigmoid(g2_ref[...]) * ypool
    o_ref[...] = x_ref[...] + _dot(merged.astype(BF16), wout_ref[...])


def _merge(x2, y_ssd, y_sb, proj, pool_w, pool_scale, w_br_ssd, w_br_sb, w_br_pool, w_out, seq):
    t = x2.shape[0]
    tm = min(512, seq)
    hb = tm // POOL_HALO
    gate = lambda k: pl.BlockSpec((tm, D_MODEL), lambda i: (i, P_GATES // D_MODEL + k))
    return pl.pallas_call(
        functools.partial(_merge_body, seq=seq),
        grid=(t // tm,),
        in_specs=[
            pl.BlockSpec((tm, D_MODEL), lambda i: (i, 0)),
            pl.BlockSpec((tm, SSD_INNER), lambda i: (i, 0)),
            pl.BlockSpec((tm, SB_WIDTH), lambda i: (i, 0)),
            pl.BlockSpec((tm, POOL_WIDTH), lambda i: (i, P_U // POOL_WIDTH)),
            pl.BlockSpec((POOL_HALO, POOL_WIDTH), lambda i: (jnp.maximum(i * hb - 1, 0), P_U // POOL_WIDTH)),
            gate(0), gate(1), gate(2),
            _const_spec((SSD_INNER, D_MODEL)),
            _const_spec((SB_WIDTH, D_MODEL)),
            _const_spec((POOL_WIDTH, D_MODEL)),
            _const_spec((D_MODEL, D_MODEL)),
            _const_spec((POOL_GROUPS, POOL_GROUP_DIM, POOL_GROUP_DIM)),
            _const_spec((1, POOL_WIDTH)),
        ],
        out_specs=pl.BlockSpec((tm, D_MODEL), lambda i: (i, 0)),
        out_shape=jax.ShapeDtypeStruct((t, D_MODEL), F32),
        scratch_shapes=[pltpu.VMEM((tm + POOL_HALO, POOL_WIDTH), F32)],
        compiler_params=_params("parallel"),
        name="merge",
    )(x2, y_ssd, y_sb, proj, proj, proj, proj, proj,
      w_br_ssd.astype(BF16), w_br_sb.astype(BF16), w_br_pool.astype(BF16), w_out.astype(BF16),
      pool_w.astype(BF16), pool_scale.reshape(1, -1).astype(F32))


def _swiglu_into(acc_ref, xb, wg_ref, wu_ref, wd_ref, width):
    for c in range(width // FFN_CHUNK):
        cs = slice(c * FFN_CHUNK, (c + 1) * FFN_CHUNK)
        hidden = _silu(_dot(xb, wg_ref[:, cs])) * _dot(xb, wu_ref[:, cs])
        acc_ref[...] += _dot(hidden.astype(BF16), wd_ref[cs, :])


def _ffn_body(x_ref, g_ref, wg_ref, wu_ref, wd_ref, o_ref):
    x = x_ref[...]
    o_ref[...] = x
    _swiglu_into(o_ref, (_rms(x) * g_ref[...]).astype(BF16), wg_ref, wu_ref, wd_ref, FFN_DENSE)


def _ffn_dense(x2, g, w_gate, w_up, w_down):
    t = x2.shape[0]
    tm = min(512, t)
    return pl.pallas_call(
        _ffn_body,
        grid=(t // tm,),
        in_specs=[
            pl.BlockSpec((tm, D_MODEL), lambda i: (i, 0)),
            _const_spec((1, D_MODEL)),
            _const_spec((D_MODEL, FFN_DENSE)),
            _const_spec((D_MODEL, FFN_DENSE)),
            _const_spec((FFN_DENSE, D_MODEL)),
        ],
        out_specs=pl.BlockSpec((tm, D_MODEL), lambda i: (i, 0)),
        out_shape=jax.ShapeDtypeStruct((t, D_MODEL), F32),
        compiler_params=_params("parallel"),
        name="ffn_dense",
    )(x2, g, w_gate.astype(BF16), w_up.astype(BF16), w_down.astype(BF16))


def _router_body(x_ref, g_ref, rw_ref, h_ref, idx_ref, wt_ref):
    h = _rms(x_ref[...]) * g_ref[...]
    h_ref[...] = h
    rw = rw_ref[...]
    h_hi = h.astype(BF16)
    h_lo = (h - h_hi.astype(F32)).astype(BF16)
    w_hi = rw.astype(BF16)
    w_lo = (rw - w_hi.astype(F32)).astype(BF16)
    logits = _dot(h_hi, w_hi) + (_dot(h_hi, w_lo) + _dot(h_lo, w_hi)) + _dot(h_lo, w_lo)
    lane = lax.broadcasted_iota(jnp.int32, logits.shape, 1)
    lane_f = lane.astype(F32)
    logits = jnp.where(lane < N_EXPERTS, logits, -jnp.inf)
    m1 = jnp.max(logits, axis=-1, keepdims=True)
    i1 = jnp.min(jnp.where(logits == m1, lane_f, float(LANES)), axis=-1, keepdims=True)
    rest = jnp.where(lane_f == i1, -jnp.inf, logits)
    m2 = jnp.max(rest, axis=-1, keepdims=True)
    i2 = jnp.min(jnp.where(rest == m2, lane_f, float(LANES)), axis=-1, keepdims=True)
    e = jnp.exp(m2 - m1)
    w1 = 1.0 / (1.0 + e)
    idx_ref[...] = jnp.where(lane == 0, i1, jnp.where(lane == 1, i2, 0.0)).astype(jnp.int32)
    wt_ref[...] = jnp.where(lane == 0, w1, jnp.where(lane == 1, e * w1, 0.0))


def _router(x2, g, router_w):
    t = x2.shape[0]
    tm = min(512, t)
    rw = jnp.pad(router_w.astype(F32), ((0, 0), (0, LANES - N_EXPERTS)))
    row = pl.BlockSpec((tm, D_MODEL), lambda i: (i, 0))
    small = pl.BlockSpec((tm, LANES), lambda i: (i, 0))
    return pl.pallas_call(
        _router_body,
        grid=(t // tm,),
        in_specs=[row, _const_spec((1, D_MODEL)), _const_spec((D_MODEL, LANES))],
        out_specs=[row, small, small],
        out_shape=[jax.ShapeDtypeStruct((t, D_MODEL), F32),
                   jax.ShapeDtypeStruct((t, LANES), jnp.int32),
                   jax.ShapeDtypeStruct((t, LANES), F32)],
        compiler_params=_params("parallel"),
        name="router",
    )(x2, g, rw)


def _gather_rows(idx_ref, n, src_hbm, dst_ref, sem):
    def issue(r, carry):
        pltpu.make_async_copy(src_hbm.at[pl.ds(idx_ref[0, 0, r], 1)], dst_ref.at[pl.ds(r, 1)], sem).start()
        return carry

    lax.fori_loop(0, n, issue, 0, unroll=8)
    pltpu.make_async_copy(src_hbm.at[pl.ds(0, n)], dst_ref, sem).wait()


def _experts_body(bexp_ref, nused_ref, tok_ref, h_hbm, wg_ref, wu_ref, wd_ref, o_ref, x_buf, sem):
    del bexp_ref
    used = pl.program_id(0) < nused_ref[0]

    @pl.when(used)
    def _():
        _gather_rows(tok_ref, MOE_BLOCK, h_hbm, x_buf, sem)
        o_ref[...] = jnp.zeros_like(o_ref)
        _swiglu_into(o_ref, x_buf[...].astype(BF16), wg_ref, wu_ref, wd_ref, FFN_EXPERT)

    @pl.when(jnp.logical_not(used))
    def _():
        o_ref[...] = jnp.zeros_like(o_ref)


def _experts(h, slot_tok, block_exp, n_used, e_gate, e_up, e_down):
    n_blocks = block_exp.shape[0]
    wspec = lambda shape: pl.BlockSpec((None,) + shape, lambda i, be, nu: (be[i], 0, 0))
    grid_spec = pltpu.PrefetchScalarGridSpec(
        num_scalar_prefetch=2,
        grid=(n_blocks,),
        in_specs=[
            pl.BlockSpec((1, 1, MOE_BLOCK), lambda i, be, nu: (i, 0, 0), memory_space=pltpu.SMEM),
            pl.BlockSpec(memory_space=pl.ANY),
            wspec((D_MODEL, FFN_EXPERT)), wspec((D_MODEL, FFN_EXPERT)), wspec((FFN_EXPERT, D_MODEL)),
        ],
        out_specs=pl.BlockSpec((MOE_BLOCK, D_MODEL), lambda i, be, nu: (i, 0)),
        scratch_shapes=[pltpu.VMEM((MOE_BLOCK, D_MODEL), F32), pltpu.SemaphoreType.DMA],
    )
    return pl.pallas_call(
        _experts_body,
        grid_spec=grid_spec,
        out_shape=jax.ShapeDtypeStruct((n_blocks * MOE_BLOCK, D_MODEL), F32),
        compiler_params=_params("arbitrary"),
        name="experts",
    )(block_exp, n_used, slot_tok.reshape(n_blocks, 1, MOE_BLOCK), h,
      e_gate.astype(BF16), e_up.astype(BF16), e_down.astype(BF16))


def _combine_body(pos_ref, x_ref, wt_ref, yb_hbm, o_ref, buf_ref, sem):
    tm = x_ref.shape[0]
    _gather_rows(pos_ref, TOP_K * tm, yb_hbm, buf_ref, sem)
    wt = wt_ref[...]
    o_ref[...] = x_ref[...] + (wt[:, 0:1] * buf_ref[0:tm, :] + wt[:, 1:2] * buf_ref[tm:2 * tm, :])


def _combine(x2, wt, pos, yb):
    t = x2.shape[0]
    tm = min(256, t)
    nt = t // tm
    pos_tiles = pos.reshape(nt, tm, TOP_K).transpose(0, 2, 1).reshape(nt, 1, TOP_K * tm)
    return pl.pallas_call(
        _combine_body,
        grid=(nt,),
        in_specs=[
            pl.BlockSpec((1, 1, TOP_K * tm), lambda i: (i, 0, 0), memory_space=pltpu.SMEM),
            pl.BlockSpec((tm, D_MODEL), lambda i: (i, 0)),
            pl.BlockSpec((tm, LANES), lambda i: (i, 0)),
            pl.BlockSpec(memory_space=pl.ANY),
        ],
        out_specs=pl.BlockSpec((tm, D_MODEL), lambda i: (i, 0)),
        out_shape=jax.ShapeDtypeStruct((t, D_MODEL), F32),
        scratch_shapes=[pltpu.VMEM((TOP_K * tm, D_MODEL), F32), pltpu.SemaphoreType.DMA],
        compiler_params=_params("arbitrary"),
        name="combine",
    )(pos_tiles, x2, wt, yb)


def _moe(x2, g, router_w, e_gate, e_up, e_down):
    t = x2.shape[0]
    h, idx, wt = _router(x2, g, router_w)
    n_assign = t * TOP_K
    cap = -(-n_assign // MOE_BLOCK) * MOE_BLOCK + N_EXPERTS * MOE_BLOCK
    n_blocks = cap // MOE_BLOCK
    expert_flat = idx[:, :TOP_K].reshape(-1)
    onehot = (expert_flat[:, None] == jnp.arange(N_EXPERTS, dtype=jnp.int32)[None, :]).astype(jnp.int32)
    running = jnp.cumsum(onehot, axis=0)
    counts = running[-1]
    rank = jnp.sum(running * onehot, axis=1) - 1
    padded = (counts + MOE_BLOCK - 1) // MOE_BLOCK * MOE_BLOCK
    pends = jnp.cumsum(padded)
    pstarts = pends - padded
    dest = (pstarts[expert_flat] + rank).astype(jnp.int32)
    token_flat = jnp.arange(n_assign, dtype=jnp.int32) // TOP_K
    slot_tok = jnp.zeros((cap,), jnp.int32).at[dest].set(token_flat)
    block_exp = jnp.minimum(jnp.searchsorted(pends, jnp.arange(n_blocks) * MOE_BLOCK, side="right"),
                            N_EXPERTS - 1).astype(jnp.int32)
    n_used = (pends[-1:] // MOE_BLOCK).astype(jnp.int32)
    yb = _experts(h, slot_tok, block_exp, n_used, e_gate, e_up, e_down)
    return _combine(x2, wt, dest.reshape(t, TOP_K), yb)


def _permute_w_in(w):
    main = jnp.concatenate([
        w[:, COL_Z:COL_XBC], w[:, :COL_Z], w[:, COL_DT:COL_Q], w[:, COL_Q:COL_K], w[:, COL_K:COL_V],
        w[:, COL_V:COL_POOL], w[:, COL_POOL:]], axis=1).astype(BF16)
    dt = jnp.pad(w[:, COL_XBC:COL_DT], ((0, 0), (0, LANES - SSD_HEADS))).astype(BF16)
    return main, dt


def kernel(x, mix_norm_g, w_in, conv_w, conv_b, dt_bias, a_log, d_skip, ssd_norm_g, q_norm_g, k_norm_g, pool_w,
           pool_scale, w_br_ssd, w_br_sb, w_br_pool, w_out, ffn_norm_g, ffn_w_gate, ffn_w_up, ffn_w_down,
           router_w, moe_w_gate, moe_w_up, moe_w_down):
    bsz, seq, d = x.shape
    depth = w_in.shape[0]
    x2 = x.reshape(bsz * seq, d)
    rowvec = lambda v: v.reshape(1, -1).astype(F32)
    for layer in range(depth):
        w_main, w_dt = _permute_w_in(w_in[layer])
        proj, dt_raw = _inproj(x2, rowvec(mix_norm_g[layer]), w_main, w_dt)
        qn, kn, vn = _qkv_prep(proj, rowvec(q_norm_g[layer]), rowvec(k_norm_g[layer]), bsz, seq)
        y_sb = _sb_attention(qn, kn, vn).reshape(bsz * seq, SB_WIDTH)
        y_ssd = _ssd(proj, dt_raw, conv_w[layer], conv_b[layer], dt_bias[layer], a_log[layer], d_skip[layer],
                     ssd_norm_g[layer], bsz, seq)
        x2 = _merge(x2, y_ssd, y_sb, proj, pool_w[layer], pool_scale[layer], w_br_ssd[layer], w_br_sb[layer],
                    w_br_pool[layer], w_out[layer], seq)
        i = layer // 2
        if layer % 2 == 0:
            x2 = _ffn_dense(x2, rowvec(ffn_norm_g[layer]), ffn_w_gate[i], ffn_w_up[i], ffn_w_down[i])
        else:
            x2 = _moe(x2, rowvec(ffn_norm_g[layer]), router_w[i], moe_w_gate[i], moe_w_up[i], moe_w_down[i])
    return x2.reshape(bsz, seq, d)
```

```python
import functools
import math

import jax
import jax.numpy as jnp
from jax import lax
from jax.experimental import pallas as pl
from jax.experimental.pallas import tpu as pltpu

F32 = jnp.float32
BF16 = jnp.bfloat16

D_MODEL = 1024
EPS = 1e-6

SSD_INNER = 1024
SSD_HEAD_DIM = 64
SSD_HEADS = 16
SSD_GROUPS = 4
SSD_HEADS_PER_GROUP = 4
SSD_STATE = 128
SSD_CONV = 4
SSD_CHUNK = 128
SSD_CONV_DIM = SSD_INNER + 2 * SSD_GROUPS * SSD_STATE

SB_HEADS = 4
SB_HEAD_DIM = 128
SB_WIDTH = SB_HEADS * SB_HEAD_DIM
SB_TILE = 256
SB_CHAINS = 2

POOL_WINDOWS = (2, 4, 8, 16)
POOL_GROUPS = 4
POOL_WIDTH = 512
POOL_GROUP_DIM = 128
POOL_HALO = 16

N_BRANCHES = 3
FFN_DENSE = 2816
N_EXPERTS = 8
TOP_K = 2
FFN_EXPERT = 1792
MOE_BLOCK = 512

COL_Z = SSD_INNER
COL_XBC = COL_Z + SSD_CONV_DIM
COL_DT = COL_XBC + SSD_HEADS
COL_Q = COL_DT + SB_WIDTH
COL_K = COL_Q + SB_WIDTH
COL_V = COL_K + SB_WIDTH
COL_POOL = COL_V + POOL_WIDTH

P_XBC = 0
P_Z = 2048
P_Q = 3072
P_K = 3584
P_V = 4096
P_U = 4608
P_GATES = 5120
P_WIDTH = 8192

PROJ_DTYPE = BF16
LANES = 128
VMEM_LIMIT = 56 * 1024 * 1024
FFN_CHUNK = 256
SB_STOP = -110.0


def _params(*sem):
    return pltpu.CompilerParams(dimension_semantics=sem, vmem_limit_bytes=VMEM_LIMIT)


def _const_spec(shape):
    nd = len(shape)
    return pl.BlockSpec(shape, lambda *_: (0,) * nd)


def _split3(x):
    a = x.astype(BF16)
    r = x - a.astype(F32)
    b = r.astype(BF16)
    c = (r - b.astype(F32)).astype(BF16)
    return a, b, c


def _dot(a, b):
    return jnp.dot(a, b, preferred_element_type=F32)


def _dot_nt(a, b):
    return lax.dot_general(a, b, (((1,), (1,)), ((), ())), preferred_element_type=F32)


def _dot_f32_by_01(x, e01):
    a, b, c = _split3(x)
    return _dot(a, e01) + _dot(b, e01) + _dot(c, e01)


def _dot_01_by_f32(t01, x):
    a, b, c = _split3(x)
    return _dot(t01, a) + _dot(t01, b) + _dot(t01, c)


def _rms(x):
    return x * lax.rsqrt(jnp.mean(x * x, axis=-1, keepdims=True) + EPS)


def _silu(x):
    return x * jax.nn.sigmoid(x)


def _softplus(x):
    return jnp.maximum(x, 0.0) + jnp.log1p(jnp.exp(-jnp.abs(x)))


def _inproj_body(x_ref, g_ref, w_ref, wdt_ref, o_ref, dt_ref, xn_ref):
    @pl.when(pl.program_id(1) == 0)
    def _():
        xb = (_rms(x_ref[...]) * g_ref[...]).astype(BF16)
        xn_ref[...] = xb
        dt_ref[...] = _dot(xb, wdt_ref[...])

    o_ref[...] = _dot(xn_ref[...], w_ref[...]).astype(o_ref.dtype)


def _inproj(x2, g, w_main, w_dt):
    t = x2.shape[0]
    tm = min(1024, t)
    tn = 1024
    return pl.pallas_call(
        _inproj_body,
        grid=(t // tm, P_WIDTH // tn),
        in_specs=[
            pl.BlockSpec((tm, D_MODEL), lambda i, j: (i, 0)),
            _const_spec((1, D_MODEL)),
            pl.BlockSpec((D_MODEL, tn), lambda i, j: (0, j)),
            _const_spec((D_MODEL, LANES)),
        ],
        out_specs=[
            pl.BlockSpec((tm, tn), lambda i, j: (i, j)),
            pl.BlockSpec((tm, LANES), lambda i, j: (i, 0)),
        ],
        out_shape=[
            jax.ShapeDtypeStruct((t, P_WIDTH), PROJ_DTYPE),
            jax.ShapeDtypeStruct((t, LANES), F32),
        ],
        scratch_shapes=[pltpu.VMEM((tm, D_MODEL), BF16)],
        compiler_params=_params("parallel", "arbitrary"),
        name="inproj",
    )(x2, g, w_main, w_dt)


def _qkv_body(q_ref, k_ref, v_ref, qg_ref, kg_ref, qo_ref, ko_ref, vo_ref):
    scale = 1.0 / math.sqrt(SB_HEAD_DIM)
    for h in range(SB_HEADS):
        sl = slice(h * SB_HEAD_DIM, (h + 1) * SB_HEAD_DIM)
        qo_ref[0, h] = ((_rms(q_ref[:, sl].astype(F32)) * qg_ref[...]) * scale).astype(BF16)
        ko_ref[0, h] = (_rms(k_ref[:, sl].astype(F32)) * kg_ref[...]).astype(BF16)
        vo_ref[0, h] = v_ref[:, sl].astype(BF16)


def _qkv_prep(proj, qg, kg, bsz, seq):
    ts = min(512, seq)
    ns = seq // ts
    col = lambda c: pl.BlockSpec((ts, SB_WIDTH), lambda b, s: (b * ns + s, c))
    out_spec = pl.BlockSpec((1, SB_HEADS, ts, SB_HEAD_DIM), lambda b, s: (b, 0, s, 0))
    out_shape = jax.ShapeDtypeStruct((bsz, SB_HEADS, seq, SB_HEAD_DIM), BF16)
    return pl.pallas_call(
        _qkv_body,
        grid=(bsz, ns),
        in_specs=[col(P_Q // SB_WIDTH), col(P_K // SB_WIDTH), col(P_V // SB_WIDTH),
                  _const_spec((1, SB_HEAD_DIM)), _const_spec((1, SB_HEAD_DIM))],
        out_specs=[out_spec, out_spec, out_spec],
        out_shape=[out_shape, out_shape, out_shape],
        compiler_params=_params("parallel", "parallel"),
        name="qkv_prep",
    )(proj, proj, proj, qg, kg)


def _sb_body(q_ref, k_ref, v_ref, uo_ref, o_ref, carry_ref, acc_ref):
    t = SB_TILE
    first = pl.program_id(2) * SB_CHAINS
    uo = uo_ref[...]
    row = lax.broadcasted_iota(jnp.int32, (t, t), 0)
    col = lax.broadcasted_iota(jnp.int32, (t, t), 1)
    mask = col < row

    def visit(c, kb, diag):
        start = pl.multiple_of(jnp.maximum(kb, 0) * t, t)
        q = q_ref[0, 0, c * t:(c + 1) * t, :]
        k = k_ref[0, 0, pl.ds(start, t), :]
        v = v_ref[0, 0, pl.ds(start, t), :]
        z = _dot_nt(q, k)
        neg_log_rest = jnp.maximum(z, 0.0) + jnp.log(1.0 + jnp.exp(-jnp.abs(z)))
        log_beta = z - neg_log_rest
        if diag:
            neg_log_rest = jnp.where(mask, neg_log_rest, 0.0)
        hi = neg_log_rest.astype(BF16)
        lo = (neg_log_rest - hi.astype(F32)).astype(BF16)
        sr = _dot(hi, uo) + _dot(lo, uo)
        if diag:
            w = jnp.where(mask, jnp.exp(log_beta - sr[:, :t]), 0.0)
            carry = -sr[:, t:]
            acc_ref[c] = _dot(w.astype(BF16), v)
        else:
            prev = carry_ref[c]
            w = jnp.exp(log_beta - sr[:, :t] + jnp.concatenate([prev] * (t // LANES), axis=1))
            w = jnp.where(kb >= 0, w, 0.0)
            carry = prev - sr[:, t:]
            acc_ref[c] += _dot(w.astype(BF16), v)
        carry_ref[c] = carry
        return jnp.max(carry)

    def unfinished(j, maxima):
        need = [jnp.logical_and(first + c - j >= 0, maxima[c] > SB_STOP) for c in range(SB_CHAINS)]
        return functools.reduce(jnp.logical_or, need).astype(jnp.int32)

    maxima = [visit(c, first + c, True) for c in range(SB_CHAINS)]

    def body(state):
        j, _ = state
        maxima = [visit(c, first + c - j, False) for c in range(SB_CHAINS)]
        return j + 1, unfinished(j + 1, maxima)

    lax.while_loop(lambda state: state[1] > 0, body, (jnp.int32(1), unfinished(1, maxima)))
    for c in range(SB_CHAINS):
        o_ref[0, c * t:(c + 1) * t, :] = acc_ref[c].astype(o_ref.dtype)


def _sb_attention(qn, kn, vn):
    bsz, _, seq, _ = qn.shape
    t = SB_TILE
    tq = t * SB_CHAINS
    assert seq % tq == 0
    r = jnp.arange(t)
    upper = (r[:, None] > r[None, :])
    uo = jnp.concatenate([upper, jnp.ones((t, LANES), bool)], axis=1).astype(BF16)
    kv_spec = pl.BlockSpec((1, 1, seq, SB_HEAD_DIM), lambda b, h, i: (b, h, 0, 0))
    return pl.pallas_call(
        _sb_body,
        grid=(bsz, SB_HEADS, seq // tq),
        in_specs=[pl.BlockSpec((1, 1, tq, SB_HEAD_DIM), lambda b, h, i: (b, h, i, 0)),
                  kv_spec, kv_spec, _const_spec((t, t + LANES))],
        out_specs=pl.BlockSpec((1, tq, SB_HEAD_DIM), lambda b, h, i: (b, i, h)),
        out_shape=jax.ShapeDtypeStruct((bsz, seq, SB_WIDTH), BF16),
        scratch_shapes=[pltpu.VMEM((SB_CHAINS, t, LANES), F32), pltpu.VMEM((SB_CHAINS, t, SB_HEAD_DIM), F32)],
        compiler_params=_params("parallel", "parallel", "arbitrary"),
        name="sb_attention",
    )(qn, kn, vn, uo)


def _ssd_body(xbc_ref, z_ref, dt_ref, cw_ref, cb_ref, dtb_ref, alog_ref, dskip_ref, ng_ref, e_ref, tri_ref,
              o_ref, buf_ref, st_ref, y_ref):
    L = SSD_CHUNK
    G, R, P, N = SSD_GROUPS, SSD_HEADS_PER_GROUP, SSD_HEAD_DIM, SSD_STATE
    tail = 8

    @pl.when(pl.program_id(1) == 0)
    def _():
        buf_ref[0:tail, :] = jnp.zeros((tail, SSD_CONV_DIM), F32)
        st_ref[...] = jnp.zeros_like(st_ref)

    buf_ref[tail:tail + L, :] = xbc_ref[...].astype(F32)

    def conv_silu(lo, hi):
        acc = cb_ref[:, lo:hi]
        for k in range(SSD_CONV):
            off = tail - (SSD_CONV - 1) + k
            acc = acc + cw_ref[k:k + 1, lo:hi] * buf_ref[off:off + L, lo:hi]
        return _silu(acc)

    xs = conv_silu(0, SSD_INNER)
    b_in = conv_silu(SSD_INNER, SSD_INNER + G * N)
    c_in = conv_silu(SSD_INNER + G * N, SSD_CONV_DIM)
    buf_ref[0:tail, :] = buf_ref[L:L + tail, :]

    e01 = e_ref[...]
    dt = _softplus(dt_ref[...] + dtb_ref[...])
    da = dt * (-jnp.exp(alog_ref[...]))
    a_cs = _dot_01_by_f32(tri_ref[...], da)
    a_cs_t = a_cs.T
    a_full = _dot_f32_by_01(a_cs, e01)
    dt_full = _dot_f32_by_01(dt, e01)
    a_last = a_full[L - 1:L, :]
    ea_full = jnp.exp(a_full)
    x_dt = xs * dt_full
    xw = (x_dt * jnp.exp(a_last - a_full)).astype(BF16)
    ea_last = jnp.exp(a_last)

    row = lax.broadcasted_iota(jnp.int32, (L, L), 0)
    col = lax.broadcasted_iota(jnp.int32, (L, L), 1)
    causal = col <= row
    lane = lax.broadcasted_iota(jnp.int32, (L, LANES), 1)
    first_head = lane < P

    for g in range(G):
        gs = slice(g * N, (g + 1) * N)
        cg = c_in[:, gs].astype(BF16)
        bg_f32 = b_in[:, gs]
        cb = _dot_nt(cg, bg_f32.astype(BF16))
        for pair in range(R // 2):
            ms = []
            for r in range(2):
                h = g * R + pair * 2 + r
                seg = a_cs[:, h:h + 1] - a_cs_t[h:h + 1, :]
                decay = jnp.exp(jnp.where(causal, seg, -jnp.inf))
                ms.append((cb * decay).astype(BF16))
            c0 = (g * R + pair * 2) * P
            xp = x_dt[:, c0:c0 + LANES]
            rhs = jnp.concatenate([jnp.where(first_head, xp, 0.0).astype(BF16),
                                   jnp.where(first_head, 0.0, xp).astype(BF16)], axis=0)
            y_ref[:, c0:c0 + LANES] = _dot(jnp.concatenate(ms, axis=1), rhs)
        cs = slice(g * R * P, (g + 1) * R * P)
        state = st_ref[g]
        y_ref[:, cs] += _dot(cg, state.astype(BF16)) * ea_full[:, cs]
        st_ref[g] = state * ea_last[:, cs] + _dot(bg_f32.T.astype(BF16), xw[:, cs])

    y = y_ref[...] + xs * dskip_ref[...]
    y = y * _silu(z_ref[...].astype(F32))
    gw = SSD_INNER // G
    for g in range(G):
        cs = slice(g * gw, (g + 1) * gw)
        o_ref[:, cs] = (_rms(y[:, cs]) * ng_ref[:, cs]).astype(o_ref.dtype)


def _ssd(proj, dt_raw, conv_w, conv_b, dt_bias, a_log, d_skip, norm_g, bsz, seq):
    L = SSD_CHUNK
    nc = seq // L
    t = bsz * seq
    pad = LANES - SSD_HEADS
    heads = jnp.arange(LANES)
    cols = jnp.arange(SSD_INNER) // SSD_HEAD_DIM
    e01 = (heads[:, None] == cols[None, :]).astype(BF16)
    r = jnp.arange(L)
    tri = (r[None, :] <= r[:, None]).astype(BF16)
    rowvec = lambda v: v.reshape(1, -1).astype(F32)
    return pl.pallas_call(
        _ssd_body,
        grid=(bsz, nc),
        in_specs=[
            pl.BlockSpec((L, SSD_CONV_DIM), lambda b, c: (b * nc + c, P_XBC // SSD_CONV_DIM)),
            pl.BlockSpec((L, SSD_INNER), lambda b, c: (b * nc + c, P_Z // SSD_INNER)),
            pl.BlockSpec((L, LANES), lambda b, c: (b * nc + c, 0)),
            _const_spec((SSD_CONV, SSD_CONV_DIM)),
            _const_spec((1, SSD_CONV_DIM)),
            _const_spec((1, LANES)),
            _const_spec((1, LANES)),
            _const_spec((1, SSD_INNER)),
            _const_spec((1, SSD_INNER)),
            _const_spec((LANES, SSD_INNER)),
            _const_spec((L, L)),
        ],
        out_specs=pl.BlockSpec((L, SSD_INNER), lambda b, c: (b * nc + c, 0)),
        out_shape=jax.ShapeDtypeStruct((t, SSD_INNER), BF16),
        scratch_shapes=[
            pltpu.VMEM((L + 8, SSD_CONV_DIM), F32),
            pltpu.VMEM((SSD_GROUPS, SSD_STATE, SSD_HEADS_PER_GROUP * SSD_HEAD_DIM), F32),
            pltpu.VMEM((L, SSD_INNER), F32),
        ],
        compiler_params=_params("parallel", "arbitrary"),
        name="ssd",
    )(proj, proj, dt_raw, conv_w.astype(F32), rowvec(conv_b),
      jnp.pad(rowvec(dt_bias), ((0, 0), (0, pad))), jnp.pad(rowvec(a_log), ((0, 0), (0, pad))),
      rowvec(jnp.repeat(d_skip, SSD_HEAD_DIM)), rowvec(norm_g), e01, tri)


def _merge_body(x_ref, yssd_ref, ysb_ref, u_ref, halo_ref, g0_ref, g1_ref, g2_ref,
                wssd_ref, wsb_ref, wpool_ref, wout_ref, pw_ref, ps_ref, o_ref, ext_ref, *, seq):
    tm = x_ref.shape[0]
    H = POOL_HALO
    start = (pl.program_id(0) * tm) % seq
    halo = halo_ref[...].astype(F32)
    ext_ref[0:H, :] = jnp.where(start == 0, jnp.zeros_like(halo), halo)
    ext_ref[H:H + tm, :] = u_ref[...].astype(F32)
    pos = start + lax.broadcasted_iota(jnp.int32, (tm, 1), 0)

    merged = jax.nn.sigmoid(g0_ref[...].astype(F32)) * _dot(yssd_ref[...], wssd_ref[...])
    merged += jax.nn.sigmoid(g1_ref[...].astype(F32)) * _dot(ysb_ref[...], wsb_ref[...])

    ypool = jnp.zeros((tm, D_MODEL), F32)
    for gi, win in enumerate(POOL_WINDOWS):
        cs = slice(gi * POOL_GROUP_DIM, (gi + 1) * POOL_GROUP_DIM)
        cur = ext_ref[H:H + tm, cs]
        wsum = cur
        for k in range(1, win):
            wsum = wsum + ext_ref[H - k:H - k + tm, cs]
        count = jnp.minimum(pos + 1, win).astype(F32)
        pooled = wsum / count - cur
        mixed = _dot(pooled.astype(BF16), pw_ref[gi]) * ps_ref[:, cs]
        ypool += _dot(mixed.astype(BF16), wpool_ref[cs, :])
    merged += jax.nn.sigmoid(g2_ref[...].astype(F32)) * ypool
    o_ref[...] = x_ref[...] + _dot(merged.astype(BF16), wout_ref[...])


def _merge(x2, y_ssd, y_sb, proj, pool_w, pool_scale, w_br_ssd, w_br_sb, w_br_pool, w_out, seq):
    t = x2.shape[0]
    tm = min(512, seq)
    hb = tm // POOL_HALO
    gate = lambda k: pl.BlockSpec((tm, D_MODEL), lambda i: (i, P_GATES // D_MODEL + k))
    return pl.pallas_call(
        functools.partial(_merge_body, seq=seq),
        grid=(t // tm,),
        in_specs=[
            pl.BlockSpec((tm, D_MODEL), lambda i: (i, 0)),
            pl.BlockSpec((tm, SSD_INNER), lambda i: (i, 0)),
            pl.BlockSpec((tm, SB_WIDTH), lambda i: (i, 0)),
            pl.BlockSpec((tm, POOL_WIDTH), lambda i: (i, P_U // POOL_WIDTH)),
            pl.BlockSpec((POOL_HALO, POOL_WIDTH), lambda i: (jnp.maximum(i * hb - 1, 0), P_U // POOL_WIDTH)),
            gate(0), gate(1), gate(2),
            _const_spec((SSD_INNER, D_MODEL)),
            _const_spec((SB_WIDTH, D_MODEL)),
            _const_spec((POOL_WIDTH, D_MODEL)),
            _const_spec((D_MODEL, D_MODEL)),
            _const_spec((POOL_GROUPS, POOL_GROUP_DIM, POOL_GROUP_DIM)),
            _const_spec((1, POOL_WIDTH)),
        ],
        out_specs=pl.BlockSpec((tm, D_MODEL), lambda i: (i, 0)),
        out_shape=jax.ShapeDtypeStruct((t, D_MODEL), F32),
        scratch_shapes=[pltpu.VMEM((tm + POOL_HALO, POOL_WIDTH), F32)],
        compiler_params=_params("parallel"),
        name="merge",
    )(x2, y_ssd, y_sb, proj, proj, proj, proj, proj,
      w_br_ssd.astype(BF16), w_br_sb.astype(BF16), w_br_pool.astype(BF16), w_out.astype(BF16),
      pool_w.astype(BF16), pool_scale.reshape(1, -1).astype(F32))


def _swiglu_into(acc_ref, xb, wg_ref, wu_ref, wd_ref, width):
    for c in range(width // FFN_CHUNK):
        cs = slice(c * FFN_CHUNK, (c + 1) * FFN_CHUNK)
        hidden = _silu(_dot(xb, wg_ref[:, cs])) * _dot(xb, wu_ref[:, cs])
        acc_ref[...] += _dot(hidden.astype(BF16), wd_ref[cs, :])


def _ffn_body(x_ref, g_ref, wg_ref, wu_ref, wd_ref, o_ref):
    x = x_ref[...]
    o_ref[...] = x
    _swiglu_into(o_ref, (_rms(x) * g_ref[...]).astype(BF16), wg_ref, wu_ref, wd_ref, FFN_DENSE)


def _ffn_dense(x2, g, w_gate, w_up, w_down):
    t = x2.shape[0]
    tm = min(512, t)
    return pl.pallas_call(
        _ffn_body,
        grid=(t // tm,),
        in_specs=[
            pl.BlockSpec((tm, D_MODEL), lambda i: (i, 0)),
            _const_spec((1, D_MODEL)),
            _const_spec((D_MODEL, FFN_DENSE)),
            _const_spec((D_MODEL, FFN_DENSE)),
            _const_spec((FFN_DENSE, D_MODEL)),
        ],
        out_specs=pl.BlockSpec((tm, D_MODEL), lambda i: (i, 0)),
        out_shape=jax.ShapeDtypeStruct((t, D_MODEL), F32),
        compiler_params=_params("parallel"),
        name="ffn_dense",
    )(x2, g, w_gate.astype(BF16), w_up.astype(BF16), w_down.astype(BF16))


def _router_body(x_ref, g_ref, rw_ref, h_ref, idx_ref, wt_ref):
    h = _rms(x_ref[...]) * g_ref[...]
    h_ref[...] = h
    rw = rw_ref[...]
    h_hi = h.astype(BF16)
    h_lo = (h - h_hi.astype(F32)).astype(BF16)
    w_hi = rw.astype(BF16)
    w_lo = (rw - w_hi.astype(F32)).astype(BF16)
    logits = _dot(h_hi, w_hi) + (_dot(h_hi, w_lo) + _dot(h_lo, w_hi)) + _dot(h_lo, w_lo)
    lane = lax.broadcasted_iota(jnp.int32, logits.shape, 1)
    lane_f = lane.astype(F32)
    logits = jnp.where(lane < N_EXPERTS, logits, -jnp.inf)
    m1 = jnp.max(logits, axis=-1, keepdims=True)
    i1 = jnp.min(jnp.where(logits == m1, lane_f, float(LANES)), axis=-1, keepdims=True)
    rest = jnp.where(lane_f == i1, -jnp.inf, logits)
    m2 = jnp.max(rest, axis=-1, keepdims=True)
    i2 = jnp.min(jnp.where(rest == m2, lane_f, float(LANES)), axis=-1, keepdims=True)
    e = jnp.exp(m2 - m1)
    w1 = 1.0 / (1.0 + e)
    idx_ref[...] = jnp.where(lane == 0, i1, jnp.where(lane == 1, i2, 0.0)).astype(jnp.int32)
    wt_ref[...] = jnp.where(lane == 0, w1, jnp.where(lane == 1, e * w1, 0.0))


def _router(x2, g, router_w):
    t = x2.shape[0]
    tm = min(512, t)
    rw = jnp.pad(router_w.astype(F32), ((0, 0), (0, LANES - N_EXPERTS)))
    row = pl.BlockSpec((tm, D_MODEL), lambda i: (i, 0))
    small = pl.BlockSpec((tm, LANES), lambda i: (i, 0))
    return pl.pallas_call(
        _router_body,
        grid=(t // tm,),
        in_specs=[row, _const_spec((1, D_MODEL)), _const_spec((D_MODEL, LANES))],
        out_specs=[row, small, small],
        out_shape=[jax.ShapeDtypeStruct((t, D_MODEL), F32),
                   jax.ShapeDtypeStruct((t, LANES), jnp.int32),
                   jax.ShapeDtypeStruct((t, LANES), F32)],
        compiler_params=_params("parallel"),
        name="router",
    )(x2, g, rw)


def _gather_rows(idx_ref, n, src_hbm, dst_ref, sem):
    def issue(r, carry):
        pltpu.make_async_copy(src_hbm.at[pl.ds(idx_ref[0, 0, r], 1)], dst_ref.at[pl.ds(r, 1)], sem).start()
        return carry

    lax.fori_loop(0, n, issue, 0, unroll=8)
    pltpu.make_async_copy(src_hbm.at[pl.ds(0, n)], dst_ref, sem).wait()


def _experts_body(bexp_ref, nused_ref, tok_ref, h_hbm, wg_ref, wu_ref, wd_ref, o_ref, x_buf, sem):
    del bexp_ref
    used = pl.program_id(0) < nused_ref[0]

    @pl.when(used)
    def _():
        _gather_rows(tok_ref, MOE_BLOCK, h_hbm, x_buf, sem)
        o_ref[...] = jnp.zeros_like(o_ref)
        _swiglu_into(o_ref, x_buf[...].astype(BF16), wg_ref, wu_ref, wd_ref, FFN_EXPERT)

    @pl.when(jnp.logical_not(used))
    def _():
        o_ref[...] = jnp.zeros_like(o_ref)


def _experts(h, slot_tok, block_exp, n_used, e_gate, e_up, e_down):
    n_blocks = block_exp.shape[0]
    wspec = lambda shape: pl.BlockSpec((None,) + shape, lambda i, be, nu: (be[i], 0, 0))
    grid_spec = pltpu.PrefetchScalarGridSpec(
        num_scalar_prefetch=2,
        grid=(n_blocks,),
        in_specs=[
            pl.BlockSpec((1, 1, MOE_BLOCK), lambda i, be, nu: (i, 0, 0), memory_space=pltpu.SMEM),
            pl.BlockSpec(memory_space=pl.ANY),
            wspec((D_MODEL, FFN_EXPERT)), wspec((D_MODEL, FFN_EXPERT)), wspec((FFN_EXPERT, D_MODEL)),
        ],
        out_specs=pl.BlockSpec((MOE_BLOCK, D_MODEL), lambda i, be, nu: (i, 0)),
        scratch_shapes=[pltpu.VMEM((MOE_BLOCK, D_MODEL), F32), pltpu.SemaphoreType.DMA],
    )
    return pl.pallas_call(
        _experts_body,
        grid_spec=grid_spec,
        out_shape=jax.ShapeDtypeStruct((n_blocks * MOE_BLOCK, D_MODEL), F32),
        compiler_params=_params("arbitrary"),
        name="experts",
    )(block_exp, n_used, slot_tok.reshape(n_blocks, 1, MOE_BLOCK), h,
      e_gate.astype(BF16), e_up.astype(BF16), e_down.astype(BF16))


def _combine_body(pos_ref, x_ref, wt_ref, yb_hbm, o_ref, buf_ref, sem):
    tm = x_ref.shape[0]
    _gather_rows(pos_ref, TOP_K * tm, yb_hbm, buf_ref, sem)
    wt = wt_ref[...]
    o_ref[...] = x_ref[...] + (wt[:, 0:1] * buf_ref[0:tm, :] + wt[:, 1:2] * buf_ref[tm:2 * tm, :])


def _combine(x2, wt, pos, yb):
    t = x2.shape[0]
    tm = min(256, t)
    nt = t // tm
    pos_tiles = pos.reshape(nt, tm, TOP_K).transpose(0, 2, 1).reshape(nt, 1, TOP_K * tm)
    return pl.pallas_call(
        _combine_body,
        grid=(nt,),
        in_specs=[
            pl.BlockSpec((1, 1, TOP_K * tm), lambda i: (i, 0, 0), memory_space=pltpu.SMEM),
            pl.BlockSpec((tm, D_MODEL), lambda i: (i, 0)),
            pl.BlockSpec((tm, LANES), lambda i: (i, 0)),
            pl.BlockSpec(memory_space=pl.ANY),
        ],
        out_specs=pl.BlockSpec((tm, D_MODEL), lambda i: (i, 0)),
        out_shape=jax.ShapeDtypeStruct((t, D_MODEL), F32),
        scratch_shapes=[pltpu.VMEM((TOP_K * tm, D_MODEL), F32), pltpu.SemaphoreType.DMA],
        compiler_params=_params("arbitrary"),
        name="combine",
    )(pos_tiles, x2, wt, yb)


def _moe(x2, g, router_w, e_gate, e_up, e_down):
    t = x2.shape[0]
    h, idx, wt = _router(x2, g, router_w)
    n_assign = t * TOP_K
    cap = -(-n_assign // MOE_BLOCK) * MOE_BLOCK + N_EXPERTS * MOE_BLOCK
    n_blocks = cap // MOE_BLOCK
    expert_flat = idx[:, :TOP_K].reshape(-1)
    onehot = (expert_flat[:, None] == jnp.arange(N_EXPERTS, dtype=jnp.int32)[None, :]).astype(jnp.int32)
    running = jnp.cumsum(onehot, axis=0)
    counts = running[-1]
    rank = jnp.sum(running * onehot, axis=1) - 1
    padded = (counts + MOE_BLOCK - 1) // MOE_BLOCK * MOE_BLOCK
    pends = jnp.cumsum(padded)
    pstarts = pends - padded
    dest = (pstarts[expert_flat] + rank).astype(jnp.int32)
    token_flat = jnp.arange(n_assign, dtype=jnp.int32) // TOP_K
    slot_tok = jnp.zeros((cap,), jnp.int32).at[dest].set(token_flat)
    block_exp = jnp.minimum(jnp.searchsorted(pends, jnp.arange(n_blocks) * MOE_BLOCK, side="right"),
                            N_EXPERTS - 1).astype(jnp.int32)
    n_used = (pends[-1:] // MOE_BLOCK).astype(jnp.int32)
    yb = _experts(h, slot_tok, block_exp, n_used, e_gate, e_up, e_down)
    return _combine(x2, wt, dest.reshape(t, TOP_K), yb)


def _permute_w_in(w):
    main = jnp.concatenate([
        w[:, COL_Z:COL_XBC], w[:, :COL_Z], w[:, COL_DT:COL_Q], w[:, COL_Q:COL_K], w[:, COL_K:COL_V],
        w[:, COL_V:COL_POOL], w[:, COL_POOL:]], axis=1).astype(BF16)
    dt = jnp.pad(w[:, COL_XBC:COL_DT], ((0, 0), (0, LANES - SSD_HEADS))).astype(BF16)
    return main, dt


def kernel(x, mix_norm_g, w_in, conv_w, conv_b, dt_bias, a_log, d_skip, ssd_norm_g, q_norm_g, k_norm_g, pool_w,
           pool_scale, w_br_ssd, w_br_sb, w_br_pool, w_out, ffn_norm_g, ffn_w_gate, ffn_w_up, ffn_w_down,
           router_w, moe_w_gate, moe_w_up, moe_w_down):
    bsz, seq, d = x.shape
    depth = w_in.shape[0]
    x2 = x.reshape(bsz * seq, d)
    rowvec = lambda v: v.reshape(1, -1).astype(F32)
    for layer in range(depth):
        w_main, w_dt = _permute_w_in(w_in[layer])
        proj, dt_raw = _inproj(x2, rowvec(mix_norm_g[layer]), w_main, w_dt)
        qn, kn, vn = _qkv_prep(proj, rowvec(q_norm_g[layer]), rowvec(k_norm_g[layer]), bsz, seq)
        y_sb = _sb_attention(qn, kn, vn).reshape(bsz * seq, SB_WIDTH)
        y_ssd = _ssd(proj, dt_raw, conv_w[layer], conv_b[layer], dt_bias[layer], a_log[layer], d_skip[layer],
                     ssd_norm_g[layer], bsz, seq)
        x2 = _merge(x2, y_ssd, y_sb, proj, pool_w[layer], pool_scale[layer], w_br_ssd[layer], w_br_sb[layer],
                    w_br_pool[layer], w_out[layer], seq)
        i = layer // 2
        if layer % 2 == 0:
            x2 = _ffn_dense(x2, rowvec(ffn_norm_g[layer]), ffn_w_gate[i], ffn_w_up[i], ffn_w_down[i])
        else:
            x2 = _moe(x2, rowvec(ffn_norm_g[layer]), router_w[i], moe_w_gate[i], moe_w_up[i], moe_w_down[i])
    return x2.reshape(bsz, seq, d)
```

```python
import functools
import math

import jax
import jax.numpy as jnp
from jax import lax
from jax.experimental import pallas as pl
from jax.experimental.pallas import tpu as pltpu

F32 = jnp.float32
BF16 = jnp.bfloat16

D_MODEL = 1024
EPS = 1e-6

SSD_INNER = 1024
SSD_HEAD_DIM = 64
SSD_HEADS = 16
SSD_GROUPS = 4
SSD_HEADS_PER_GROUP = 4
SSD_STATE = 128
SSD_CONV = 4
SSD_CHUNK = 128
SSD_CONV_DIM = SSD_INNER + 2 * SSD_GROUPS * SSD_STATE

SB_HEADS = 4
SB_HEAD_DIM = 128
SB_WIDTH = SB_HEADS * SB_HEAD_DIM
SB_TILE = 256
SB_CHAINS = 2

POOL_WINDOWS = (2, 4, 8, 16)
POOL_GROUPS = 4
POOL_WIDTH = 512
POOL_GROUP_DIM = 128
POOL_HALO = 16

N_BRANCHES = 3
FFN_DENSE = 2816
N_EXPERTS = 8
TOP_K = 2
FFN_EXPERT = 1792
MOE_BLOCK = 512
MOE_TOKEN_TILE = 256

COL_Z = SSD_INNER
COL_XBC = COL_Z + SSD_CONV_DIM
COL_DT = COL_XBC + SSD_HEADS
COL_Q = COL_DT + SB_WIDTH
COL_K = COL_Q + SB_WIDTH
COL_V = COL_K + SB_WIDTH
COL_POOL = COL_V + POOL_WIDTH

P_XBC = 0
P_Z = 2048
P_Q = 3072
P_K = 3584
P_V = 4096
P_U = 4608
P_GATES = 5120
P_WIDTH = 8192

PROJ_DTYPE = BF16
LANES = 128
VMEM_LIMIT = 56 * 1024 * 1024
FFN_CHUNK = 256
INPROJ_CHUNK = 1024
SB_STOP = -110.0


def _params(*sem):
    return pltpu.CompilerParams(dimension_semantics=sem, vmem_limit_bytes=VMEM_LIMIT)


def _const_spec(shape):
    nd = len(shape)
    return pl.BlockSpec(shape, lambda *_: (0,) * nd, pipeline_mode=pl.Buffered(1))


def _split3(x):
    a = x.astype(BF16)
    r = x - a.astype(F32)
    b = r.astype(BF16)
    c = (r - b.astype(F32)).astype(BF16)
    return a, b, c


def _dot(a, b):
    return jnp.dot(a, b, preferred_element_type=F32)


def _dot_nt(a, b):
    return lax.dot_general(a, b, (((1,), (1,)), ((), ())), preferred_element_type=F32)


def _dot_f32_by_01(x, e01):
    a, b, c = _split3(x)
    return _dot(a, e01) + _dot(b, e01) + _dot(c, e01)


def _dot_01_by_f32(t01, x):
    a, b, c = _split3(x)
    return _dot(t01, a) + _dot(t01, b) + _dot(t01, c)


def _rms(x):
    return x * lax.rsqrt(jnp.mean(x * x, axis=-1, keepdims=True) + EPS)


def _silu(x):
    return x * jax.nn.sigmoid(x)


def _softplus(x):
    return jnp.maximum(x, 0.0) + jnp.log1p(jnp.exp(-jnp.abs(x)))


def _inproj_body(x_ref, g_ref, w_ref, wdt_ref, o_ref, dt_ref):
    xb = (_rms(x_ref[...]) * g_ref[...]).astype(BF16)
    dt_ref[...] = _dot(xb, wdt_ref[...])
    for j in range(P_WIDTH // INPROJ_CHUNK):
        cs = slice(j * INPROJ_CHUNK, (j + 1) * INPROJ_CHUNK)
        o_ref[:, cs] = _dot(xb, w_ref[:, cs]).astype(o_ref.dtype)


def _inproj(x2, g, w_main, w_dt):
    t = x2.shape[0]
    tm = min(512, t)
    return pl.pallas_call(
        _inproj_body,
        grid=(t // tm,),
        in_specs=[
            pl.BlockSpec((tm, D_MODEL), lambda i: (i, 0)),
            _const_spec((1, D_MODEL)),
            _const_spec((D_MODEL, P_WIDTH)),
            _const_spec((D_MODEL, LANES)),
        ],
        out_specs=[
            pl.BlockSpec((tm, P_WIDTH), lambda i: (i, 0)),
            pl.BlockSpec((tm, LANES), lambda i: (i, 0)),
        ],
        out_shape=[
            jax.ShapeDtypeStruct((t, P_WIDTH), PROJ_DTYPE),
            jax.ShapeDtypeStruct((t, LANES), F32),
        ],
        compiler_params=_params("parallel"),
        name="inproj",
    )(x2, g, w_main, w_dt)


def _qkv_body(q_ref, k_ref, v_ref, qg_ref, kg_ref, qo_ref, ko_ref, vo_ref):
    scale = 1.0 / math.sqrt(SB_HEAD_DIM)
    for h in range(SB_HEADS):
        sl = slice(h * SB_HEAD_DIM, (h + 1) * SB_HEAD_DIM)
        qo_ref[0, h] = ((_rms(q_ref[:, sl].astype(F32)) * qg_ref[...]) * scale).astype(BF16)
        ko_ref[0, h] = (_rms(k_ref[:, sl].astype(F32)) * kg_ref[...]).astype(BF16)
        vo_ref[0, h] = v_ref[:, sl].astype(BF16)


def _qkv_prep(proj, qg, kg, bsz, seq):
    ts = min(512, seq)
    ns = seq // ts
    col = lambda c: pl.BlockSpec((ts, SB_WIDTH), lambda b, s: (b * ns + s, c))
    out_spec = pl.BlockSpec((1, SB_HEADS, ts, SB_HEAD_DIM), lambda b, s: (b, 0, s, 0))
    out_shape = jax.ShapeDtypeStruct((bsz, SB_HEADS, seq, SB_HEAD_DIM), BF16)
    return pl.pallas_call(
        _qkv_body,
        grid=(bsz, ns),
        in_specs=[col(P_Q // SB_WIDTH), col(P_K // SB_WIDTH), col(P_V // SB_WIDTH),
                  _const_spec((1, SB_HEAD_DIM)), _const_spec((1, SB_HEAD_DIM))],
        out_specs=[out_spec, out_spec, out_spec],
        out_shape=[out_shape, out_shape, out_shape],
        compiler_params=_params("parallel", "parallel"),
        name="qkv_prep",
    )(proj, proj, proj, qg, kg)


def _sb_body(q_ref, k_ref, v_ref, uo_ref, o_ref, carry_ref, acc_ref):
    t = SB_TILE
    first = pl.program_id(2) * SB_CHAINS
    uo = uo_ref[...]
    row = lax.broadcasted_iota(jnp.int32, (t, t), 0)
    col = lax.broadcasted_iota(jnp.int32, (t, t), 1)
    mask = col < row

    def visit(c, kb, diag):
        start = pl.multiple_of(jnp.maximum(kb, 0) * t, t)
        q = q_ref[0, 0, c * t:(c + 1) * t, :]
        k = k_ref[0, 0, pl.ds(start, t), :]
        v = v_ref[0, 0, pl.ds(start, t), :]
        z = _dot_nt(q, k)
        neg_log_rest = jnp.maximum(z, 0.0) + jnp.log(1.0 + jnp.exp(-jnp.abs(z)))
        log_beta = z - neg_log_rest
        if diag:
            neg_log_rest = jnp.where(mask, neg_log_rest, 0.0)
        hi = neg_log_rest.astype(BF16)
        lo = (neg_log_rest - hi.astype(F32)).astype(BF16)
        sr = _dot(hi, uo) + _dot(lo, uo)
        if diag:
            w = jnp.where(mask, jnp.exp(log_beta - sr[:, :t]), 0.0)
            carry = -sr[:, t:]
            acc_ref[c] = _dot(w.astype(BF16), v)
        else:
            prev = carry_ref[c]
            w = jnp.exp(log_beta - sr[:, :t] + jnp.concatenate([prev] * (t // LANES), axis=1))
            w = jnp.where(kb >= 0, w, 0.0)
            carry = prev - sr[:, t:]
            acc_ref[c] += _dot(w.astype(BF16), v)
        carry_ref[c] = carry
        return jnp.max(carry)

    def unfinished(j, maxima):
        need = [jnp.logical_and(first + c - j >= 0, maxima[c] > SB_STOP) for c in range(SB_CHAINS)]
        return functools.reduce(jnp.logical_or, need).astype(jnp.int32)

    maxima = [visit(c, first + c, True) for c in range(SB_CHAINS)]

    def body(state):
        j, _ = state
        maxima = [visit(c, first + c - j, False) for c in range(SB_CHAINS)]
        return j + 1, unfinished(j + 1, maxima)

    lax.while_loop(lambda state: state[1] > 0, body, (jnp.int32(1), unfinished(1, maxima)))
    for c in range(SB_CHAINS):
        o_ref[0, c * t:(c + 1) * t, :] = acc_ref[c].astype(o_ref.dtype)


def _sb_attention(qn, kn, vn):
    bsz, _, seq, _ = qn.shape
    t = SB_TILE
    tq = t * SB_CHAINS
    assert seq % tq == 0
    r = jnp.arange(t)
    upper = (r[:, None] > r[None, :])
    uo = jnp.concatenate([upper, jnp.ones((t, LANES), bool)], axis=1).astype(BF16)
    kv_spec = pl.BlockSpec((1, 1, seq, SB_HEAD_DIM), lambda b, h, i: (b, h, 0, 0))
    return pl.pallas_call(
        _sb_body,
        grid=(bsz, SB_HEADS, seq // tq),
        in_specs=[pl.BlockSpec((1, 1, tq, SB_HEAD_DIM), lambda b, h, i: (b, h, i, 0)),
                  kv_spec, kv_spec, _const_spec((t, t + LANES))],
        out_specs=pl.BlockSpec((1, tq, SB_HEAD_DIM), lambda b, h, i: (b, i, h)),
        out_shape=jax.ShapeDtypeStruct((bsz, seq, SB_WIDTH), BF16),
        scratch_shapes=[pltpu.VMEM((SB_CHAINS, t, LANES), F32), pltpu.VMEM((SB_CHAINS, t, SB_HEAD_DIM), F32)],
        compiler_params=_params("parallel", "parallel", "arbitrary"),
        name="sb_attention",
    )(qn, kn, vn, uo)


def _ssd_body(xbc_ref, z_ref, dt_ref, cw_ref, cb_ref, dtb_ref, alog_ref, dskip_ref, ng_ref, e_ref, tri_ref,
              o_ref, buf_ref, st_ref, y_ref):
    L = SSD_CHUNK
    G, R, P, N = SSD_GROUPS, SSD_HEADS_PER_GROUP, SSD_HEAD_DIM, SSD_STATE
    tail = 8

    @pl.when(pl.program_id(1) == 0)
    def _():
        buf_ref[0:tail, :] = jnp.zeros((tail, SSD_CONV_DIM), F32)
        st_ref[...] = jnp.zeros_like(st_ref)

    buf_ref[tail:tail + L, :] = xbc_ref[...].astype(F32)

    def conv_silu(lo, hi):
        acc = cb_ref[:, lo:hi]
        for k in range(SSD_CONV):
            off = tail - (SSD_CONV - 1) + k
            acc = acc + cw_ref[k:k + 1, lo:hi] * buf_ref[off:off + L, lo:hi]
        return _silu(acc)

    xs = conv_silu(0, SSD_INNER)
    b_in = conv_silu(SSD_INNER, SSD_INNER + G * N)
    c_in = conv_silu(SSD_INNER + G * N, SSD_CONV_DIM)
    buf_ref[0:tail, :] = buf_ref[L:L + tail, :]

    e01 = e_ref[...]
    dt = _softplus(dt_ref[...] + dtb_ref[...])
    da = dt * (-jnp.exp(alog_ref[...]))
    a_cs = _dot_01_by_f32(tri_ref[...], da)
    a_cs_t = a_cs.T
    a_full = _dot_f32_by_01(a_cs, e01)
    dt_full = _dot_f32_by_01(dt, e01)
    a_last = a_full[L - 1:L, :]
    ea_full = jnp.exp(a_full)
    x_dt = xs * dt_full
    xw = (x_dt * jnp.exp(a_last - a_full)).astype(BF16)
    ea_last = jnp.exp(a_last)

    row = lax.broadcasted_iota(jnp.int32, (L, L), 0)
    col = lax.broadcasted_iota(jnp.int32, (L, L), 1)
    causal = col <= row
    lane = lax.broadcasted_iota(jnp.int32, (L, LANES), 1)
    first_head = lane < P

    for g in range(G):
        gs = slice(g * N, (g + 1) * N)
        cg = c_in[:, gs].astype(BF16)
        bg_f32 = b_in[:, gs]
        cb = _dot_nt(cg, bg_f32.astype(BF16))
        for pair in range(R // 2):
            ms = []
            for r in range(2):
                h = g * R + pair * 2 + r
                seg = a_cs[:, h:h + 1] - a_cs_t[h:h + 1, :]
                decay = jnp.exp(jnp.where(causal, seg, -jnp.inf))
                ms.append((cb * decay).astype(BF16))
            c0 = (g * R + pair * 2) * P
            xp = x_dt[:, c0:c0 + LANES]
            rhs = jnp.concatenate([jnp.where(first_head, xp, 0.0).astype(BF16),
                                   jnp.where(first_head, 0.0, xp).astype(BF16)], axis=0)
            y_ref[:, c0:c0 + LANES] = _dot(jnp.concatenate(ms, axis=1), rhs)
        cs = slice(g * R * P, (g + 1) * R * P)
        state = st_ref[g]
        y_ref[:, cs] += _dot(cg, state.astype(BF16)) * ea_full[:, cs]
        st_ref[g] = state * ea_last[:, cs] + _dot(bg_f32.T.astype(BF16), xw[:, cs])

    y = y_ref[...] + xs * dskip_ref[...]
    y = y * _silu(z_ref[...].astype(F32))
    gw = SSD_INNER // G
    for g in range(G):
        cs = slice(g * gw, (g + 1) * gw)
        o_ref[:, cs] = (_rms(y[:, cs]) * ng_ref[:, cs]).astype(o_ref.dtype)


def _ssd(proj, dt_raw, conv_w, conv_b, dt_bias, a_log, d_skip, norm_g, bsz, seq):
    L = SSD_CHUNK
    nc = seq // L
    t = bsz * seq
    pad = LANES - SSD_HEADS
    heads = jnp.arange(LANES)
    cols = jnp.arange(SSD_INNER) // SSD_HEAD_DIM
    e01 = (heads[:, None] == cols[None, :]).astype(BF16)
    r = jnp.arange(L)
    tri = (r[None, :] <= r[:, None]).astype(BF16)
    rowvec = lambda v: v.reshape(1, -1).astype(F32)
    return pl.pallas_call(
        _ssd_body,
        grid=(bsz, nc),
        in_specs=[
            pl.BlockSpec((L, SSD_CONV_DIM), lambda b, c: (b * nc + c, P_XBC // SSD_CONV_DIM)),
            pl.BlockSpec((L, SSD_INNER), lambda b, c: (b * nc + c, P_Z // SSD_INNER)),
            pl.BlockSpec((L, LANES), lambda b, c: (b * nc + c, 0)),
            _const_spec((SSD_CONV, SSD_CONV_DIM)),
            _const_spec((1, SSD_CONV_DIM)),
            _const_spec((1, LANES)),
            _const_spec((1, LANES)),
            _const_spec((1, SSD_INNER)),
            _const_spec((1, SSD_INNER)),
            _const_spec((LANES, SSD_INNER)),
            _const_spec((L, L)),
        ],
        out_specs=pl.BlockSpec((L, SSD_INNER), lambda b, c: (b * nc + c, 0)),
        out_shape=jax.ShapeDtypeStruct((t, SSD_INNER), BF16),
        scratch_shapes=[
            pltpu.VMEM((L + 8, SSD_CONV_DIM), F32),
            pltpu.VMEM((SSD_GROUPS, SSD_STATE, SSD_HEADS_PER_GROUP * SSD_HEAD_DIM), F32),
            pltpu.VMEM((L, SSD_INNER), F32),
        ],
        compiler_params=_params("parallel", "arbitrary"),
        name="ssd",
    )(proj, proj, dt_raw, conv_w.astype(F32), rowvec(conv_b),
      jnp.pad(rowvec(dt_bias), ((0, 0), (0, pad))), jnp.pad(rowvec(a_log), ((0, 0), (0, pad))),
      rowvec(jnp.repeat(d_skip, SSD_HEAD_DIM)), rowvec(norm_g), e01, tri)


def _merge_body(x_ref, yssd_ref, ysb_ref, u_ref, halo_ref, g0_ref, g1_ref, g2_ref,
                wssd_ref, wsb_ref, wpool_ref, wout_ref, pw_ref, ps_ref, o_ref, ext_ref, *, seq):
    tm = x_ref.shape[0]
    H = POOL_HALO
    start = (pl.program_id(0) * tm) % seq
    halo = halo_ref[...].astype(F32)
    ext_ref[0:H, :] = jnp.where(start == 0, jnp.zeros_like(halo), halo)
    ext_ref[H:H + tm, :] = u_ref[...].astype(F32)
    pos = start + lax.broadcasted_iota(jnp.int32, (tm, 1), 0)

    merged = jax.nn.sigmoid(g0_ref[...].astype(F32)) * _dot(yssd_ref[...], wssd_ref[...])
    merged += jax.nn.sigmoid(g1_ref[...].astype(F32)) * _dot(ysb_ref[...], wsb_ref[...])

    ypool = jnp.zeros((tm, D_MODEL), F32)
    for gi, win in enumerate(POOL_WINDOWS):
        cs = slice(gi * POOL_GROUP_DIM, (gi + 1) * POOL_GROUP_DIM)
        cur = ext_ref[H:H + tm, cs]
        wsum = cur
        for k in range(1, win):
            wsum = wsum + ext_ref[H - k:H - k + tm, cs]
        count = jnp.minimum(pos + 1, win).astype(F32)
        pooled = wsum / count - cur
        mixed = _dot(pooled.astype(BF16), pw_ref[gi]) * ps_ref[:, cs]
        ypool += _dot(mixed.astype(BF16), wpool_ref[cs, :])
    merged += jax.nn.sigmoid(g2_ref[...].astype(F32)) * ypool
    o_ref[...] = x_ref[...] + _dot(merged.astype(BF16), wout_ref[...])


def _merge(x2, y_ssd, y_sb, proj, pool_w, pool_scale, w_br_ssd, w_br_sb, w_br_pool, w_out, seq):
    t = x2.shape[0]
    tm = min(512, seq)
    hb = tm // POOL_HALO
    gate = lambda k: pl.BlockSpec((tm, D_MODEL), lambda i: (i, P_GATES // D_MODEL + k))
    return pl.pallas_call(
        functools.partial(_merge_body, seq=seq),
        grid=(t // tm,),
        in_specs=[
            pl.BlockSpec((tm, D_MODEL), lambda i: (i, 0)),
            pl.BlockSpec((tm, SSD_INNER), lambda i: (i, 0)),
            pl.BlockSpec((tm, SB_WIDTH), lambda i: (i, 0)),
            pl.BlockSpec((tm, POOL_WIDTH), lambda i: (i, P_U // POOL_WIDTH)),
            pl.BlockSpec((POOL_HALO, POOL_WIDTH), lambda i: (jnp.maximum(i * hb - 1, 0), P_U // POOL_WIDTH)),
            gate(0), gate(1), gate(2),
            _const_spec((SSD_INNER, D_MODEL)),
            _const_spec((SB_WIDTH, D_MODEL)),
            _const_spec((POOL_WIDTH, D_MODEL)),
            _const_spec((D_MODEL, D_MODEL)),
            _const_spec((POOL_GROUPS, POOL_GROUP_DIM, POOL_GROUP_DIM)),
            _const_spec((1, POOL_WIDTH)),
        ],
        out_specs=pl.BlockSpec((tm, D_MODEL), lambda i: (i, 0)),
        out_shape=jax.ShapeDtypeStruct((t, D_MODEL), F32),
        scratch_shapes=[pltpu.VMEM((tm + POOL_HALO, POOL_WIDTH), F32)],
        compiler_params=_params("parallel"),
        name="merge",
    )(x2, y_ssd, y_sb, proj, proj, proj, proj, proj,
      w_br_ssd.astype(BF16), w_br_sb.astype(BF16), w_br_pool.astype(BF16), w_out.astype(BF16),
      pool_w.astype(BF16), pool_scale.reshape(1, -1).astype(F32))


def _swiglu_into(acc_ref, xb, wg_ref, wu_ref, wd_ref, width):
    for c in range(width // FFN_CHUNK):
        cs = slice(c * FFN_CHUNK, (c + 1) * FFN_CHUNK)
        hidden = _silu(_dot(xb, wg_ref[:, cs])) * _dot(xb, wu_ref[:, cs])
        acc_ref[...] += _dot(hidden.astype(BF16), wd_ref[cs, :])


def _ffn_body(x_ref, g_ref, wg_ref, wu_ref, wd_ref, o_ref):
    x = x_ref[...]
    o_ref[...] = x
    _swiglu_into(o_ref, (_rms(x) * g_ref[...]).astype(BF16), wg_ref, wu_ref, wd_ref, FFN_DENSE)


def _ffn_dense(x2, g, w_gate, w_up, w_down):
    t = x2.shape[0]
    tm = min(512, t)
    return pl.pallas_call(
        _ffn_body,
        grid=(t // tm,),
        in_specs=[
            pl.BlockSpec((tm, D_MODEL), lambda i: (i, 0)),
            _const_spec((1, D_MODEL)),
            _const_spec((D_MODEL, FFN_DENSE)),
            _const_spec((D_MODEL, FFN_DENSE)),
            _const_spec((FFN_DENSE, D_MODEL)),
        ],
        out_specs=pl.BlockSpec((tm, D_MODEL), lambda i: (i, 0)),
        out_shape=jax.ShapeDtypeStruct((t, D_MODEL), F32),
        compiler_params=_params("parallel"),
        name="ffn_dense",
    )(x2, g, w_gate.astype(BF16), w_up.astype(BF16), w_down.astype(BF16))


def _router_body(x_ref, g_ref, rw_ref, h_ref, idx_ref, wt_ref):
    h = _rms(x_ref[...]) * g_ref[...]
    h_ref[...] = h
    rw = rw_ref[...]
    h_hi = h.astype(BF16)
    h_lo = (h - h_hi.astype(F32)).astype(BF16)
    w_hi = rw.astype(BF16)
    w_lo = (rw - w_hi.astype(F32)).astype(BF16)
    logits = _dot(h_hi, w_hi) + (_dot(h_hi, w_lo) + _dot(h_lo, w_hi)) + _dot(h_lo, w_lo)
    lane = lax.broadcasted_iota(jnp.int32, logits.shape, 1)
    lane_f = lane.astype(F32)
    logits = jnp.where(lane < N_EXPERTS, logits, -jnp.inf)
    m1 = jnp.max(logits, axis=-1, keepdims=True)
    i1 = jnp.min(jnp.where(logits == m1, lane_f, float(LANES)), axis=-1, keepdims=True)
    rest = jnp.where(lane_f == i1, -jnp.inf, logits)
    m2 = jnp.max(rest, axis=-1, keepdims=True)
    i2 = jnp.min(jnp.where(rest == m2, lane_f, float(LANES)), axis=-1, keepdims=True)
    e = jnp.exp(m2 - m1)
    w1 = 1.0 / (1.0 + e)
    idx_ref[...] = jnp.where(lane == 0, i1, jnp.where(lane == 1, i2, 0.0)).astype(jnp.int32)
    wt_ref[...] = jnp.where(lane == 0, w1, jnp.where(lane == 1, e * w1, 0.0))


def _router(x2, g, router_w):
    t = x2.shape[0]
    tm = min(512, t)
    rw = jnp.pad(router_w.astype(F32), ((0, 0), (0, LANES - N_EXPERTS)))
    row = pl.BlockSpec((tm, D_MODEL), lambda i: (i, 0))
    small = pl.BlockSpec((tm, LANES), lambda i: (i, 0))
    return pl.pallas_call(
        _router_body,
        grid=(t // tm,),
        in_specs=[row, _const_spec((1, D_MODEL)), _const_spec((D_MODEL, LANES))],
        out_specs=[row, small, small],
        out_shape=[jax.ShapeDtypeStruct((t, D_MODEL), F32),
                   jax.ShapeDtypeStruct((t, LANES), jnp.int32),
                   jax.ShapeDtypeStruct((t, LANES), F32)],
        compiler_params=_params("parallel"),
        name="router",
    )(x2, g, rw)


def _dispatch_body(slot_ref, h_ref, xb_zero_hbm, xb_hbm, sem):
    del xb_zero_hbm
    tm = h_ref.shape[0]
    for k in range(TOP_K):
        for r in range(tm):
            pltpu.make_async_copy(h_ref.at[pl.ds(r, 1)], xb_hbm.at[pl.ds(slot_ref[0, 0, k * tm + r], 1)], sem).start()
    for k in range(TOP_K):
        pltpu.make_async_copy(h_ref, xb_hbm.at[pl.ds(0, tm)], sem).wait()


def _dispatch(h, slot_tiles, cap):
    t = h.shape[0]
    nt = slot_tiles.shape[0]
    tm = t // nt
    return pl.pallas_call(
        _dispatch_body,
        grid=(nt,),
        in_specs=[
            pl.BlockSpec((1, 1, TOP_K * tm), lambda i: (i, 0, 0), memory_space=pltpu.SMEM),
            pl.BlockSpec((tm, D_MODEL), lambda i: (i, 0)),
            pl.BlockSpec(memory_space=pl.ANY),
        ],
        out_specs=pl.BlockSpec(memory_space=pl.ANY),
        out_shape=jax.ShapeDtypeStruct((cap, D_MODEL), F32),
        scratch_shapes=[pltpu.SemaphoreType.DMA],
        input_output_aliases={2: 0},
        compiler_params=_params("arbitrary"),
        name="dispatch",
    )(slot_tiles, h, jnp.zeros((cap, D_MODEL), F32))


def _experts_body(bexp_ref, nused_ref, xb_ref, wg_ref, wu_ref, wd_ref, o_ref):
    del bexp_ref
    o_ref[...] = jnp.zeros_like(o_ref)

    @pl.when(pl.program_id(0) < nused_ref[0])
    def _():
        _swiglu_into(o_ref, xb_ref[...].astype(BF16), wg_ref, wu_ref, wd_ref, FFN_EXPERT)


def _experts(xb, block_exp, n_used, e_gate, e_up, e_down):
    n_blocks = block_exp.shape[0]
    wspec = lambda shape: pl.BlockSpec((None,) + shape, lambda i, be, nu: (be[i], 0, 0))
    rows = pl.BlockSpec((MOE_BLOCK, D_MODEL), lambda i, be, nu: (i, 0))
    grid_spec = pltpu.PrefetchScalarGridSpec(
        num_scalar_prefetch=2,
        grid=(n_blocks,),
        in_specs=[rows, wspec((D_MODEL, FFN_EXPERT)), wspec((D_MODEL, FFN_EXPERT)), wspec((FFN_EXPERT, D_MODEL))],
        out_specs=rows,
    )
    return pl.pallas_call(
        _experts_body,
        grid_spec=grid_spec,
        out_shape=jax.ShapeDtypeStruct((n_blocks * MOE_BLOCK, D_MODEL), F32),
        compiler_params=_params("arbitrary"),
        name="experts",
    )(block_exp, n_used, xb, e_gate.astype(BF16), e_up.astype(BF16), e_down.astype(BF16))


def _combine_body(slot_ref, next_ref, x_ref, wt_ref, yb_hbm, o_ref, buf_ref, sem):
    tm = x_ref.shape[0]
    n = TOP_K * tm
    i = pl.program_id(0)
    cur = i % 2

    def fetch(idx_ref, b):
        def issue(r, carry):
            pltpu.make_async_copy(yb_hbm.at[pl.ds(idx_ref[0, 0, r], 1)], buf_ref.at[b, pl.ds(r, 1)], sem.at[b]).start()
            return carry
        lax.fori_loop(0, n, issue, 0, unroll=8)

    @pl.when(i == 0)
    def _():
        fetch(slot_ref, cur)

    @pl.when(i + 1 < pl.num_programs(0))
    def _():
        fetch(next_ref, 1 - cur)

    pltpu.make_async_copy(yb_hbm.at[pl.ds(0, n)], buf_ref.at[cur], sem.at[cur]).wait()
    wt = wt_ref[...]
    o_ref[...] = x_ref[...] + (wt[:, 0:1] * buf_ref[cur, 0:tm, :] + wt[:, 1:2] * buf_ref[cur, tm:n, :])


def _combine(x2, wt, slot_tiles, yb):
    t = x2.shape[0]
    nt = slot_tiles.shape[0]
    tm = t // nt
    slots = lambda shift: pl.BlockSpec((1, 1, TOP_K * tm), lambda i: (jnp.minimum(i + shift, nt - 1), 0, 0),
                                       memory_space=pltpu.SMEM)
    return pl.pallas_call(
        _combine_body,
        grid=(nt,),
        in_specs=[
            slots(0), slots(1),
            pl.BlockSpec((tm, D_MODEL), lambda i: (i, 0)),
            pl.BlockSpec((tm, LANES), lambda i: (i, 0)),
            pl.BlockSpec(memory_space=pl.ANY),
        ],
        out_specs=pl.BlockSpec((tm, D_MODEL), lambda i: (i, 0)),
        out_shape=jax.ShapeDtypeStruct((t, D_MODEL), F32),
        scratch_shapes=[pltpu.VMEM((2, TOP_K * tm, D_MODEL), F32), pltpu.SemaphoreType.DMA((2,))],
        compiler_params=_params("arbitrary"),
        name="combine",
    )(slot_tiles, slot_tiles, x2, wt, yb)


def _moe(x2, g, router_w, e_gate, e_up, e_down):
    t = x2.shape[0]
    h, idx, wt = _router(x2, g, router_w)
    n_assign = t * TOP_K
    cap = -(-n_assign // MOE_BLOCK) * MOE_BLOCK + N_EXPERTS * MOE_BLOCK
    n_blocks = cap // MOE_BLOCK
    expert_flat = idx[:, :TOP_K].reshape(-1)
    onehot = (expert_flat[:, None] == jnp.arange(N_EXPERTS, dtype=jnp.int32)[None, :]).astype(jnp.int32)
    running = jnp.cumsum(onehot, axis=0)
    counts = running[-1]
    rank = jnp.sum(running * onehot, axis=1) - 1
    padded = (counts + MOE_BLOCK - 1) // MOE_BLOCK * MOE_BLOCK
    pends = jnp.cumsum(padded)
    pstarts = pends - padded
    dest = (pstarts[expert_flat] + rank).astype(jnp.int32)
    block_exp = jnp.minimum(jnp.searchsorted(pends, jnp.arange(n_blocks) * MOE_BLOCK, side="right"),
                            N_EXPERTS - 1).astype(jnp.int32)
    n_used = (pends[-1:] // MOE_BLOCK).astype(jnp.int32)
    tm = min(MOE_TOKEN_TILE, t)
    slot_tiles = dest.reshape(t // tm, tm, TOP_K).transpose(0, 2, 1).reshape(t // tm, 1, TOP_K * tm)
    xb = _dispatch(h, slot_tiles, cap)
    yb = _experts(xb, block_exp, n_used, e_gate, e_up, e_down)
    return _combine(x2, wt, slot_tiles, yb)


def _permute_w_in(w):
    main = jnp.concatenate([
        w[:, COL_Z:COL_XBC], w[:, :COL_Z], w[:, COL_DT:COL_Q], w[:, COL_Q:COL_K], w[:, COL_K:COL_V],
        w[:, COL_V:COL_POOL], w[:, COL_POOL:]], axis=1).astype(BF16)
    dt = jnp.pad(w[:, COL_XBC:COL_DT], ((0, 0), (0, LANES - SSD_HEADS))).astype(BF16)
    return main, dt


def kernel(x, mix_norm_g, w_in, conv_w, conv_b, dt_bias, a_log, d_skip, ssd_norm_g, q_norm_g, k_norm_g, pool_w,
           pool_scale, w_br_ssd, w_br_sb, w_br_pool, w_out, ffn_norm_g, ffn_w_gate, ffn_w_up, ffn_w_down,
           router_w, moe_w_gate, moe_w_up, moe_w_down):
    bsz, seq, d = x.shape
    depth = w_in.shape[0]
    x2 = x.reshape(bsz * seq, d)
    rowvec = lambda v: v.reshape(1, -1).astype(F32)
    for layer in range(depth):
        w_main, w_dt = _permute_w_in(w_in[layer])
        proj, dt_raw = _inproj(x2, rowvec(mix_norm_g[layer]), w_main, w_dt)
        qn, kn, vn = _qkv_prep(proj, rowvec(q_norm_g[layer]), rowvec(k_norm_g[layer]), bsz, seq)
        y_sb = _sb_attention(qn, kn, vn).reshape(bsz * seq, SB_WIDTH)
        y_ssd = _ssd(proj, dt_raw, conv_w[layer], conv_b[layer], dt_bias[layer], a_log[layer], d_skip[layer],
                     ssd_norm_g[layer], bsz, seq)
        x2 = _merge(x2, y_ssd, y_sb, proj, pool_w[layer], pool_scale[layer], w_br_ssd[layer], w_br_sb[layer],
                    w_br_pool[layer], w_out[layer], seq)
        i = layer // 2
        if layer % 2 == 0:
            x2 = _ffn_dense(x2, rowvec(ffn_norm_g[layer]), ffn_w_gate[i], ffn_w_up[i], ffn_w_down[i])
        else:
            x2 = _moe(x2, rowvec(ffn_norm_g[layer]), router_w[i], moe_w_gate[i], moe_w_up[i], moe_w_down[i])
    return x2.reshape(bsz, seq, d)
```

```python
import functools
import math

import jax
import jax.numpy as jnp
from jax import lax
from jax.experimental import pallas as pl
from jax.experimental.pallas import tpu as pltpu

F32 = jnp.float32
BF16 = jnp.bfloat16

D_MODEL = 1024
EPS = 1e-6

SSD_INNER = 1024
SSD_HEAD_DIM = 64
SSD_HEADS = 16
SSD_GROUPS = 4
SSD_HEADS_PER_GROUP = 4
SSD_STATE = 128
SSD_CONV = 4
SSD_CHUNK = 128
SSD_CONV_DIM = SSD_INNER + 2 * SSD_GROUPS * SSD_STATE
SSD_CONV_TAIL = 16
SSD_CONV_PIECE = 512

SB_HEADS = 4
SB_HEAD_DIM = 128
SB_WIDTH = SB_HEADS * SB_HEAD_DIM
SB_TILE = 256
SB_CHAINS = 2

POOL_WINDOWS = (2, 4, 8, 16)
POOL_GROUPS = 4
POOL_WIDTH = 512
POOL_GROUP_DIM = 128
POOL_HALO = 16

N_BRANCHES = 3
FFN_DENSE = 2816
N_EXPERTS = 8
TOP_K = 2
FFN_EXPERT = 1792
MOE_BLOCK = 512
MOE_TOKEN_TILE = 256

COL_Z = SSD_INNER
COL_XBC = COL_Z + SSD_CONV_DIM
COL_DT = COL_XBC + SSD_HEADS
COL_Q = COL_DT + SB_WIDTH
COL_K = COL_Q + SB_WIDTH
COL_V = COL_K + SB_WIDTH
COL_POOL = COL_V + POOL_WIDTH

P_XBC = 0
P_Z = 2048
P_Q = 3072
P_K = 3584
P_V = 4096
P_U = 4608
P_GATES = 5120
P_WIDTH = 8192

PROJ_DTYPE = BF16
LANES = 128
VMEM_LIMIT = 56 * 1024 * 1024
FFN_CHUNK = 256
INPROJ_CHUNK = 1024
SB_STOP = -110.0


def _params(*sem):
    return pltpu.CompilerParams(dimension_semantics=sem, vmem_limit_bytes=VMEM_LIMIT)


def _const_spec(shape):
    nd = len(shape)
    return pl.BlockSpec(shape, lambda *_: (0,) * nd, pipeline_mode=pl.Buffered(1))


def _split3(x):
    a = x.astype(BF16)
    r = x - a.astype(F32)
    b = r.astype(BF16)
    c = (r - b.astype(F32)).astype(BF16)
    return a, b, c


def _dot(a, b):
    return jnp.dot(a, b, preferred_element_type=F32)


def _dot_nt(a, b):
    return lax.dot_general(a, b, (((1,), (1,)), ((), ())), preferred_element_type=F32)


def _dot_f32_by_01(x, e01):
    a, b, c = _split3(x)
    return _dot(a, e01) + _dot(b, e01) + _dot(c, e01)


def _dot_01_by_f32(t01, x):
    a, b, c = _split3(x)
    return _dot(t01, a) + _dot(t01, b) + _dot(t01, c)


def _rms(x):
    return x * lax.rsqrt(jnp.mean(x * x, axis=-1, keepdims=True) + EPS)


def _silu(x):
    return x * jax.nn.sigmoid(x)


def _softplus(x):
    return jnp.maximum(x, 0.0) + jnp.log1p(jnp.exp(-jnp.abs(x)))


def _inproj_body(x_ref, g_ref, w_ref, wdt_ref, o_ref, dt_ref):
    xb = (_rms(x_ref[...]) * g_ref[...]).astype(BF16)
    dt_ref[...] = _dot(xb, wdt_ref[...])
    for j in range(P_WIDTH // INPROJ_CHUNK):
        cs = slice(j * INPROJ_CHUNK, (j + 1) * INPROJ_CHUNK)
        o_ref[:, cs] = _dot(xb, w_ref[:, cs]).astype(o_ref.dtype)


def _inproj(x2, g, w_main, w_dt):
    t = x2.shape[0]
    tm = min(512, t)
    return pl.pallas_call(
        _inproj_body,
        grid=(t // tm,),
        in_specs=[
            pl.BlockSpec((tm, D_MODEL), lambda i: (i, 0)),
            _const_spec((1, D_MODEL)),
            _const_spec((D_MODEL, P_WIDTH)),
            _const_spec((D_MODEL, LANES)),
        ],
        out_specs=[
            pl.BlockSpec((tm, P_WIDTH), lambda i: (i, 0)),
            pl.BlockSpec((tm, LANES), lambda i: (i, 0)),
        ],
        out_shape=[
            jax.ShapeDtypeStruct((t, P_WIDTH), PROJ_DTYPE),
            jax.ShapeDtypeStruct((t, LANES), F32),
        ],
        compiler_params=_params("parallel"),
        name="inproj",
    )(x2, g, w_main, w_dt)


def _qkv_body(q_ref, k_ref, v_ref, qg_ref, kg_ref, qo_ref, ko_ref, vo_ref):
    scale = 1.0 / math.sqrt(SB_HEAD_DIM)
    for h in range(SB_HEADS):
        sl = slice(h * SB_HEAD_DIM, (h + 1) * SB_HEAD_DIM)
        qo_ref[0, h] = ((_rms(q_ref[:, sl].astype(F32)) * qg_ref[...]) * scale).astype(BF16)
        ko_ref[0, h] = (_rms(k_ref[:, sl].astype(F32)) * kg_ref[...]).astype(BF16)
        vo_ref[0, h] = v_ref[:, sl].astype(BF16)


def _qkv_prep(proj, qg, kg, bsz, seq):
    ts = min(512, seq)
    ns = seq // ts
    col = lambda c: pl.BlockSpec((ts, SB_WIDTH), lambda b, s: (b * ns + s, c))
    out_spec = pl.BlockSpec((1, SB_HEADS, ts, SB_HEAD_DIM), lambda b, s: (b, 0, s, 0))
    out_shape = jax.ShapeDtypeStruct((bsz, SB_HEADS, seq, SB_HEAD_DIM), BF16)
    return pl.pallas_call(
        _qkv_body,
        grid=(bsz, ns),
        in_specs=[col(P_Q // SB_WIDTH), col(P_K // SB_WIDTH), col(P_V // SB_WIDTH),
                  _const_spec((1, SB_HEAD_DIM)), _const_spec((1, SB_HEAD_DIM))],
        out_specs=[out_spec, out_spec, out_spec],
        out_shape=[out_shape, out_shape, out_shape],
        compiler_params=_params("parallel", "parallel"),
        name="qkv_prep",
    )(proj, proj, proj, qg, kg)


def _sb_body(q_ref, k_ref, v_ref, uo_ref, o_ref, carry_ref, acc_ref):
    t = SB_TILE
    first = pl.program_id(2) * SB_CHAINS
    uo = uo_ref[...]
    row = lax.broadcasted_iota(jnp.int32, (t, t), 0)
    col = lax.broadcasted_iota(jnp.int32, (t, t), 1)
    mask = col < row

    def visit(c, kb, diag):
        start = pl.multiple_of(jnp.maximum(kb, 0) * t, t)
        q = q_ref[0, 0, c * t:(c + 1) * t, :]
        k = k_ref[0, 0, pl.ds(start, t), :]
        v = v_ref[0, 0, pl.ds(start, t), :]
        z = _dot_nt(q, k)
        neg_log_rest = jnp.maximum(z, 0.0) + jnp.log(1.0 + jnp.exp(-jnp.abs(z)))
        log_beta = z - neg_log_rest
        if diag:
            neg_log_rest = jnp.where(mask, neg_log_rest, 0.0)
        sr = _dot(neg_log_rest.astype(BF16), uo)
        if diag:
            w = jnp.where(mask, jnp.exp(log_beta - sr[:, :t]), 0.0)
            carry = -sr[:, t:]
            acc_ref[c] = _dot(w.astype(BF16), v)
        else:
            prev = carry_ref[c]
            w = jnp.exp(log_beta - sr[:, :t] + jnp.concatenate([prev] * (t // LANES), axis=1))
            w = jnp.where(kb >= 0, w, 0.0)
            carry = prev - sr[:, t:]
            acc_ref[c] += _dot(w.astype(BF16), v)
        carry_ref[c] = carry
        return jnp.max(carry)

    def unfinished(j, maxima):
        need = [jnp.logical_and(first + c - j >= 0, maxima[c] > SB_STOP) for c in range(SB_CHAINS)]
        return functools.reduce(jnp.logical_or, need).astype(jnp.int32)

    maxima = [visit(c, first + c, True) for c in range(SB_CHAINS)]

    def body(state):
        j, _ = state
        maxima = [visit(c, first + c - j, False) for c in range(SB_CHAINS)]
        return j + 1, unfinished(j + 1, maxima)

    lax.while_loop(lambda state: state[1] > 0, body, (jnp.int32(1), unfinished(1, maxima)))
    for c in range(SB_CHAINS):
        o_ref[0, c * t:(c + 1) * t, :] = acc_ref[c].astype(o_ref.dtype)


def _sb_attention(qn, kn, vn):
    bsz, _, seq, _ = qn.shape
    t = SB_TILE
    tq = t * SB_CHAINS
    assert seq % tq == 0
    r = jnp.arange(t)
    upper = (r[:, None] > r[None, :])
    uo = jnp.concatenate([upper, jnp.ones((t, LANES), bool)], axis=1).astype(BF16)
    kv_spec = pl.BlockSpec((1, 1, seq, SB_HEAD_DIM), lambda b, h, i: (b, h, 0, 0))
    return pl.pallas_call(
        _sb_body,
        grid=(bsz, SB_HEADS, seq // tq),
        in_specs=[pl.BlockSpec((1, 1, tq, SB_HEAD_DIM), lambda b, h, i: (b, h, i, 0)),
                  kv_spec, kv_spec, _const_spec((t, t + LANES))],
        out_specs=pl.BlockSpec((1, tq, SB_HEAD_DIM), lambda b, h, i: (b, i, h)),
        out_shape=jax.ShapeDtypeStruct((bsz, seq, SB_WIDTH), BF16),
        scratch_shapes=[pltpu.VMEM((SB_CHAINS, t, LANES), F32), pltpu.VMEM((SB_CHAINS, t, SB_HEAD_DIM), F32)],
        compiler_params=_params("parallel", "parallel", "arbitrary"),
        name="sb_attention",
    )(qn, kn, vn, uo)


def _ssd_body(xbc_ref, z_ref, dt_ref, cw_ref, cb_ref, dtb_ref, alog_ref, dskip_ref, ng_ref, e_ref, tri_ref,
              shift_ref, o_ref, buf_ref, st_ref, y_ref):
    L = SSD_CHUNK
    G, R, P, N = SSD_GROUPS, SSD_HEADS_PER_GROUP, SSD_HEAD_DIM, SSD_STATE
    tail = SSD_CONV_TAIL

    @pl.when(pl.program_id(1) == 0)
    def _():
        buf_ref[0:tail, :] = jnp.zeros((tail, SSD_CONV_DIM), buf_ref.dtype)
        st_ref[...] = jnp.zeros_like(st_ref)

    buf_ref[tail:tail + L, :] = xbc_ref[...]

    def conv_silu_piece(lo, hi):
        delayed = _dot(shift_ref[...], buf_ref[:, lo:hi])
        acc = cb_ref[:, lo:hi] + cw_ref[SSD_CONV - 1:SSD_CONV, lo:hi] * xbc_ref[:, lo:hi].astype(F32)
        for k in range(SSD_CONV - 1):
            acc = acc + cw_ref[k:k + 1, lo:hi] * delayed[k * L:(k + 1) * L, :]
        return _silu(acc)

    def conv_silu(lo, hi):
        step = SSD_CONV_PIECE
        return jnp.concatenate([conv_silu_piece(c, c + step) for c in range(lo, hi, step)], axis=1)

    xs = conv_silu(0, SSD_INNER)
    b_in = conv_silu(SSD_INNER, SSD_INNER + G * N)
    c_in = conv_silu(SSD_INNER + G * N, SSD_CONV_DIM)
    buf_ref[0:tail, :] = buf_ref[L:L + tail, :]

    e01 = e_ref[...]
    dt = _softplus(dt_ref[...] + dtb_ref[...])
    da = dt * (-jnp.exp(alog_ref[...]))
    a_cs = _dot_01_by_f32(tri_ref[...], da)
    a_cs_t = a_cs.T
    a_full = _dot_f32_by_01(a_cs, e01)
    dt_full = _dot_f32_by_01(dt, e01)
    a_last = a_full[L - 1:L, :]
    ea_full = jnp.exp(a_full)
    x_dt = xs * dt_full
    xw = (x_dt * jnp.exp(a_last - a_full)).astype(BF16)
    ea_last = jnp.exp(a_last)

    row = lax.broadcasted_iota(jnp.int32, (L, L), 0)
    col = lax.broadcasted_iota(jnp.int32, (L, L), 1)
    causal = col <= row
    lane = lax.broadcasted_iota(jnp.int32, (L, LANES), 1)
    first_head = lane < P

    for g in range(G):
        gs = slice(g * N, (g + 1) * N)
        cg = c_in[:, gs].astype(BF16)
        bg_f32 = b_in[:, gs]
        cb = _dot_nt(cg, bg_f32.astype(BF16))
        for pair in range(R // 2):
            ms = []
            for r in range(2):
                h = g * R + pair * 2 + r
                seg = a_cs[:, h:h + 1] - a_cs_t[h:h + 1, :]
                decay = jnp.exp(jnp.where(causal, seg, -jnp.inf))
                ms.append((cb * decay).astype(BF16))
            c0 = (g * R + pair * 2) * P
            xp = x_dt[:, c0:c0 + LANES]
            rhs = jnp.concatenate([jnp.where(first_head, xp, 0.0).astype(BF16),
                                   jnp.where(first_head, 0.0, xp).astype(BF16)], axis=0)
            y_ref[:, c0:c0 + LANES] = _dot(jnp.concatenate(ms, axis=1), rhs)
        cs = slice(g * R * P, (g + 1) * R * P)
        state = st_ref[g]
        y_ref[:, cs] += _dot(cg, state.astype(BF16)) * ea_full[:, cs]
        st_ref[g] = state * ea_last[:, cs] + _dot(bg_f32.T.astype(BF16), xw[:, cs])

    y = y_ref[...] + xs * dskip_ref[...]
    y = y * _silu(z_ref[...].astype(F32))
    gw = SSD_INNER // G
    for g in range(G):
        cs = slice(g * gw, (g + 1) * gw)
        o_ref[:, cs] = (_rms(y[:, cs]) * ng_ref[:, cs]).astype(o_ref.dtype)


def _ssd(proj, dt_raw, conv_w, conv_b, dt_bias, a_log, d_skip, norm_g, bsz, seq):
    L = SSD_CHUNK
    nc = seq // L
    t = bsz * seq
    pad = LANES - SSD_HEADS
    heads = jnp.arange(LANES)
    cols = jnp.arange(SSD_INNER) // SSD_HEAD_DIM
    e01 = (heads[:, None] == cols[None, :]).astype(BF16)
    r = jnp.arange(L)
    tri = (r[None, :] <= r[:, None]).astype(BF16)
    assert proj.dtype == BF16, "the 0/1 shift product is exact only on bf16 data"
    delay = (SSD_CONV - 1) - jnp.arange((SSD_CONV - 1) * L) // L
    src = SSD_CONV_TAIL + jnp.arange((SSD_CONV - 1) * L) % L - delay
    shift = (src[:, None] == jnp.arange(SSD_CONV_TAIL + L)[None, :]).astype(BF16)
    rowvec = lambda v: v.reshape(1, -1).astype(F32)
    return pl.pallas_call(
        _ssd_body,
        grid=(bsz, nc),
        in_specs=[
            pl.BlockSpec((L, SSD_CONV_DIM), lambda b, c: (b * nc + c, P_XBC // SSD_CONV_DIM)),
            pl.BlockSpec((L, SSD_INNER), lambda b, c: (b * nc + c, P_Z // SSD_INNER)),
            pl.BlockSpec((L, LANES), lambda b, c: (b * nc + c, 0)),
            _const_spec((SSD_CONV, SSD_CONV_DIM)),
            _const_spec((1, SSD_CONV_DIM)),
            _const_spec((1, LANES)),
            _const_spec((1, LANES)),
            _const_spec((1, SSD_INNER)),
            _const_spec((1, SSD_INNER)),
            _const_spec((LANES, SSD_INNER)),
            _const_spec((L, L)),
            _const_spec(((SSD_CONV - 1) * L, SSD_CONV_TAIL + L)),
        ],
        out_specs=pl.BlockSpec((L, SSD_INNER), lambda b, c: (b * nc + c, 0)),
        out_shape=jax.ShapeDtypeStruct((t, SSD_INNER), BF16),
        scratch_shapes=[
            pltpu.VMEM((SSD_CONV_TAIL + L, SSD_CONV_DIM), BF16),
            pltpu.VMEM((SSD_GROUPS, SSD_STATE, SSD_HEADS_PER_GROUP * SSD_HEAD_DIM), F32),
            pltpu.VMEM((L, SSD_INNER), F32),
        ],
        compiler_params=_params("parallel", "arbitrary"),
        name="ssd",
    )(proj, proj, dt_raw, conv_w.astype(F32), rowvec(conv_b),
      jnp.pad(rowvec(dt_bias), ((0, 0), (0, pad))), jnp.pad(rowvec(a_log), ((0, 0), (0, pad))),
      rowvec(jnp.repeat(d_skip, SSD_HEAD_DIM)), rowvec(norm_g), e01, tri, shift)


def _merge_body(x_ref, yssd_ref, ysb_ref, u_ref, halo_ref, g0_ref, g1_ref, g2_ref,
                wssd_ref, wsb_ref, wpool_ref, wout_ref, pw_ref, ps_ref, o_ref, ext_ref, *, seq):
    tm = x_ref.shape[0]
    H = POOL_HALO
    start = (pl.program_id(0) * tm) % seq
    halo = halo_ref[...].astype(F32)
    ext_ref[0:H, :] = jnp.where(start == 0, jnp.zeros_like(halo), halo)
    ext_ref[H:H + tm, :] = u_ref[...].astype(F32)
    pos = start + lax.broadcasted_iota(jnp.int32, (tm, 1), 0)

    merged = jax.nn.sigmoid(g0_ref[...].astype(F32)) * _dot(yssd_ref[...], wssd_ref[...])
    merged += jax.nn.sigmoid(g1_ref[...].astype(F32)) * _dot(ysb_ref[...], wsb_ref[...])

    ypool = jnp.zeros((tm, D_MODEL), F32)
    for gi, win in enumerate(POOL_WINDOWS):
        cs = slice(gi * POOL_GROUP_DIM, (gi + 1) * POOL_GROUP_DIM)
        cur = ext_ref[H:H + tm, cs]
        wsum = cur
        for k in range(1, win):
            wsum = wsum + ext_ref[H - k:H - k + tm, cs]
        count = jnp.minimum(pos + 1, win).astype(F32)
        pooled = wsum / count - cur
        mixed = _dot(pooled.astype(BF16), pw_ref[gi]) * ps_ref[:, cs]
        ypool += _dot(mixed.astype(BF16), wpool_ref[cs, :])
    merged += jax.nn.sigmoid(g2_ref[...].astype(F32)) * ypool
    o_ref[...] = x_ref[...] + _dot(merged.astype(BF16), wout_ref[...])


def _merge(x2, y_ssd, y_sb, proj, pool_w, pool_scale, w_br_ssd, w_br_sb, w_br_pool, w_out, seq):
    t = x2.shape[0]
    tm = min(512, seq)
    hb = tm // POOL_HALO
    gate = lambda k: pl.BlockSpec((tm, D_MODEL), lambda i: (i, P_GATES // D_MODEL + k))
    return pl.pallas_call(
        functools.partial(_merge_body, seq=seq),
        grid=(t // tm,),
        in_specs=[
            pl.BlockSpec((tm, D_MODEL), lambda i: (i, 0)),
            pl.BlockSpec((tm, SSD_INNER), lambda i: (i, 0)),
            pl.BlockSpec((tm, SB_WIDTH), lambda i: (i, 0)),
            pl.BlockSpec((tm, POOL_WIDTH), lambda i: (i, P_U // POOL_WIDTH)),
            pl.BlockSpec((POOL_HALO, POOL_WIDTH), lambda i: (jnp.maximum(i * hb - 1, 0), P_U // POOL_WIDTH)),
            gate(0), gate(1), gate(2),
            _const_spec((SSD_INNER, D_MODEL)),
            _const_spec((SB_WIDTH, D_MODEL)),
            _const_spec((POOL_WIDTH, D_MODEL)),
            _const_spec((D_MODEL, D_MODEL)),
            _const_spec((POOL_GROUPS, POOL_GROUP_DIM, POOL_GROUP_DIM)),
            _const_spec((1, POOL_WIDTH)),
        ],
        out_specs=pl.BlockSpec((tm, D_MODEL), lambda i: (i, 0)),
        out_shape=jax.ShapeDtypeStruct((t, D_MODEL), F32),
        scratch_shapes=[pltpu.VMEM((tm + POOL_HALO, POOL_WIDTH), F32)],
        compiler_params=_params("parallel"),
        name="merge",
    )(x2, y_ssd, y_sb, proj, proj, proj, proj, proj,
      w_br_ssd.astype(BF16), w_br_sb.astype(BF16), w_br_pool.astype(BF16), w_out.astype(BF16),
      pool_w.astype(BF16), pool_scale.reshape(1, -1).astype(F32))


def _swiglu_into(acc_ref, xb, wg_ref, wu_ref, wd_ref, width):
    for c in range(width // FFN_CHUNK):
        cs = slice(c * FFN_CHUNK, (c + 1) * FFN_CHUNK)
        hidden = _silu(_dot(xb, wg_ref[:, cs])) * _dot(xb, wu_ref[:, cs])
        acc_ref[...] += _dot(hidden.astype(BF16), wd_ref[cs, :])


def _ffn_body(x_ref, g_ref, wg_ref, wu_ref, wd_ref, o_ref):
    x = x_ref[...]
    o_ref[...] = x
    _swiglu_into(o_ref, (_rms(x) * g_ref[...]).astype(BF16), wg_ref, wu_ref, wd_ref, FFN_DENSE)


def _ffn_dense(x2, g, w_gate, w_up, w_down):
    t = x2.shape[0]
    tm = min(512, t)
    return pl.pallas_call(
        _ffn_body,
        grid=(t // tm,),
        in_specs=[
            pl.BlockSpec((tm, D_MODEL), lambda i: (i, 0)),
            _const_spec((1, D_MODEL)),
            _const_spec((D_MODEL, FFN_DENSE)),
            _const_spec((D_MODEL, FFN_DENSE)),
            _const_spec((FFN_DENSE, D_MODEL)),
        ],
        out_specs=pl.BlockSpec((tm, D_MODEL), lambda i: (i, 0)),
        out_shape=jax.ShapeDtypeStruct((t, D_MODEL), F32),
        compiler_params=_params("parallel"),
        name="ffn_dense",
    )(x2, g, w_gate.astype(BF16), w_up.astype(BF16), w_down.astype(BF16))


def _router_body(x_ref, g_ref, rw_ref, h_ref, idx_ref, wt_ref):
    h = _rms(x_ref[...]) * g_ref[...]
    h_ref[...] = h
    rw = rw_ref[...]
    h_hi = h.astype(BF16)
    h_lo = (h - h_hi.astype(F32)).astype(BF16)
    w_hi = rw.astype(BF16)
    w_lo = (rw - w_hi.astype(F32)).astype(BF16)
    logits = _dot(h_hi, w_hi) + (_dot(h_hi, w_lo) + _dot(h_lo, w_hi)) + _dot(h_lo, w_lo)
    lane = lax.broadcasted_iota(jnp.int32, logits.shape, 1)
    lane_f = lane.astype(F32)
    logits = jnp.where(lane < N_EXPERTS, logits, -jnp.inf)
    m1 = jnp.max(logits, axis=-1, keepdims=True)
    i1 = jnp.min(jnp.where(logits == m1, lane_f, float(LANES)), axis=-1, keepdims=True)
    rest = jnp.where(lane_f == i1, -jnp.inf, logits)
    m2 = jnp.max(rest, axis=-1, keepdims=True)
    i2 = jnp.min(jnp.where(rest == m2, lane_f, float(LANES)), axis=-1, keepdims=True)
    e = jnp.exp(m2 - m1)
    w1 = 1.0 / (1.0 + e)
    idx_ref[...] = jnp.where(lane == 0, i1, jnp.where(lane == 1, i2, 0.0)).astype(jnp.int32)
    wt_ref[...] = jnp.where(lane == 0, w1, jnp.where(lane == 1, e * w1, 0.0))


def _router(x2, g, router_w):
    t = x2.shape[0]
    tm = min(512, t)
    rw = jnp.pad(router_w.astype(F32), ((0, 0), (0, LANES - N_EXPERTS)))
    row = pl.BlockSpec((tm, D_MODEL), lambda i: (i, 0))
    small = pl.BlockSpec((tm, LANES), lambda i: (i, 0))
    return pl.pallas_call(
        _router_body,
        grid=(t // tm,),
        in_specs=[row, _const_spec((1, D_MODEL)), _const_spec((D_MODEL, LANES))],
        out_specs=[row, small, small],
        out_shape=[jax.ShapeDtypeStruct((t, D_MODEL), F32),
                   jax.ShapeDtypeStruct((t, LANES), jnp.int32),
                   jax.ShapeDtypeStruct((t, LANES), F32)],
        compiler_params=_params("parallel"),
        name="router",
    )(x2, g, rw)


def _dispatch_body(slot_ref, h_ref, xb_zero_hbm, xb_hbm, sem):
    del xb_zero_hbm
    tm = h_ref.shape[0]
    for k in range(TOP_K):
        for r in range(tm):
            pltpu.make_async_copy(h_ref.at[pl.ds(r, 1)], xb_hbm.at[pl.ds(slot_ref[0, 0, k * tm + r], 1)], sem).start()
    for k in range(TOP_K):
        pltpu.make_async_copy(h_ref, xb_hbm.at[pl.ds(0, tm)], sem).wait()


def _dispatch(h, slot_tiles, cap):
    t = h.shape[0]
    nt = slot_tiles.shape[0]
    tm = t // nt
    return pl.pallas_call(
        _dispatch_body,
        grid=(nt,),
        in_specs=[
            pl.BlockSpec((1, 1, TOP_K * tm), lambda i: (i, 0, 0), memory_space=pltpu.SMEM),
            pl.BlockSpec((tm, D_MODEL), lambda i: (i, 0)),
            pl.BlockSpec(memory_space=pl.ANY),
        ],
        out_specs=pl.BlockSpec(memory_space=pl.ANY),
        out_shape=jax.ShapeDtypeStruct((cap, D_MODEL), F32),
        scratch_shapes=[pltpu.SemaphoreType.DMA],
        input_output_aliases={2: 0},
        compiler_params=_params("arbitrary"),
        name="dispatch",
    )(slot_tiles, h, jnp.zeros((cap, D_MODEL), F32))


def _experts_body(bexp_ref, nused_ref, xb_ref, wg_ref, wu_ref, wd_ref, o_ref):
    del bexp_ref
    o_ref[...] = jnp.zeros_like(o_ref)

    @pl.when(pl.program_id(0) < nused_ref[0])
    def _():
        _swiglu_into(o_ref, xb_ref[...].astype(BF16), wg_ref, wu_ref, wd_ref, FFN_EXPERT)


def _experts(xb, block_exp, n_used, e_gate, e_up, e_down):
    n_blocks = block_exp.shape[0]
    wspec = lambda shape: pl.BlockSpec((None,) + shape, lambda i, be, nu: (be[i], 0, 0))
    rows = pl.BlockSpec((MOE_BLOCK, D_MODEL), lambda i, be, nu: (i, 0))
    grid_spec = pltpu.PrefetchScalarGridSpec(
        num_scalar_prefetch=2,
        grid=(n_blocks,),
        in_specs=[rows, wspec((D_MODEL, FFN_EXPERT)), wspec((D_MODEL, FFN_EXPERT)), wspec((FFN_EXPERT, D_MODEL))],
        out_specs=rows,
    )
    return pl.pallas_call(
        _experts_body,
        grid_spec=grid_spec,
        out_shape=jax.ShapeDtypeStruct((n_blocks * MOE_BLOCK, D_MODEL), F32),
        compiler_params=_params("arbitrary"),
        name="experts",
    )(block_exp, n_used, xb, e_gate.astype(BF16), e_up.astype(BF16), e_down.astype(BF16))


def _combine_body(slot_ref, next_ref, x_ref, wt_ref, yb_hbm, o_ref, buf_ref, sem):
    tm = x_ref.shape[0]
    n = TOP_K * tm
    i = pl.program_id(0)

    def fetch(idx_ref, b):
        for r in range(n):
            pltpu.make_async_copy(yb_hbm.at[pl.ds(idx_ref[0, 0, r], 1)], buf_ref.at[b, pl.ds(r, 1)], sem.at[b]).start()

    @pl.when(i == 0)
    def _():
        fetch(slot_ref, 0)

    for cur in range(2):
        @pl.when(i % 2 == cur)
        def _():
            @pl.when(i + 1 < pl.num_programs(0))
            def _():
                fetch(next_ref, 1 - cur)

            pltpu.make_async_copy(yb_hbm.at[pl.ds(0, n)], buf_ref.at[cur], sem.at[cur]).wait()
            wt = wt_ref[...]
            o_ref[...] = x_ref[...] + (wt[:, 0:1] * buf_ref[cur, 0:tm, :] + wt[:, 1:2] * buf_ref[cur, tm:n, :])


def _combine(x2, wt, slot_tiles, yb):
    t = x2.shape[0]
    nt = slot_tiles.shape[0]
    tm = t // nt
    slots = lambda shift: pl.BlockSpec((1, 1, TOP_K * tm), lambda i: (jnp.minimum(i + shift, nt - 1), 0, 0),
                                       memory_space=pltpu.SMEM)
    return pl.pallas_call(
        _combine_body,
        grid=(nt,),
        in_specs=[
            slots(0), slots(1),
            pl.BlockSpec((tm, D_MODEL), lambda i: (i, 0)),
            pl.BlockSpec((tm, LANES), lambda i: (i, 0)),
            pl.BlockSpec(memory_space=pl.ANY),
        ],
        out_specs=pl.BlockSpec((tm, D_MODEL), lambda i: (i, 0)),
        out_shape=jax.ShapeDtypeStruct((t, D_MODEL), F32),
        scratch_shapes=[pltpu.VMEM((2, TOP_K * tm, D_MODEL), F32), pltpu.SemaphoreType.DMA((2,))],
        compiler_params=_params("arbitrary"),
        name="combine",
    )(slot_tiles, slot_tiles, x2, wt, yb)


def _moe(x2, g, router_w, e_gate, e_up, e_down):
    t = x2.shape[0]
    h, idx, wt = _router(x2, g, router_w)
    n_assign = t * TOP_K
    cap = -(-n_assign // MOE_BLOCK) * MOE_BLOCK + N_EXPERTS * MOE_BLOCK
    n_blocks = cap // MOE_BLOCK
    expert_flat = idx[:, :TOP_K].reshape(-1)
    onehot = (expert_flat[:, None] == jnp.arange(N_EXPERTS, dtype=jnp.int32)[None, :]).astype(jnp.int32)
    running = jnp.cumsum(onehot, axis=0)
    counts = running[-1]
    rank = jnp.sum(running * onehot, axis=1) - 1
    padded = (counts + MOE_BLOCK - 1) // MOE_BLOCK * MOE_BLOCK
    pends = jnp.cumsum(padded)
    pstarts = pends - padded
    dest = (pstarts[expert_flat] + rank).astype(jnp.int32)
    block_exp = jnp.minimum(jnp.searchsorted(pends, jnp.arange(n_blocks) * MOE_BLOCK, side="right"),
                            N_EXPERTS - 1).astype(jnp.int32)
    n_used = (pends[-1:] // MOE_BLOCK).astype(jnp.int32)
    tm = min(MOE_TOKEN_TILE, t)
    slot_tiles = dest.reshape(t // tm, tm, TOP_K).transpose(0, 2, 1).reshape(t // tm, 1, TOP_K * tm)
    xb = _dispatch(h, slot_tiles, cap)
    yb = _experts(xb, block_exp, n_used, e_gate, e_up, e_down)
    return _combine(x2, wt, slot_tiles, yb)


def _permute_w_in(w):
    main = jnp.concatenate([
        w[:, COL_Z:COL_XBC], w[:, :COL_Z], w[:, COL_DT:COL_Q], w[:, COL_Q:COL_K], w[:, COL_K:COL_V],
        w[:, COL_V:COL_POOL], w[:, COL_POOL:]], axis=1).astype(BF16)
    dt = jnp.pad(w[:, COL_XBC:COL_DT], ((0, 0), (0, LANES - SSD_HEADS))).astype(BF16)
    return main, dt


def kernel(x, mix_norm_g, w_in, conv_w, conv_b, dt_bias, a_log, d_skip, ssd_norm_g, q_norm_g, k_norm_g, pool_w,
           pool_scale, w_br_ssd, w_br_sb, w_br_pool, w_out, ffn_norm_g, ffn_w_gate, ffn_w_up, ffn_w_down,
           router_w, moe_w_gate, moe_w_up, moe_w_down):
    bsz, seq, d = x.shape
    depth = w_in.shape[0]
    x2 = x.reshape(bsz * seq, d)
    rowvec = lambda v: v.reshape(1, -1).astype(F32)
    for layer in range(depth):
        w_main, w_dt = _permute_w_in(w_in[layer])
        proj, dt_raw = _inproj(x2, rowvec(mix_norm_g[layer]), w_main, w_dt)
        qn, kn, vn = _qkv_prep(proj, rowvec(q_norm_g[layer]), rowvec(k_norm_g[layer]), bsz, seq)
        y_sb = _sb_attention(qn, kn, vn).reshape(bsz * seq, SB_WIDTH)
        y_ssd = _ssd(proj, dt_raw, conv_w[layer], conv_b[layer], dt_bias[layer], a_log[layer], d_skip[layer],
                     ssd_norm_g[layer], bsz, seq)
        x2 = _merge(x2, y_ssd, y_sb, proj, pool_w[layer], pool_scale[layer], w_br_ssd[layer], w_br_sb[layer],
                    w_br_pool[layer], w_out[layer], seq)
        i = layer // 2
        if layer % 2 == 0:
            x2 = _ffn_dense(x2, rowvec(ffn_norm_g[layer]), ffn_w_gate[i], ffn_w_up[i], ffn_w_down[i])
        else:
            x2 = _moe(x2, rowvec(ffn_norm_g[layer]), router_w[i], moe_w_gate[i], moe_w_up[i], moe_w_down[i])
    return x2.reshape(bsz, seq, d)
```

```python
import functools
import math

import jax
import jax.numpy as jnp
from jax import lax
from jax.experimental import pallas as pl
from jax.experimental.pallas import tpu as pltpu

F32 = jnp.float32
BF16 = jnp.bfloat16

D_MODEL = 1024
EPS = 1e-6

SSD_INNER = 1024
SSD_HEAD_DIM = 64
SSD_HEADS = 16
SSD_GROUPS = 4
SSD_HEADS_PER_GROUP = 4
SSD_STATE = 128
SSD_CONV = 4
SSD_CHUNK = 128
SSD_CONV_DIM = SSD_INNER + 2 * SSD_GROUPS * SSD_STATE
SSD_CONV_TAIL = 16
SSD_CONV_PIECE = 512

SB_HEADS = 4
SB_HEAD_DIM = 128
SB_WIDTH = SB_HEADS * SB_HEAD_DIM
SB_TILE = 256
SB_CHAINS = 2

POOL_WINDOWS = (2, 4, 8, 16)
POOL_GROUPS = 4
POOL_WIDTH = 512
POOL_GROUP_DIM = 128
POOL_HALO = 16

N_BRANCHES = 3
FFN_DENSE = 2816
N_EXPERTS = 8
TOP_K = 2
FFN_EXPERT = 1792
MOE_BLOCK = 512
MOE_TOKEN_TILE = 256

COL_Z = SSD_INNER
COL_XBC = COL_Z + SSD_CONV_DIM
COL_DT = COL_XBC + SSD_HEADS
COL_Q = COL_DT + SB_WIDTH
COL_K = COL_Q + SB_WIDTH
COL_V = COL_K + SB_WIDTH
COL_POOL = COL_V + POOL_WIDTH

P_XBC = 0
P_Z = 2048
P_GATES = 3072
P_U = 6144
P_WIDTH = 6656

PROJ_DTYPE = BF16
LANES = 128
VMEM_LIMIT = 56 * 1024 * 1024
FFN_CHUNK = 256
INPROJ_CHUNK = 1024
SB_STOP = -110.0


def _params(*sem):
    return pltpu.CompilerParams(dimension_semantics=sem, vmem_limit_bytes=VMEM_LIMIT)


def _const_spec(shape):
    nd = len(shape)
    return pl.BlockSpec(shape, lambda *_: (0,) * nd, pipeline_mode=pl.Buffered(1))


def _split3(x):
    a = x.astype(BF16)
    r = x - a.astype(F32)
    b = r.astype(BF16)
    c = (r - b.astype(F32)).astype(BF16)
    return a, b, c


def _dot(a, b):
    return jnp.dot(a, b, preferred_element_type=F32)


def _dot_nt(a, b):
    return lax.dot_general(a, b, (((1,), (1,)), ((), ())), preferred_element_type=F32)


def _dot_f32_by_01(x, e01):
    a, b, c = _split3(x)
    return _dot(a, e01) + _dot(b, e01) + _dot(c, e01)


def _dot_01_by_f32(t01, x):
    a, b, c = _split3(x)
    return _dot(t01, a) + _dot(t01, b) + _dot(t01, c)


def _rms(x):
    return x * lax.rsqrt(jnp.mean(x * x, axis=-1, keepdims=True) + EPS)


def _silu(x):
    return x * jax.nn.sigmoid(x)


def _softplus(x):
    return jnp.maximum(x, 0.0) + jnp.log1p(jnp.exp(-jnp.abs(x)))


def _inproj_body(x_ref, g_ref, w_ref, wqkv_ref, wdt_ref, qg_ref, kg_ref, o_ref, dt_ref, qo_ref, ko_ref, vo_ref):
    xb = (_rms(x_ref[...]) * g_ref[...]).astype(BF16)
    dt_ref[...] = _dot(xb, wdt_ref[...])
    for lo in range(0, P_WIDTH, INPROJ_CHUNK):
        cs = slice(lo, min(lo + INPROJ_CHUNK, P_WIDTH))
        o_ref[:, cs] = _dot(xb, w_ref[:, cs]).astype(o_ref.dtype)
    scale = 1.0 / math.sqrt(SB_HEAD_DIM)
    q = _dot(xb, wqkv_ref[:, 0:SB_WIDTH])
    k = _dot(xb, wqkv_ref[:, SB_WIDTH:2 * SB_WIDTH])
    v = _dot(xb, wqkv_ref[:, 2 * SB_WIDTH:3 * SB_WIDTH])
    for h in range(SB_HEADS):
        sl = slice(h * SB_HEAD_DIM, (h + 1) * SB_HEAD_DIM)
        qo_ref[0, h] = ((_rms(q[:, sl]) * qg_ref[...]) * scale).astype(BF16)
        ko_ref[0, h] = (_rms(k[:, sl]) * kg_ref[...]).astype(BF16)
        vo_ref[0, h] = v[:, sl].astype(BF16)


def _inproj(x2, g, w_main, w_qkv, w_dt, qg, kg, bsz, seq):
    t = x2.shape[0]
    tm = min(512, seq)
    ns = seq // tm
    head_spec = pl.BlockSpec((1, SB_HEADS, tm, SB_HEAD_DIM), lambda i: (i // ns, 0, i % ns, 0))
    head_shape = jax.ShapeDtypeStruct((bsz, SB_HEADS, seq, SB_HEAD_DIM), BF16)
    return pl.pallas_call(
        _inproj_body,
        grid=(t // tm,),
        in_specs=[
            pl.BlockSpec((tm, D_MODEL), lambda i: (i, 0)),
            _const_spec((1, D_MODEL)),
            _const_spec((D_MODEL, P_WIDTH)),
            _const_spec((D_MODEL, 3 * SB_WIDTH)),
            _const_spec((D_MODEL, LANES)),
            _const_spec((1, SB_HEAD_DIM)),
            _const_spec((1, SB_HEAD_DIM)),
        ],
        out_specs=[
            pl.BlockSpec((tm, P_WIDTH), lambda i: (i, 0)),
            pl.BlockSpec((tm, LANES), lambda i: (i, 0)),
            head_spec, head_spec, head_spec,
        ],
        out_shape=[
            jax.ShapeDtypeStruct((t, P_WIDTH), PROJ_DTYPE),
            jax.ShapeDtypeStruct((t, LANES), F32),
            head_shape, head_shape, head_shape,
        ],
        compiler_params=_params("parallel"),
        name="inproj",
    )(x2, g, w_main, w_qkv, w_dt, qg, kg)


def _sb_body(q_ref, k_ref, v_ref, uo_ref, o_ref, carry_ref, acc_ref):
    t = SB_TILE
    first = pl.program_id(2) * SB_CHAINS
    uo = uo_ref[...]
    row = lax.broadcasted_iota(jnp.int32, (t, t), 0)
    col = lax.broadcasted_iota(jnp.int32, (t, t), 1)
    mask = col < row

    def visit(c, kb, diag):
        start = pl.multiple_of(jnp.maximum(kb, 0) * t, t)
        q = q_ref[0, 0, c * t:(c + 1) * t, :]
        k = k_ref[0, 0, pl.ds(start, t), :]
        v = v_ref[0, 0, pl.ds(start, t), :]
        z = _dot_nt(q, k)
        neg_log_rest = jnp.maximum(z, 0.0) + jnp.log(1.0 + jnp.exp(-jnp.abs(z)))
        log_beta = z - neg_log_rest
        if diag:
            neg_log_rest = jnp.where(mask, neg_log_rest, 0.0)
        sr = _dot(neg_log_rest.astype(BF16), uo)
        if diag:
            w = jnp.where(mask, jnp.exp(log_beta - sr[:, :t]), 0.0)
            carry = -sr[:, t:]
            acc_ref[c] = _dot(w.astype(BF16), v)
        else:
            prev = carry_ref[c]
            w = jnp.exp(log_beta - sr[:, :t] + jnp.concatenate([prev] * (t // LANES), axis=1))
            w = jnp.where(kb >= 0, w, 0.0)
            carry = prev - sr[:, t:]
            acc_ref[c] += _dot(w.astype(BF16), v)
        carry_ref[c] = carry
        return jnp.max(carry)

    def unfinished(j, maxima):
        need = [jnp.logical_and(first + c - j >= 0, maxima[c] > SB_STOP) for c in range(SB_CHAINS)]
        return functools.reduce(jnp.logical_or, need).astype(jnp.int32)

    maxima = [visit(c, first + c, True) for c in range(SB_CHAINS)]

    def body(state):
        j, _ = state
        maxima = [visit(c, first + c - j, False) for c in range(SB_CHAINS)]
        return j + 1, unfinished(j + 1, maxima)

    lax.while_loop(lambda state: state[1] > 0, body, (jnp.int32(1), unfinished(1, maxima)))
    for c in range(SB_CHAINS):
        o_ref[0, c * t:(c + 1) * t, :] = acc_ref[c].astype(o_ref.dtype)


def _sb_attention(qn, kn, vn):
    bsz, _, seq, _ = qn.shape
    t = SB_TILE
    tq = t * SB_CHAINS
    assert seq % tq == 0
    r = jnp.arange(t)
    upper = (r[:, None] > r[None, :])
    uo = jnp.concatenate([upper, jnp.ones((t, LANES), bool)], axis=1).astype(BF16)
    kv_spec = pl.BlockSpec((1, 1, seq, SB_HEAD_DIM), lambda b, h, i: (b, h, 0, 0))
    return pl.pallas_call(
        _sb_body,
        grid=(bsz, SB_HEADS, seq // tq),
        in_specs=[pl.BlockSpec((1, 1, tq, SB_HEAD_DIM), lambda b, h, i: (b, h, i, 0)),
                  kv_spec, kv_spec, _const_spec((t, t + LANES))],
        out_specs=pl.BlockSpec((1, tq, SB_HEAD_DIM), lambda b, h, i: (b, i, h)),
        out_shape=jax.ShapeDtypeStruct((bsz, seq, SB_WIDTH), BF16),
        scratch_shapes=[pltpu.VMEM((SB_CHAINS, t, LANES), F32), pltpu.VMEM((SB_CHAINS, t, SB_HEAD_DIM), F32)],
        compiler_params=_params("parallel", "parallel", "arbitrary"),
        name="sb_attention",
    )(qn, kn, vn, uo)


def _ssd_body(xbc_ref, z_ref, dt_ref, cw_ref, cb_ref, dtb_ref, alog_ref, dskip_ref, ng_ref, e_ref, tri_ref,
              shift_ref, o_ref, buf_ref, st_ref, y_ref):
    L = SSD_CHUNK
    G, R, P, N = SSD_GROUPS, SSD_HEADS_PER_GROUP, SSD_HEAD_DIM, SSD_STATE
    tail = SSD_CONV_TAIL

    @pl.when(pl.program_id(1) == 0)
    def _():
        buf_ref[0:tail, :] = jnp.zeros((tail, SSD_CONV_DIM), buf_ref.dtype)
        st_ref[...] = jnp.zeros_like(st_ref)

    buf_ref[tail:tail + L, :] = xbc_ref[...]

    def conv_silu_piece(lo, hi):
        delayed = _dot(shift_ref[...], buf_ref[:, lo:hi])
        acc = cb_ref[:, lo:hi] + cw_ref[SSD_CONV - 1:SSD_CONV, lo:hi] * xbc_ref[:, lo:hi].astype(F32)
        for k in range(SSD_CONV - 1):
            acc = acc + cw_ref[k:k + 1, lo:hi] * delayed[k * L:(k + 1) * L, :]
        return _silu(acc)

    def conv_silu(lo, hi):
        step = SSD_CONV_PIECE
        return jnp.concatenate([conv_silu_piece(c, c + step) for c in range(lo, hi, step)], axis=1)

    xs = conv_silu(0, SSD_INNER)
    b_in = conv_silu(SSD_INNER, SSD_INNER + G * N)
    c_in = conv_silu(SSD_INNER + G * N, SSD_CONV_DIM)
    buf_ref[0:tail, :] = buf_ref[L:L + tail, :]

    e01 = e_ref[...]
    dt = _softplus(dt_ref[...] + dtb_ref[...])
    da = dt * (-jnp.exp(alog_ref[...]))
    a_cs = _dot_01_by_f32(tri_ref[...], da)
    a_cs_t = a_cs.T
    a_full = _dot_f32_by_01(a_cs, e01)
    dt_full = _dot_f32_by_01(dt, e01)
    a_last = a_full[L - 1:L, :]
    ea_full = jnp.exp(a_full)
    x_dt = xs * dt_full
    xw = (x_dt * jnp.exp(a_last - a_full)).astype(BF16)
    ea_last = jnp.exp(a_last)

    row = lax.broadcasted_iota(jnp.int32, (L, L), 0)
    col = lax.broadcasted_iota(jnp.int32, (L, L), 1)
    causal = col <= row
    lane = lax.broadcasted_iota(jnp.int32, (L, LANES), 1)
    first_head = lane < P

    for g in range(G):
        gs = slice(g * N, (g + 1) * N)
        cg = c_in[:, gs].astype(BF16)
        bg_f32 = b_in[:, gs]
        cb = _dot_nt(cg, bg_f32.astype(BF16))
        for pair in range(R // 2):
            ms = []
            for r in range(2):
                h = g * R + pair * 2 + r
                seg = a_cs[:, h:h + 1] - a_cs_t[h:h + 1, :]
                decay = jnp.exp(jnp.where(causal, seg, -jnp.inf))
                ms.append((cb * decay).astype(BF16))
            c0 = (g * R + pair * 2) * P
            xp = x_dt[:, c0:c0 + LANES]
            rhs = jnp.concatenate([jnp.where(first_head, xp, 0.0).astype(BF16),
                                   jnp.where(first_head, 0.0, xp).astype(BF16)], axis=0)
            y_ref[:, c0:c0 + LANES] = _dot(jnp.concatenate(ms, axis=1), rhs)
        cs = slice(g * R * P, (g + 1) * R * P)
        state = st_ref[g]
        y_ref[:, cs] += _dot(cg, state.astype(BF16)) * ea_full[:, cs]
        st_ref[g] = state * ea_last[:, cs] + _dot(bg_f32.T.astype(BF16), xw[:, cs])

    y = y_ref[...] + xs * dskip_ref[...]
    y = y * _silu(z_ref[...].astype(F32))
    gw = SSD_INNER // G
    for g in range(G):
        cs = slice(g * gw, (g + 1) * gw)
        o_ref[:, cs] = (_rms(y[:, cs]) * ng_ref[:, cs]).astype(o_ref.dtype)


def _ssd(proj, dt_raw, conv_w, conv_b, dt_bias, a_log, d_skip, norm_g, bsz, seq):
    L = SSD_CHUNK
    nc = seq // L
    t = bsz * seq
    pad = LANES - SSD_HEADS
    heads = jnp.arange(LANES)
    cols = jnp.arange(SSD_INNER) // SSD_HEAD_DIM
    e01 = (heads[:, None] == cols[None, :]).astype(BF16)
    r = jnp.arange(L)
    tri = (r[None, :] <= r[:, None]).astype(BF16)
    assert proj.dtype == BF16, "the 0/1 shift product is exact only on bf16 data"
    delay = (SSD_CONV - 1) - jnp.arange((SSD_CONV - 1) * L) // L
    src = SSD_CONV_TAIL + jnp.arange((SSD_CONV - 1) * L) % L - delay
    shift = (src[:, None] == jnp.arange(SSD_CONV_TAIL + L)[None, :]).astype(BF16)
    rowvec = lambda v: v.reshape(1, -1).astype(F32)
    return pl.pallas_call(
        _ssd_body,
        grid=(bsz, nc),
        in_specs=[
            pl.BlockSpec((L, SSD_CONV_DIM), lambda b, c: (b * nc + c, P_XBC // SSD_CONV_DIM)),
            pl.BlockSpec((L, SSD_INNER), lambda b, c: (b * nc + c, P_Z // SSD_INNER)),
            pl.BlockSpec((L, LANES), lambda b, c: (b * nc + c, 0)),
            _const_spec((SSD_CONV, SSD_CONV_DIM)),
            _const_spec((1, SSD_CONV_DIM)),
            _const_spec((1, LANES)),
            _const_spec((1, LANES)),
            _const_spec((1, SSD_INNER)),
            _const_spec((1, SSD_INNER)),
            _const_spec((LANES, SSD_INNER)),
            _const_spec((L, L)),
            _const_spec(((SSD_CONV - 1) * L, SSD_CONV_TAIL + L)),
        ],
        out_specs=pl.BlockSpec((L, SSD_INNER), lambda b, c: (b * nc + c, 0)),
        out_shape=jax.ShapeDtypeStruct((t, SSD_INNER), BF16),
        scratch_shapes=[
            pltpu.VMEM((SSD_CONV_TAIL + L, SSD_CONV_DIM), BF16),
            pltpu.VMEM((SSD_GROUPS, SSD_STATE, SSD_HEADS_PER_GROUP * SSD_HEAD_DIM), F32),
            pltpu.VMEM((L, SSD_INNER), F32),
        ],
        compiler_params=_params("parallel", "arbitrary"),
        name="ssd",
    )(proj, proj, dt_raw, conv_w.astype(F32), rowvec(conv_b),
      jnp.pad(rowvec(dt_bias), ((0, 0), (0, pad))), jnp.pad(rowvec(a_log), ((0, 0), (0, pad))),
      rowvec(jnp.repeat(d_skip, SSD_HEAD_DIM)), rowvec(norm_g), e01, tri, shift)


def _merge_body(x_ref, yssd_ref, ysb_ref, u_ref, halo_ref, g0_ref, g1_ref, g2_ref,
                wssd_ref, wsb_ref, wpool_ref, wout_ref, pw_ref, ps_ref, o_ref, ext_ref, *, seq):
    tm = x_ref.shape[0]
    H = POOL_HALO
    start = (pl.program_id(0) * tm) % seq
    halo = halo_ref[...].astype(F32)
    ext_ref[0:H, :] = jnp.where(start == 0, jnp.zeros_like(halo), halo)
    ext_ref[H:H + tm, :] = u_ref[...].astype(F32)
    pos = start + lax.broadcasted_iota(jnp.int32, (tm, 1), 0)

    merged = jax.nn.sigmoid(g0_ref[...].astype(F32)) * _dot(yssd_ref[...], wssd_ref[...])
    merged += jax.nn.sigmoid(g1_ref[...].astype(F32)) * _dot(ysb_ref[...], wsb_ref[...])

    ypool = jnp.zeros((tm, D_MODEL), F32)
    for gi, win in enumerate(POOL_WINDOWS):
        cs = slice(gi * POOL_GROUP_DIM, (gi + 1) * POOL_GROUP_DIM)
        cur = ext_ref[H:H + tm, cs]
        wsum = cur
        for k in range(1, win):
            wsum = wsum + ext_ref[H - k:H - k + tm, cs]
        count = jnp.minimum(pos + 1, win).astype(F32)
        pooled = wsum / count - cur
        mixed = _dot(pooled.astype(BF16), pw_ref[gi]) * ps_ref[:, cs]
        ypool += _dot(mixed.astype(BF16), wpool_ref[cs, :])
    merged += jax.nn.sigmoid(g2_ref[...].astype(F32)) * ypool
    o_ref[...] = x_ref[...] + _dot(merged.astype(BF16), wout_ref[...])


def _merge(x2, y_ssd, y_sb, proj, pool_w, pool_scale, w_br_ssd, w_br_sb, w_br_pool, w_out, seq):
    t = x2.shape[0]
    tm = min(512, seq)
    hb = tm // POOL_HALO
    gate = lambda k: pl.BlockSpec((tm, D_MODEL), lambda i: (i, P_GATES // D_MODEL + k))
    return pl.pallas_call(
        functools.partial(_merge_body, seq=seq),
        grid=(t // tm,),
        in_specs=[
            pl.BlockSpec((tm, D_MODEL), lambda i: (i, 0)),
            pl.BlockSpec((tm, SSD_INNER), lambda i: (i, 0)),
            pl.BlockSpec((tm, SB_WIDTH), lambda i: (i, 0)),
            pl.BlockSpec((tm, POOL_WIDTH), lambda i: (i, P_U // POOL_WIDTH)),
            pl.BlockSpec((POOL_HALO, POOL_WIDTH), lambda i: (jnp.maximum(i * hb - 1, 0), P_U // POOL_WIDTH)),
            gate(0), gate(1), gate(2),
            _const_spec((SSD_INNER, D_MODEL)),
            _const_spec((SB_WIDTH, D_MODEL)),
            _const_spec((POOL_WIDTH, D_MODEL)),
            _const_spec((D_MODEL, D_MODEL)),
            _const_spec((POOL_GROUPS, POOL_GROUP_DIM, POOL_GROUP_DIM)),
            _const_spec((1, POOL_WIDTH)),
        ],
        out_specs=pl.BlockSpec((tm, D_MODEL), lambda i: (i, 0)),
        out_shape=jax.ShapeDtypeStruct((t, D_MODEL), F32),
        scratch_shapes=[pltpu.VMEM((tm + POOL_HALO, POOL_WIDTH), F32)],
        compiler_params=_params("parallel"),
        name="merge",
    )(x2, y_ssd, y_sb, proj, proj, proj, proj, proj,
      w_br_ssd.astype(BF16), w_br_sb.astype(BF16), w_br_pool.astype(BF16), w_out.astype(BF16),
      pool_w.astype(BF16), pool_scale.reshape(1, -1).astype(F32))


def _swiglu_into(acc_ref, xb, wg_ref, wu_ref, wd_ref, width):
    for c in range(width // FFN_CHUNK):
        cs = slice(c * FFN_CHUNK, (c + 1) * FFN_CHUNK)
        hidden = _silu(_dot(xb, wg_ref[:, cs])) * _dot(xb, wu_ref[:, cs])
        acc_ref[...] += _dot(hidden.astype(BF16), wd_ref[cs, :])


def _ffn_body(x_ref, g_ref, wg_ref, wu_ref, wd_ref, o_ref):
    x = x_ref[...]
    o_ref[...] = x
    _swiglu_into(o_ref, (_rms(x) * g_ref[...]).astype(BF16), wg_ref, wu_ref, wd_ref, FFN_DENSE)


def _ffn_dense(x2, g, w_gate, w_up, w_down):
    t = x2.shape[0]
    tm = min(512, t)
    return pl.pallas_call(
        _ffn_body,
        grid=(t // tm,),
        in_specs=[
            pl.BlockSpec((tm, D_MODEL), lambda i: (i, 0)),
            _const_spec((1, D_MODEL)),
            _const_spec((D_MODEL, FFN_DENSE)),
            _const_spec((D_MODEL, FFN_DENSE)),
            _const_spec((FFN_DENSE, D_MODEL)),
        ],
        out_specs=pl.BlockSpec((tm, D_MODEL), lambda i: (i, 0)),
        out_shape=jax.ShapeDtypeStruct((t, D_MODEL), F32),
        compiler_params=_params("parallel"),
        name="ffn_dense",
    )(x2, g, w_gate.astype(BF16), w_up.astype(BF16), w_down.astype(BF16))


def _router_body(x_ref, g_ref, rw_ref, h_ref, idx_ref, wt_ref):
    h = _rms(x_ref[...]) * g_ref[...]
    h_ref[...] = h
    rw = rw_ref[...]
    h_hi = h.astype(BF16)
    h_lo = (h - h_hi.astype(F32)).astype(BF16)
    w_hi = rw.astype(BF16)
    w_lo = (rw - w_hi.astype(F32)).astype(BF16)
    logits = _dot(h_hi, w_hi) + (_dot(h_hi, w_lo) + _dot(h_lo, w_hi)) + _dot(h_lo, w_lo)
    lane = lax.broadcasted_iota(jnp.int32, logits.shape, 1)
    lane_f = lane.astype(F32)
    logits = jnp.where(lane < N_EXPERTS, logits, -jnp.inf)
    m1 = jnp.max(logits, axis=-1, keepdims=True)
    i1 = jnp.min(jnp.where(logits == m1, lane_f, float(LANES)), axis=-1, keepdims=True)
    rest = jnp.where(lane_f == i1, -jnp.inf, logits)
    m2 = jnp.max(rest, axis=-1, keepdims=True)
    i2 = jnp.min(jnp.where(rest == m2, lane_f, float(LANES)), axis=-1, keepdims=True)
    e = jnp.exp(m2 - m1)
    w1 = 1.0 / (1.0 + e)
    idx_ref[...] = jnp.where(lane == 0, i1, jnp.where(lane == 1, i2, 0.0)).astype(jnp.int32)
    wt_ref[...] = jnp.where(lane == 0, w1, jnp.where(lane == 1, e * w1, 0.0))


def _router(x2, g, router_w):
    t = x2.shape[0]
    tm = min(512, t)
    rw = jnp.pad(router_w.astype(F32), ((0, 0), (0, LANES - N_EXPERTS)))
    row = pl.BlockSpec((tm, D_MODEL), lambda i: (i, 0))
    small = pl.BlockSpec((tm, LANES), lambda i: (i, 0))
    return pl.pallas_call(
        _router_body,
        grid=(t // tm,),
        in_specs=[row, _const_spec((1, D_MODEL)), _const_spec((D_MODEL, LANES))],
        out_specs=[row, small, small],
        out_shape=[jax.ShapeDtypeStruct((t, D_MODEL), F32),
                   jax.ShapeDtypeStruct((t, LANES), jnp.int32),
                   jax.ShapeDtypeStruct((t, LANES), F32)],
        compiler_params=_params("parallel"),
        name="router",
    )(x2, g, rw)


def _dispatch_body(slot_ref, h_ref, xb_zero_hbm, xb_hbm, sem):
    del xb_zero_hbm
    tm = h_ref.shape[0]
    for k in range(TOP_K):
        for r in range(tm):
            pltpu.make_async_copy(h_ref.at[pl.ds(r, 1)], xb_hbm.at[pl.ds(slot_ref[0, 0, k * tm + r], 1)], sem).start()
    for k in range(TOP_K):
        pltpu.make_async_copy(h_ref, xb_hbm.at[pl.ds(0, tm)], sem).wait()


def _dispatch(h, slot_tiles, cap):
    t = h.shape[0]
    nt = slot_tiles.shape[0]
    tm = t // nt
    return pl.pallas_call(
        _dispatch_body,
        grid=(nt,),
        in_specs=[
            pl.BlockSpec((1, 1, TOP_K * tm), lambda i: (i, 0, 0), memory_space=pltpu.SMEM),
            pl.BlockSpec((tm, D_MODEL), lambda i: (i, 0)),
            pl.BlockSpec(memory_space=pl.ANY),
        ],
        out_specs=pl.BlockSpec(memory_space=pl.ANY),
        out_shape=jax.ShapeDtypeStruct((cap, D_MODEL), F32),
        scratch_shapes=[pltpu.SemaphoreType.DMA],
        input_output_aliases={2: 0},
        compiler_params=_params("arbitrary"),
        name="dispatch",
    )(slot_tiles, h, jnp.zeros((cap, D_MODEL), F32))


def _experts_body(bexp_ref, nused_ref, xb_ref, wg_ref, wu_ref, wd_ref, o_ref):
    del bexp_ref
    o_ref[...] = jnp.zeros_like(o_ref)

    @pl.when(pl.program_id(0) < nused_ref[0])
    def _():
        _swiglu_into(o_ref, xb_ref[...].astype(BF16), wg_ref, wu_ref, wd_ref, FFN_EXPERT)


def _experts(xb, block_exp, n_used, e_gate, e_up, e_down):
    n_blocks = block_exp.shape[0]
    wspec = lambda shape: pl.BlockSpec((None,) + shape, lambda i, be, nu: (be[i], 0, 0))
    rows = pl.BlockSpec((MOE_BLOCK, D_MODEL), lambda i, be, nu: (i, 0))
    grid_spec = pltpu.PrefetchScalarGridSpec(
        num_scalar_prefetch=2,
        grid=(n_blocks,),
        in_specs=[rows, wspec((D_MODEL, FFN_EXPERT)), wspec((D_MODEL, FFN_EXPERT)), wspec((FFN_EXPERT, D_MODEL))],
        out_specs=rows,
    )
    return pl.pallas_call(
        _experts_body,
        grid_spec=grid_spec,
        out_shape=jax.ShapeDtypeStruct((n_blocks * MOE_BLOCK, D_MODEL), F32),
        compiler_params=_params("arbitrary"),
        name="experts",
    )(block_exp, n_used, xb, e_gate.astype(BF16), e_up.astype(BF16), e_down.astype(BF16))


def _combine_body(slot_ref, next_ref, x_ref, wt_ref, yb_hbm, o_ref, buf_ref, sem):
    tm = x_ref.shape[0]
    n = TOP_K * tm
    i = pl.program_id(0)

    def fetch(idx_ref, b):
        for r in range(n):
            pltpu.make_async_copy(yb_hbm.at[pl.ds(idx_ref[0, 0, r], 1)], buf_ref.at[b, pl.ds(r, 1)], sem.at[b]).start()

    @pl.when(i == 0)
    def _():
        fetch(slot_ref, 0)

    for cur in range(2):
        @pl.when(i % 2 == cur)
        def _():
            @pl.when(i + 1 < pl.num_programs(0))
            def _():
                fetch(next_ref, 1 - cur)

            pltpu.make_async_copy(yb_hbm.at[pl.ds(0, n)], buf_ref.at[cur], sem.at[cur]).wait()
            wt = wt_ref[...]
            o_ref[...] = x_ref[...] + (wt[:, 0:1] * buf_ref[cur, 0:tm, :] + wt[:, 1:2] * buf_ref[cur, tm:n, :])


def _combine(x2, wt, slot_tiles, yb):
    t = x2.shape[0]
    nt = slot_tiles.shape[0]
    tm = t // nt
    slots = lambda shift: pl.BlockSpec((1, 1, TOP_K * tm), lambda i: (jnp.minimum(i + shift, nt - 1), 0, 0),
                                       memory_space=pltpu.SMEM)
    return pl.pallas_call(
        _combine_body,
        grid=(nt,),
        in_specs=[
            slots(0), slots(1),
            pl.BlockSpec((tm, D_MODEL), lambda i: (i, 0)),
            pl.BlockSpec((tm, LANES), lambda i: (i, 0)),
            pl.BlockSpec(memory_space=pl.ANY),
        ],
        out_specs=pl.BlockSpec((tm, D_MODEL), lambda i: (i, 0)),
        out_shape=jax.ShapeDtypeStruct((t, D_MODEL), F32),
        scratch_shapes=[pltpu.VMEM((2, TOP_K * tm, D_MODEL), F32), pltpu.SemaphoreType.DMA((2,))],
        compiler_params=_params("arbitrary"),
        name="combine",
    )(slot_tiles, slot_tiles, x2, wt, yb)


def _moe(x2, g, router_w, e_gate, e_up, e_down):
    t = x2.shape[0]
    h, idx, wt = _router(x2, g, router_w)
    n_assign = t * TOP_K
    cap = -(-n_assign // MOE_BLOCK) * MOE_BLOCK + N_EXPERTS * MOE_BLOCK
    n_blocks = cap // MOE_BLOCK
    expert_flat = idx[:, :TOP_K].reshape(-1)
    onehot = (expert_flat[:, None] == jnp.arange(N_EXPERTS, dtype=jnp.int32)[None, :]).astype(jnp.int32)
    running = jnp.cumsum(onehot, axis=0)
    counts = running[-1]
    rank = jnp.sum(running * onehot, axis=1) - 1
    padded = (counts + MOE_BLOCK - 1) // MOE_BLOCK * MOE_BLOCK
    pends = jnp.cumsum(padded)
    pstarts = pends - padded
    dest = (pstarts[expert_flat] + rank).astype(jnp.int32)
    block_exp = jnp.minimum(jnp.searchsorted(pends, jnp.arange(n_blocks) * MOE_BLOCK, side="right"),
                            N_EXPERTS - 1).astype(jnp.int32)
    n_used = (pends[-1:] // MOE_BLOCK).astype(jnp.int32)
    tm = min(MOE_TOKEN_TILE, t)
    slot_tiles = dest.reshape(t // tm, tm, TOP_K).transpose(0, 2, 1).reshape(t // tm, 1, TOP_K * tm)
    xb = _dispatch(h, slot_tiles, cap)
    yb = _experts(xb, block_exp, n_used, e_gate, e_up, e_down)
    return _combine(x2, wt, slot_tiles, yb)


def _permute_w_in(w):
    main = jnp.concatenate([w[:, COL_Z:COL_XBC], w[:, :COL_Z], w[:, COL_POOL:], w[:, COL_V:COL_POOL]],
                           axis=1).astype(BF16)
    qkv = w[:, COL_DT:COL_V].astype(BF16)
    dt = jnp.pad(w[:, COL_XBC:COL_DT], ((0, 0), (0, LANES - SSD_HEADS))).astype(BF16)
    return main, qkv, dt


def kernel(x, mix_norm_g, w_in, conv_w, conv_b, dt_bias, a_log, d_skip, ssd_norm_g, q_norm_g, k_norm_g, pool_w,
           pool_scale, w_br_ssd, w_br_sb, w_br_pool, w_out, ffn_norm_g, ffn_w_gate, ffn_w_up, ffn_w_down,
           router_w, moe_w_gate, moe_w_up, moe_w_down):
    bsz, seq, d = x.shape
    depth = w_in.shape[0]
    x2 = x.reshape(bsz * seq, d)
    rowvec = lambda v: v.reshape(1, -1).astype(F32)
    for layer in range(depth):
        w_main, w_qkv, w_dt = _permute_w_in(w_in[layer])
        proj, dt_raw, qn, kn, vn = _inproj(x2, rowvec(mix_norm_g[layer]), w_main, w_qkv, w_dt,
                                           rowvec(q_norm_g[layer]), rowvec(k_norm_g[layer]), bsz, seq)
        y_sb = _sb_attention(qn, kn, vn).reshape(bsz * seq, SB_WIDTH)
        y_ssd = _ssd(proj, dt_raw, conv_w[layer], conv_b[layer], dt_bias[layer], a_log[layer], d_skip[layer],
                     ssd_norm_g[layer], bsz, seq)
        x2 = _merge(x2, y_ssd, y_sb, proj, pool_w[layer], pool_scale[layer], w_br_ssd[layer], w_br_sb[layer],
                    w_br_pool[layer], w_out[layer], seq)
        i = layer // 2
        if layer % 2 == 0:
            x2 = _ffn_dense(x2, rowvec(ffn_norm_g[layer]), ffn_w_gate[i], ffn_w_up[i], ffn_w_down[i])
        else:
            x2 = _moe(x2, rowvec(ffn_norm_g[layer]), router_w[i], moe_w_gate[i], moe_w_up[i], moe_w_down[i])
    return x2.reshape(bsz, seq, d)
```

```python
import functools
import math

import jax
import jax.numpy as jnp
from jax import lax
from jax.experimental import pallas as pl
from jax.experimental.pallas import tpu as pltpu

F32 = jnp.float32
BF16 = jnp.bfloat16

D_MODEL = 1024
EPS = 1e-6

SSD_INNER = 1024
SSD_HEAD_DIM = 64
SSD_HEADS = 16
SSD_GROUPS = 4
SSD_HEADS_PER_GROUP = 4
SSD_STATE = 128
SSD_CONV = 4
SSD_CHUNK = 128
SSD_CONV_DIM = SSD_INNER + 2 * SSD_GROUPS * SSD_STATE
SSD_CONV_TAIL = 16
SSD_CONV_PIECE = 512

SB_HEADS = 4
SB_HEAD_DIM = 128
SB_WIDTH = SB_HEADS * SB_HEAD_DIM
SB_TILE = 256
SB_CHAINS = 4

POOL_WINDOWS = (2, 4, 8, 16)
POOL_GROUPS = 4
POOL_WIDTH = 512
POOL_GROUP_DIM = 128
POOL_HALO = 16

N_BRANCHES = 3
FFN_DENSE = 2816
N_EXPERTS = 8
TOP_K = 2
FFN_EXPERT = 1792
MOE_BLOCK = 512
MOE_TOKEN_TILE = 256

COL_Z = SSD_INNER
COL_XBC = COL_Z + SSD_CONV_DIM
COL_DT = COL_XBC + SSD_HEADS
COL_Q = COL_DT + SB_WIDTH
COL_K = COL_Q + SB_WIDTH
COL_V = COL_K + SB_WIDTH
COL_POOL = COL_V + POOL_WIDTH

P_XBC = 0
P_Z = 2048
P_GATES = 3072
P_U = 6144
P_WIDTH = 6656

PROJ_DTYPE = BF16
LANES = 128
VMEM_LIMIT = 56 * 1024 * 1024
FFN_CHUNK = 256
INPROJ_CHUNK = 1024
SB_STOP = -110.0


def _params(*sem):
    return pltpu.CompilerParams(dimension_semantics=sem, vmem_limit_bytes=VMEM_LIMIT)


def _const_spec(shape):
    nd = len(shape)
    return pl.BlockSpec(shape, lambda *_: (0,) * nd, pipeline_mode=pl.Buffered(1))


def _split3(x):
    a = x.astype(BF16)
    r = x - a.astype(F32)
    b = r.astype(BF16)
    c = (r - b.astype(F32)).astype(BF16)
    return a, b, c


def _dot(a, b):
    return jnp.dot(a, b, preferred_element_type=F32)


def _dot_nt(a, b):
    return lax.dot_general(a, b, (((1,), (1,)), ((), ())), preferred_element_type=F32)


def _dot_f32_by_01(x, e01):
    a, b, c = _split3(x)
    return _dot(a, e01) + _dot(b, e01) + _dot(c, e01)


def _dot_01_by_f32(t01, x):
    a, b, c = _split3(x)
    return _dot(t01, a) + _dot(t01, b) + _dot(t01, c)


def _rms(x):
    return x * lax.rsqrt(jnp.mean(x * x, axis=-1, keepdims=True) + EPS)


def _silu(x):
    return x * jax.nn.sigmoid(x)


def _softplus(x):
    return jnp.maximum(x, 0.0) + jnp.log1p(jnp.exp(-jnp.abs(x)))


def _inproj_body(x_ref, g_ref, w_ref, wqkv_ref, wdt_ref, qg_ref, kg_ref, o_ref, dt_ref, qo_ref, ko_ref, vo_ref):
    xb = (_rms(x_ref[...]) * g_ref[...]).astype(BF16)
    dt_ref[...] = _dot(xb, wdt_ref[...])
    for lo in range(0, P_WIDTH, INPROJ_CHUNK):
        cs = slice(lo, min(lo + INPROJ_CHUNK, P_WIDTH))
        o_ref[:, cs] = _dot(xb, w_ref[:, cs]).astype(o_ref.dtype)
    scale = 1.0 / math.sqrt(SB_HEAD_DIM)
    q = _dot(xb, wqkv_ref[:, 0:SB_WIDTH])
    k = _dot(xb, wqkv_ref[:, SB_WIDTH:2 * SB_WIDTH])
    v = _dot(xb, wqkv_ref[:, 2 * SB_WIDTH:3 * SB_WIDTH])
    for h in range(SB_HEADS):
        sl = slice(h * SB_HEAD_DIM, (h + 1) * SB_HEAD_DIM)
        qo_ref[0, h] = ((_rms(q[:, sl]) * qg_ref[...]) * scale).astype(BF16)
        ko_ref[0, h] = (_rms(k[:, sl]) * kg_ref[...]).astype(BF16)
        vo_ref[0, h] = v[:, sl].astype(BF16)


def _inproj(x2, g, w_main, w_qkv, w_dt, qg, kg, bsz, seq):
    t = x2.shape[0]
    tm = min(512, seq)
    ns = seq // tm
    head_spec = pl.BlockSpec((1, SB_HEADS, tm, SB_HEAD_DIM), lambda i: (i // ns, 0, i % ns, 0))
    head_shape = jax.ShapeDtypeStruct((bsz, SB_HEADS, seq, SB_HEAD_DIM), BF16)
    return pl.pallas_call(
        _inproj_body,
        grid=(t // tm,),
        in_specs=[
            pl.BlockSpec((tm, D_MODEL), lambda i: (i, 0)),
            _const_spec((1, D_MODEL)),
            _const_spec((D_MODEL, P_WIDTH)),
            _const_spec((D_MODEL, 3 * SB_WIDTH)),
            _const_spec((D_MODEL, LANES)),
            _const_spec((1, SB_HEAD_DIM)),
            _const_spec((1, SB_HEAD_DIM)),
        ],
        out_specs=[
            pl.BlockSpec((tm, P_WIDTH), lambda i: (i, 0)),
            pl.BlockSpec((tm, LANES), lambda i: (i, 0)),
            head_spec, head_spec, head_spec,
        ],
        out_shape=[
            jax.ShapeDtypeStruct((t, P_WIDTH), PROJ_DTYPE),
            jax.ShapeDtypeStruct((t, LANES), F32),
            head_shape, head_shape, head_shape,
        ],
        compiler_params=_params("parallel"),
        name="inproj",
    )(x2, g, w_main, w_qkv, w_dt, qg, kg)


def _sb_body(q_ref, k_ref, v_ref, uo_ref, o_ref, carry_ref, acc_ref):
    t = SB_TILE
    first = pl.program_id(2) * SB_CHAINS
    uo = uo_ref[...]
    row = lax.broadcasted_iota(jnp.int32, (t, t), 0)
    col = lax.broadcasted_iota(jnp.int32, (t, t), 1)
    mask = col < row

    def visit(c, kb, diag):
        start = pl.multiple_of(jnp.maximum(kb, 0) * t, t)
        q = q_ref[0, 0, c * t:(c + 1) * t, :]
        k = k_ref[0, 0, pl.ds(start, t), :]
        v = v_ref[0, 0, pl.ds(start, t), :]
        z = _dot_nt(q, k)
        neg_log_rest = jnp.maximum(z, 0.0) + jnp.log(1.0 + jnp.exp(-jnp.abs(z)))
        log_beta = z - neg_log_rest
        if diag:
            neg_log_rest = jnp.where(mask, neg_log_rest, 0.0)
        sr = _dot(neg_log_rest.astype(BF16), uo)
        if diag:
            w = jnp.where(mask, jnp.exp(log_beta - sr[:, :t]), 0.0)
            carry = -sr[:, t:]
            acc_ref[c] = _dot(w.astype(BF16), v)
        else:
            prev = carry_ref[c]
            w = jnp.exp(log_beta - sr[:, :t] + jnp.concatenate([prev] * (t // LANES), axis=1))
            w = jnp.where(kb >= 0, w, 0.0)
            carry = prev - sr[:, t:]
            acc_ref[c] += _dot(w.astype(BF16), v)
        carry_ref[c] = carry
        return jnp.max(carry)

    def unfinished(j, maxima):
        need = [jnp.logical_and(first + c - j >= 0, maxima[c] > SB_STOP) for c in range(SB_CHAINS)]
        return functools.reduce(jnp.logical_or, need).astype(jnp.int32)

    maxima = [visit(c, first + c, True) for c in range(SB_CHAINS)]

    def body(state):
        j, _ = state
        maxima = [visit(c, first + c - j, False) for c in range(SB_CHAINS)]
        return j + 1, unfinished(j + 1, maxima)

    lax.while_loop(lambda state: state[1] > 0, body, (jnp.int32(1), unfinished(1, maxima)))
    for c in range(SB_CHAINS):
        o_ref[0, c * t:(c + 1) * t, :] = acc_ref[c].astype(o_ref.dtype)


def _sb_attention(qn, kn, vn):
    bsz, _, seq, _ = qn.shape
    t = SB_TILE
    tq = t * SB_CHAINS
    assert seq % tq == 0
    r = jnp.arange(t)
    upper = (r[:, None] > r[None, :])
    uo = jnp.concatenate([upper, jnp.ones((t, LANES), bool)], axis=1).astype(BF16)
    kv_spec = pl.BlockSpec((1, 1, seq, SB_HEAD_DIM), lambda b, h, i: (b, h, 0, 0))
    return pl.pallas_call(
        _sb_body,
        grid=(bsz, SB_HEADS, seq // tq),
        in_specs=[pl.BlockSpec((1, 1, tq, SB_HEAD_DIM), lambda b, h, i: (b, h, i, 0)),
                  kv_spec, kv_spec, _const_spec((t, t + LANES))],
        out_specs=pl.BlockSpec((1, tq, SB_HEAD_DIM), lambda b, h, i: (b, i, h)),
        out_shape=jax.ShapeDtypeStruct((bsz, seq, SB_WIDTH), BF16),
        scratch_shapes=[pltpu.VMEM((SB_CHAINS, t, LANES), F32), pltpu.VMEM((SB_CHAINS, t, SB_HEAD_DIM), F32)],
        compiler_params=_params("parallel", "parallel", "arbitrary"),
        name="sb_attention",
    )(qn, kn, vn, uo)


def _ssd_body(xbcs_ref, zs_ref, dts_ref, cw_ref, cb_ref, dtb_ref, alog_ref, dskip_ref, ng_ref, e_ref, tri_ref,
              shift_ref, os_ref, bufs_ref, sts_ref, ys_ref):
    tail = SSD_CONV_TAIL

    @pl.when(pl.program_id(0) == 0)
    def _():
        bufs_ref[:, 0:tail, :] = jnp.zeros((bufs_ref.shape[0], tail, SSD_CONV_DIM), bufs_ref.dtype)
        sts_ref[...] = jnp.zeros_like(sts_ref)

    for b in range(xbcs_ref.shape[0]):
        _ssd_chunk(xbcs_ref.at[b], zs_ref.at[b], dts_ref.at[b], cw_ref, cb_ref, dtb_ref, alog_ref, dskip_ref, ng_ref,
                   e_ref, tri_ref, shift_ref, os_ref.at[b], bufs_ref.at[b], sts_ref.at[b], ys_ref.at[b])


def _ssd_chunk(xbc_ref, z_ref, dt_ref, cw_ref, cb_ref, dtb_ref, alog_ref, dskip_ref, ng_ref, e_ref, tri_ref,
               shift_ref, o_ref, buf_ref, st_ref, y_ref):
    L = SSD_CHUNK
    G, R, P, N = SSD_GROUPS, SSD_HEADS_PER_GROUP, SSD_HEAD_DIM, SSD_STATE
    tail = SSD_CONV_TAIL
    buf_ref[tail:tail + L, :] = xbc_ref[...]

    def conv_silu_piece(lo, hi):
        delayed = _dot(shift_ref[...], buf_ref[:, lo:hi])
        acc = cb_ref[:, lo:hi] + cw_ref[SSD_CONV - 1:SSD_CONV, lo:hi] * xbc_ref[:, lo:hi].astype(F32)
        for k in range(SSD_CONV - 1):
            acc = acc + cw_ref[k:k + 1, lo:hi] * delayed[k * L:(k + 1) * L, :]
        return _silu(acc)

    def conv_silu(lo, hi):
        step = SSD_CONV_PIECE
        return jnp.concatenate([conv_silu_piece(c, c + step) for c in range(lo, hi, step)], axis=1)

    xs = conv_silu(0, SSD_INNER)
    b_in = conv_silu(SSD_INNER, SSD_INNER + G * N)
    c_in = conv_silu(SSD_INNER + G * N, SSD_CONV_DIM)
    buf_ref[0:tail, :] = buf_ref[L:L + tail, :]

    e01 = e_ref[...]
    dt = _softplus(dt_ref[...] + dtb_ref[...])
    da = dt * (-jnp.exp(alog_ref[...]))
    a_cs = _dot_01_by_f32(tri_ref[...], da)
    a_cs_t = a_cs.T
    a_full = _dot_f32_by_01(a_cs, e01)
    dt_full = _dot_f32_by_01(dt, e01)
    a_last = a_full[L - 1:L, :]
    ea_full = jnp.exp(a_full)
    x_dt = xs * dt_full
    xw = (x_dt * jnp.exp(a_last - a_full)).astype(BF16)
    ea_last = jnp.exp(a_last)

    row = lax.broadcasted_iota(jnp.int32, (L, L), 0)
    col = lax.broadcasted_iota(jnp.int32, (L, L), 1)
    causal = col <= row
    lane = lax.broadcasted_iota(jnp.int32, (L, LANES), 1)
    first_head = lane < P

    for g in range(G):
        gs = slice(g * N, (g + 1) * N)
        cg = c_in[:, gs].astype(BF16)
        bg_f32 = b_in[:, gs]
        cb = _dot_nt(cg, bg_f32.astype(BF16))
        for pair in range(R // 2):
            ms = []
            for r in range(2):
                h = g * R + pair * 2 + r
                seg = a_cs[:, h:h + 1] - a_cs_t[h:h + 1, :]
                decay = jnp.exp(jnp.where(causal, seg, -jnp.inf))
                ms.append((cb * decay).astype(BF16))
            c0 = (g * R + pair * 2) * P
            xp = x_dt[:, c0:c0 + LANES]
            rhs = jnp.concatenate([jnp.where(first_head, xp, 0.0).astype(BF16),
                                   jnp.where(first_head, 0.0, xp).astype(BF16)], axis=0)
            y_ref[:, c0:c0 + LANES] = _dot(jnp.concatenate(ms, axis=1), rhs)
        cs = slice(g * R * P, (g + 1) * R * P)
        state = st_ref[g]
        y_ref[:, cs] += _dot(cg, state.astype(BF16)) * ea_full[:, cs]
        st_ref[g] = state * ea_last[:, cs] + _dot(bg_f32.T.astype(BF16), xw[:, cs])

    y = y_ref[...] + xs * dskip_ref[...]
    y = y * _silu(z_ref[...].astype(F32))
    gw = SSD_INNER // G
    for g in range(G):
        cs = slice(g * gw, (g + 1) * gw)
        o_ref[:, cs] = (_rms(y[:, cs]) * ng_ref[:, cs]).astype(o_ref.dtype)


def _ssd(proj, dt_raw, conv_w, conv_b, dt_bias, a_log, d_skip, norm_g, bsz, seq):
    L = SSD_CHUNK
    nc = seq // L
    t = bsz * seq
    pad = LANES - SSD_HEADS
    heads = jnp.arange(LANES)
    cols = jnp.arange(SSD_INNER) // SSD_HEAD_DIM
    e01 = (heads[:, None] == cols[None, :]).astype(BF16)
    r = jnp.arange(L)
    tri = (r[None, :] <= r[:, None]).astype(BF16)
    assert proj.dtype == BF16, "the 0/1 shift product is exact only on bf16 data"
    delay = (SSD_CONV - 1) - jnp.arange((SSD_CONV - 1) * L) // L
    src = SSD_CONV_TAIL + jnp.arange((SSD_CONV - 1) * L) % L - delay
    shift = (src[:, None] == jnp.arange(SSD_CONV_TAIL + L)[None, :]).astype(BF16)
    rowvec = lambda v: v.reshape(1, -1).astype(F32)
    proj3 = proj.reshape(bsz, seq, proj.shape[-1])
    return pl.pallas_call(
        _ssd_body,
        grid=(nc,),
        in_specs=[
            pl.BlockSpec((bsz, L, SSD_CONV_DIM), lambda c: (0, c, P_XBC // SSD_CONV_DIM)),
            pl.BlockSpec((bsz, L, SSD_INNER), lambda c: (0, c, P_Z // SSD_INNER)),
            pl.BlockSpec((bsz, L, LANES), lambda c: (0, c, 0)),
            _const_spec((SSD_CONV, SSD_CONV_DIM)),
            _const_spec((1, SSD_CONV_DIM)),
            _const_spec((1, LANES)),
            _const_spec((1, LANES)),
            _const_spec((1, SSD_INNER)),
            _const_spec((1, SSD_INNER)),
            _const_spec((LANES, SSD_INNER)),
            _const_spec((L, L)),
            _const_spec(((SSD_CONV - 1) * L, SSD_CONV_TAIL + L)),
        ],
        out_specs=pl.BlockSpec((bsz, L, SSD_INNER), lambda c: (0, c, 0)),
        out_shape=jax.ShapeDtypeStruct((bsz, seq, SSD_INNER), BF16),
        scratch_shapes=[
            pltpu.VMEM((bsz, SSD_CONV_TAIL + L, SSD_CONV_DIM), BF16),
            pltpu.VMEM((bsz, SSD_GROUPS, SSD_STATE, SSD_HEADS_PER_GROUP * SSD_HEAD_DIM), F32),
            pltpu.VMEM((bsz, L, SSD_INNER), F32),
        ],
        compiler_params=_params("arbitrary"),
        name="ssd",
    )(proj3, proj3, dt_raw.reshape(bsz, seq, LANES), conv_w.astype(F32), rowvec(conv_b),
      jnp.pad(rowvec(dt_bias), ((0, 0), (0, pad))), jnp.pad(rowvec(a_log), ((0, 0), (0, pad))),
      rowvec(jnp.repeat(d_skip, SSD_HEAD_DIM)), rowvec(norm_g), e01, tri, shift).reshape(t, SSD_INNER)


def _merge_body(x_ref, yssd_ref, ysb_ref, u_ref, halo_ref, g0_ref, g1_ref, g2_ref,
                wssd_ref, wsb_ref, wpool_ref, wout_ref, pw_ref, ps_ref, o_ref, ext_ref, *, seq):
    tm = x_ref.shape[0]
    H = POOL_HALO
    start = (pl.program_id(0) * tm) % seq
    halo = halo_ref[...].astype(F32)
    ext_ref[0:H, :] = jnp.where(start == 0, jnp.zeros_like(halo), halo)
    ext_ref[H:H + tm, :] = u_ref[...].astype(F32)
    pos = start + lax.broadcasted_iota(jnp.int32, (tm, 1), 0)

    merged = jax.nn.sigmoid(g0_ref[...].astype(F32)) * _dot(yssd_ref[...], wssd_ref[...])
    merged += jax.nn.sigmoid(g1_ref[...].astype(F32)) * _dot(ysb_ref[...], wsb_ref[...])

    ypool = jnp.zeros((tm, D_MODEL), F32)
    for gi, win in enumerate(POOL_WINDOWS):
        cs = slice(gi * POOL_GROUP_DIM, (gi + 1) * POOL_GROUP_DIM)
        cur = ext_ref[H:H + tm, cs]
        wsum = cur
        for k in range(1, win):
            wsum = wsum + ext_ref[H - k:H - k + tm, cs]
        count = jnp.minimum(pos + 1, win).astype(F32)
        pooled = wsum / count - cur
        mixed = _dot(pooled.astype(BF16), pw_ref[gi]) * ps_ref[:, cs]
        ypool += _dot(mixed.astype(BF16), wpool_ref[cs, :])
    merged += jax.nn.sigmoid(g2_ref[...].astype(F32)) * ypool
    o_ref[...] = x_ref[...] + _dot(merged.astype(BF16), wout_ref[...])


def _merge(x2, y_ssd, y_sb, proj, pool_w, pool_scale, w_br_ssd, w_br_sb, w_br_pool, w_out, seq):
    t = x2.shape[0]
    tm = min(512, seq)
    hb = tm // POOL_HALO
    gate = lambda k: pl.BlockSpec((tm, D_MODEL), lambda i: (i, P_GATES // D_MODEL + k))
    return pl.pallas_call(
        functools.partial(_merge_body, seq=seq),
        grid=(t // tm,),
        in_specs=[
            pl.BlockSpec((tm, D_MODEL), lambda i: (i, 0)),
            pl.BlockSpec((tm, SSD_INNER), lambda i: (i, 0)),
            pl.BlockSpec((tm, SB_WIDTH), lambda i: (i, 0)),
            pl.BlockSpec((tm, POOL_WIDTH), lambda i: (i, P_U // POOL_WIDTH)),
            pl.BlockSpec((POOL_HALO, POOL_WIDTH), lambda i: (jnp.maximum(i * hb - 1, 0), P_U // POOL_WIDTH)),
            gate(0), gate(1), gate(2),
            _const_spec((SSD_INNER, D_MODEL)),
            _const_spec((SB_WIDTH, D_MODEL)),
            _const_spec((POOL_WIDTH, D_MODEL)),
            _const_spec((D_MODEL, D_MODEL)),
            _const_spec((POOL_GROUPS, POOL_GROUP_DIM, POOL_GROUP_DIM)),
            _const_spec((1, POOL_WIDTH)),
        ],
        out_specs=pl.BlockSpec((tm, D_MODEL), lambda i: (i, 0)),
        out_shape=jax.ShapeDtypeStruct((t, D_MODEL), F32),
        scratch_shapes=[pltpu.VMEM((tm + POOL_HALO, POOL_WIDTH), F32)],
        compiler_params=_params("parallel"),
        name="merge",
    )(x2, y_ssd, y_sb, proj, proj, proj, proj, proj,
      w_br_ssd.astype(BF16), w_br_sb.astype(BF16), w_br_pool.astype(BF16), w_out.astype(BF16),
      pool_w.astype(BF16), pool_scale.reshape(1, -1).astype(F32))


def _swiglu_into(acc_ref, xb, wg_ref, wu_ref, wd_ref, width):
    for c in range(width // FFN_CHUNK):
        cs = slice(c * FFN_CHUNK, (c + 1) * FFN_CHUNK)
        hidden = _silu(_dot(xb, wg_ref[:, cs])) * _dot(xb, wu_ref[:, cs])
        acc_ref[...] += _dot(hidden.astype(BF16), wd_ref[cs, :])


def _ffn_body(x_ref, g_ref, wg_ref, wu_ref, wd_ref, o_ref):
    x = x_ref[...]
    o_ref[...] = x
    _swiglu_into(o_ref, (_rms(x) * g_ref[...]).astype(BF16), wg_ref, wu_ref, wd_ref, FFN_DENSE)


def _ffn_dense(x2, g, w_gate, w_up, w_down):
    t = x2.shape[0]
    tm = min(512, t)
    return pl.pallas_call(
        _ffn_body,
        grid=(t // tm,),
        in_specs=[
            pl.BlockSpec((tm, D_MODEL), lambda i: (i, 0)),
            _const_spec((1, D_MODEL)),
            _const_spec((D_MODEL, FFN_DENSE)),
            _const_spec((D_MODEL, FFN_DENSE)),
            _const_spec((FFN_DENSE, D_MODEL)),
        ],
        out_specs=pl.BlockSpec((tm, D_MODEL), lambda i: (i, 0)),
        out_shape=jax.ShapeDtypeStruct((t, D_MODEL), F32),
        compiler_params=_params("parallel"),
        name="ffn_dense",
    )(x2, g, w_gate.astype(BF16), w_up.astype(BF16), w_down.astype(BF16))


def _router_body(x_ref, g_ref, rw_ref, h_ref, idx_ref, wt_ref):
    h = _rms(x_ref[...]) * g_ref[...]
    h_ref[...] = h
    rw = rw_ref[...]
    h_hi = h.astype(BF16)
    h_lo = (h - h_hi.astype(F32)).astype(BF16)
    w_hi = rw.astype(BF16)
    w_lo = (rw - w_hi.astype(F32)).astype(BF16)
    logits = _dot(h_hi, w_hi) + (_dot(h_hi, w_lo) + _dot(h_lo, w_hi)) + _dot(h_lo, w_lo)
    lane = lax.broadcasted_iota(jnp.int32, logits.shape, 1)
    lane_f = lane.astype(F32)
    logits = jnp.where(lane < N_EXPERTS, logits, -jnp.inf)
    m1 = jnp.max(logits, axis=-1, keepdims=True)
    i1 = jnp.min(jnp.where(logits == m1, lane_f, float(LANES)), axis=-1, keepdims=True)
    rest = jnp.where(lane_f == i1, -jnp.inf, logits)
    m2 = jnp.max(rest, axis=-1, keepdims=True)
    i2 = jnp.min(jnp.where(rest == m2, lane_f, float(LANES)), axis=-1, keepdims=True)
    e = jnp.exp(m2 - m1)
    w1 = 1.0 / (1.0 + e)
    idx_ref[...] = jnp.where(lane == 0, i1, jnp.where(lane == 1, i2, 0.0)).astype(jnp.int32)
    wt_ref[...] = jnp.where(lane == 0, w1, jnp.where(lane == 1, e * w1, 0.0))


def _router(x2, g, router_w):
    t = x2.shape[0]
    tm = min(512, t)
    rw = jnp.pad(router_w.astype(F32), ((0, 0), (0, LANES - N_EXPERTS)))
    row = pl.BlockSpec((tm, D_MODEL), lambda i: (i, 0))
    small = pl.BlockSpec((tm, LANES), lambda i: (i, 0))
    return pl.pallas_call(
        _router_body,
        grid=(t // tm,),
        in_specs=[row, _const_spec((1, D_MODEL)), _const_spec((D_MODEL, LANES))],
        out_specs=[row, small, small],
        out_shape=[jax.ShapeDtypeStruct((t, D_MODEL), F32),
                   jax.ShapeDtypeStruct((t, LANES), jnp.int32),
                   jax.ShapeDtypeStruct((t, LANES), F32)],
        compiler_params=_params("parallel"),
        name="router",
    )(x2, g, rw)


def _dispatch_body(slot_ref, h_ref, xb_zero_hbm, xb_hbm, sem):
    del xb_zero_hbm
    tm = h_ref.shape[0]
    for k in range(TOP_K):
        for r in range(tm):
            pltpu.make_async_copy(h_ref.at[pl.ds(r, 1)], xb_hbm.at[pl.ds(slot_ref[0, 0, k * tm + r], 1)], sem).start()
    for k in range(TOP_K):
        pltpu.make_async_copy(h_ref, xb_hbm.at[pl.ds(0, tm)], sem).wait()


def _dispatch(h, slot_tiles, cap):
    t = h.shape[0]
    nt = slot_tiles.shape[0]
    tm = t // nt
    return pl.pallas_call(
        _dispatch_body,
        grid=(nt,),
        in_specs=[
            pl.BlockSpec((1, 1, TOP_K * tm), lambda i: (i, 0, 0), memory_space=pltpu.SMEM),
            pl.BlockSpec((tm, D_MODEL), lambda i: (i, 0)),
            pl.BlockSpec(memory_space=pl.ANY),
        ],
        out_specs=pl.BlockSpec(memory_space=pl.ANY),
        out_shape=jax.ShapeDtypeStruct((cap, D_MODEL), F32),
        scratch_shapes=[pltpu.SemaphoreType.DMA],
        input_output_aliases={2: 0},
        compiler_params=_params("arbitrary"),
        name="dispatch",
    )(slot_tiles, h, jnp.zeros((cap, D_MODEL), F32))


def _experts_body(bexp_ref, nused_ref, xb_ref, wg_ref, wu_ref, wd_ref, o_ref):
    del bexp_ref
    o_ref[...] = jnp.zeros_like(o_ref)

    @pl.when(pl.program_id(0) < nused_ref[0])
    def _():
        _swiglu_into(o_ref, xb_ref[...].astype(BF16), wg_ref, wu_ref, wd_ref, FFN_EXPERT)


def _experts(xb, block_exp, n_used, e_gate, e_up, e_down):
    n_blocks = block_exp.shape[0]
    wspec = lambda shape: pl.BlockSpec((None,) + shape, lambda i, be, nu: (be[i], 0, 0))
    rows = pl.BlockSpec((MOE_BLOCK, D_MODEL), lambda i, be, nu: (i, 0))
    grid_spec = pltpu.PrefetchScalarGridSpec(
        num_scalar_prefetch=2,
        grid=(n_blocks,),
        in_specs=[rows, wspec((D_MODEL, FFN_EXPERT)), wspec((D_MODEL, FFN_EXPERT)), wspec((FFN_EXPERT, D_MODEL))],
        out_specs=rows,
    )
    return pl.pallas_call(
        _experts_body,
        grid_spec=grid_spec,
        out_shape=jax.ShapeDtypeStruct((n_blocks * MOE_BLOCK, D_MODEL), F32),
        compiler_params=_params("arbitrary"),
        name="experts",
    )(block_exp, n_used, xb, e_gate.astype(BF16), e_up.astype(BF16), e_down.astype(BF16))


def _combine_body(slot_ref, next_ref, x_ref, wt_ref, yb_hbm, o_ref, buf_ref, sem):
    tm = x_ref.shape[0]
    n = TOP_K * tm
    i = pl.program_id(0)

    def fetch(idx_ref, b):
        for r in range(n):
            pltpu.make_async_copy(yb_hbm.at[pl.ds(idx_ref[0, 0, r], 1)], buf_ref.at[b, pl.ds(r, 1)], sem.at[b]).start()

    @pl.when(i == 0)
    def _():
        fetch(slot_ref, 0)

    for cur in range(2):
        @pl.when(i % 2 == cur)
        def _():
            @pl.when(i + 1 < pl.num_programs(0))
            def _():
                fetch(next_ref, 1 - cur)

            pltpu.make_async_copy(yb_hbm.at[pl.ds(0, n)], buf_ref.at[cur], sem.at[cur]).wait()
            wt = wt_ref[...]
            o_ref[...] = x_ref[...] + (wt[:, 0:1] * buf_ref[cur, 0:tm, :] + wt[:, 1:2] * buf_ref[cur, tm:n, :])


def _combine(x2, wt, slot_tiles, yb):
    t = x2.shape[0]
    nt = slot_tiles.shape[0]
    tm = t // nt
    slots = lambda shift: pl.BlockSpec((1, 1, TOP_K * tm), lambda i: (jnp.minimum(i + shift, nt - 1), 0, 0),
                                       memory_space=pltpu.SMEM)
    return pl.pallas_call(
        _combine_body,
        grid=(nt,),
        in_specs=[
            slots(0), slots(1),
            pl.BlockSpec((tm, D_MODEL), lambda i: (i, 0)),
            pl.BlockSpec((tm, LANES), lambda i: (i, 0)),
            pl.BlockSpec(memory_space=pl.ANY),
        ],
        out_specs=pl.BlockSpec((tm, D_MODEL), lambda i: (i, 0)),
        out_shape=jax.ShapeDtypeStruct((t, D_MODEL), F32),
        scratch_shapes=[pltpu.VMEM((2, TOP_K * tm, D_MODEL), F32), pltpu.SemaphoreType.DMA((2,))],
        compiler_params=_params("arbitrary"),
        name="combine",
    )(slot_tiles, slot_tiles, x2, wt, yb)


def _moe(x2, g, router_w, e_gate, e_up, e_down):
    t = x2.shape[0]
    h, idx, wt = _router(x2, g, router_w)
    n_assign = t * TOP_K
    cap = -(-n_assign // MOE_BLOCK) * MOE_BLOCK + N_EXPERTS * MOE_BLOCK
    n_blocks = cap // MOE_BLOCK
    expert_flat = idx[:, :TOP_K].reshape(-1)
    onehot = (expert_flat[:, None] == jnp.arange(N_EXPERTS, dtype=jnp.int32)[None, :]).astype(jnp.int32)
    running = jnp.cumsum(onehot, axis=0)
    counts = running[-1]
    rank = jnp.sum(running * onehot, axis=1) - 1
    padded = (counts + MOE_BLOCK - 1) // MOE_BLOCK * MOE_BLOCK
    pends = jnp.cumsum(padded)
    pstarts = pends - padded
    dest = (pstarts[expert_flat] + rank).astype(jnp.int32)
    block_exp = jnp.minimum(jnp.searchsorted(pends, jnp.arange(n_blocks) * MOE_BLOCK, side="right"),
                            N_EXPERTS - 1).astype(jnp.int32)
    n_used = (pends[-1:] // MOE_BLOCK).astype(jnp.int32)
    tm = min(MOE_TOKEN_TILE, t)
    slot_tiles = dest.reshape(t // tm, tm, TOP_K).transpose(0, 2, 1).reshape(t // tm, 1, TOP_K * tm)
    xb = _dispatch(h, slot_tiles, cap)
    yb = _experts(xb, block_exp, n_used, e_gate, e_up, e_down)
    return _combine(x2, wt, slot_tiles, yb)


def _permute_w_in(w):
    main = jnp.concatenate([w[:, COL_Z:COL_XBC], w[:, :COL_Z], w[:, COL_POOL:], w[:, COL_V:COL_POOL]],
                           axis=1).astype(BF16)
    qkv = w[:, COL_DT:COL_V].astype(BF16)
    dt = jnp.pad(w[:, COL_XBC:COL_DT], ((0, 0), (0, LANES - SSD_HEADS))).astype(BF16)
    return main, qkv, dt


def kernel(x, mix_norm_g, w_in, conv_w, conv_b, dt_bias, a_log, d_skip, ssd_norm_g, q_norm_g, k_norm_g, pool_w,
           pool_scale, w_br_ssd, w_br_sb, w_br_pool, w_out, ffn_norm_g, ffn_w_gate, ffn_w_up, ffn_w_down,
           router_w, moe_w_gate, moe_w_up, moe_w_down):
    bsz, seq, d = x.shape
    depth = w_in.shape[0]
    x2 = x.reshape(bsz * seq, d)
    rowvec = lambda v: v.reshape(1, -1).astype(F32)
    for layer in range(depth):
        w_main, w_qkv, w_dt = _permute_w_in(w_in[layer])
        proj, dt_raw, qn, kn, vn = _inproj(x2, rowvec(mix_norm_g[layer]), w_main, w_qkv, w_dt,
                                           rowvec(q_norm_g[layer]), rowvec(k_norm_g[layer]), bsz, seq)
        y_sb = _sb_attention(qn, kn, vn).reshape(bsz * seq, SB_WIDTH)
        y_ssd = _ssd(proj, dt_raw, conv_w[layer], conv_b[layer], dt_bias[layer], a_log[layer], d_skip[layer],
                     ssd_norm_g[layer], bsz, seq)
        x2 = _merge(x2, y_ssd, y_sb, proj, pool_w[layer], pool_scale[layer], w_br_ssd[layer], w_br_sb[layer],
                    w_br_pool[layer], w_out[layer], seq)
        i = layer // 2
        if layer % 2 == 0:
            x2 = _ffn_dense(x2, rowvec(ffn_norm_g[layer]), ffn_w_gate[i], ffn_w_up[i], ffn_w_down[i])
        else:
            x2 = _moe(x2, rowvec(ffn_norm_g[layer]), router_w[i], moe_w_gate[i], moe_w_up[i], moe_w_down[i])
    return x2.reshape(bsz, seq, d)
```

```python
import functools
import math

import jax
import jax.numpy as jnp
from jax import lax
from jax.experimental import pallas as pl
from jax.experimental.pallas import tpu as pltpu

F32 = jnp.float32
BF16 = jnp.bfloat16

D_MODEL = 1024
EPS = 1e-6

SSD_INNER = 1024
SSD_HEAD_DIM = 64
SSD_HEADS = 16
SSD_GROUPS = 4
SSD_HEADS_PER_GROUP = 4
SSD_STATE = 128
SSD_CONV = 4
SSD_CHUNK = 128
SSD_CONV_DIM = SSD_INNER + 2 * SSD_GROUPS * SSD_STATE
SSD_CONV_TAIL = 16
SSD_CONV_PIECE = 512

SB_HEADS = 4
SB_HEAD_DIM = 128
SB_WIDTH = SB_HEADS * SB_HEAD_DIM
SB_TILE = 256
SB_CHAINS = 8

POOL_WINDOWS = (2, 4, 8, 16)
POOL_GROUPS = 4
POOL_WIDTH = 512
POOL_GROUP_DIM = 128
POOL_HALO = 16

N_BRANCHES = 3
FFN_DENSE = 2816
N_EXPERTS = 8
TOP_K = 2
FFN_EXPERT = 1792
MOE_BLOCK = 512
DISPATCH_TILE = 512
COMBINE_TILE = 256

COL_Z = SSD_INNER
COL_XBC = COL_Z + SSD_CONV_DIM
COL_DT = COL_XBC + SSD_HEADS
COL_Q = COL_DT + SB_WIDTH
COL_K = COL_Q + SB_WIDTH
COL_V = COL_K + SB_WIDTH
COL_POOL = COL_V + POOL_WIDTH

P_XBC = 0
P_Z = 2048
P_GATES = 3072
P_U = 6144
P_WIDTH = 6656

PROJ_DTYPE = BF16
LANES = 128
VMEM_LIMIT = 56 * 1024 * 1024
FFN_CHUNK = 256
INPROJ_CHUNK = 1024
SB_STOP = -110.0


def _params(*sem):
    return pltpu.CompilerParams(dimension_semantics=sem, vmem_limit_bytes=VMEM_LIMIT)


def _const_spec(shape):
    nd = len(shape)
    return pl.BlockSpec(shape, lambda *_: (0,) * nd, pipeline_mode=pl.Buffered(1))


def _split3(x):
    a = x.astype(BF16)
    r = x - a.astype(F32)
    b = r.astype(BF16)
    c = (r - b.astype(F32)).astype(BF16)
    return a, b, c


def _dot(a, b):
    return jnp.dot(a, b, preferred_element_type=F32)


def _dot_nt(a, b):
    return lax.dot_general(a, b, (((1,), (1,)), ((), ())), preferred_element_type=F32)


def _dot_f32_by_01(x, e01):
    a, b, c = _split3(x)
    return _dot(a, e01) + _dot(b, e01) + _dot(c, e01)


def _dot_01_by_f32(t01, x):
    a, b, c = _split3(x)
    return _dot(t01, a) + _dot(t01, b) + _dot(t01, c)


def _rms(x):
    return x * lax.rsqrt(jnp.mean(x * x, axis=-1, keepdims=True) + EPS)


def _silu(x):
    return x * jax.nn.sigmoid(x)


def _softplus(x):
    return jnp.maximum(x, 0.0) + jnp.log1p(jnp.exp(-jnp.abs(x)))


def _inproj_body(x_ref, g_ref, w_ref, wqkv_ref, wdt_ref, qg_ref, kg_ref, o_ref, dt_ref, qo_ref, ko_ref, vo_ref):
    xb = (_rms(x_ref[...]) * g_ref[...]).astype(BF16)
    dt_ref[...] = _dot(xb, wdt_ref[...])
    for lo in range(0, P_WIDTH, INPROJ_CHUNK):
        cs = slice(lo, min(lo + INPROJ_CHUNK, P_WIDTH))
        o_ref[:, cs] = _dot(xb, w_ref[:, cs]).astype(o_ref.dtype)
    scale = 1.0 / math.sqrt(SB_HEAD_DIM)
    q = _dot(xb, wqkv_ref[:, 0:SB_WIDTH])
    k = _dot(xb, wqkv_ref[:, SB_WIDTH:2 * SB_WIDTH])
    v = _dot(xb, wqkv_ref[:, 2 * SB_WIDTH:3 * SB_WIDTH])
    for h in range(SB_HEADS):
        sl = slice(h * SB_HEAD_DIM, (h + 1) * SB_HEAD_DIM)
        qo_ref[0, h] = ((_rms(q[:, sl]) * qg_ref[...]) * scale).astype(BF16)
        ko_ref[0, h] = (_rms(k[:, sl]) * kg_ref[...]).astype(BF16)
        vo_ref[0, h] = v[:, sl].astype(BF16)


def _inproj(x2, g, w_main, w_qkv, w_dt, qg, kg, bsz, seq):
    t = x2.shape[0]
    tm = min(512, seq)
    ns = seq // tm
    head_spec = pl.BlockSpec((1, SB_HEADS, tm, SB_HEAD_DIM), lambda i: (i // ns, 0, i % ns, 0))
    head_shape = jax.ShapeDtypeStruct((bsz, SB_HEADS, seq, SB_HEAD_DIM), BF16)
    return pl.pallas_call(
        _inproj_body,
        grid=(t // tm,),
        in_specs=[
            pl.BlockSpec((tm, D_MODEL), lambda i: (i, 0)),
            _const_spec((1, D_MODEL)),
            _const_spec((D_MODEL, P_WIDTH)),
            _const_spec((D_MODEL, 3 * SB_WIDTH)),
            _const_spec((D_MODEL, LANES)),
            _const_spec((1, SB_HEAD_DIM)),
            _const_spec((1, SB_HEAD_DIM)),
        ],
        out_specs=[
            pl.BlockSpec((tm, P_WIDTH), lambda i: (i, 0)),
            pl.BlockSpec((tm, LANES), lambda i: (i, 0)),
            head_spec, head_spec, head_spec,
        ],
        out_shape=[
            jax.ShapeDtypeStruct((t, P_WIDTH), PROJ_DTYPE),
            jax.ShapeDtypeStruct((t, LANES), F32),
            head_shape, head_shape, head_shape,
        ],
        compiler_params=_params("parallel"),
        name="inproj",
    )(x2, g, w_main, w_qkv, w_dt, qg, kg)


def _sb_body(q_ref, k_ref, v_ref, uo_ref, o_ref, carry_ref, acc_ref):
    t = SB_TILE
    first = pl.program_id(2) * SB_CHAINS
    uo = uo_ref[...]
    row = lax.broadcasted_iota(jnp.int32, (t, t), 0)
    col = lax.broadcasted_iota(jnp.int32, (t, t), 1)
    mask = col < row

    def visit(c, kb, diag):
        start = pl.multiple_of(jnp.maximum(kb, 0) * t, t)
        q = q_ref[0, 0, c * t:(c + 1) * t, :]
        k = k_ref[0, 0, pl.ds(start, t), :]
        v = v_ref[0, 0, pl.ds(start, t), :]
        z = _dot_nt(q, k)
        neg_log_rest = jnp.maximum(z, 0.0) + jnp.log(1.0 + jnp.exp(-jnp.abs(z)))
        log_beta = z - neg_log_rest
        if diag:
            neg_log_rest = jnp.where(mask, neg_log_rest, 0.0)
        sr = _dot(neg_log_rest.astype(BF16), uo)
        if diag:
            w = jnp.where(mask, jnp.exp(log_beta - sr[:, :t]), 0.0)
            carry = -sr[:, t:]
            acc_ref[c] = _dot(w.astype(BF16), v)
        else:
            prev = carry_ref[c]
            w = jnp.exp(log_beta - sr[:, :t] + jnp.concatenate([prev] * (t // LANES), axis=1))
            w = jnp.where(kb >= 0, w, 0.0)
            carry = prev - sr[:, t:]
            acc_ref[c] += _dot(w.astype(BF16), v)
        carry_ref[c] = carry
        return jnp.max(carry)

    def unfinished(j, maxima):
        need = [jnp.logical_and(first + c - j >= 0, maxima[c] > SB_STOP) for c in range(SB_CHAINS)]
        return functools.reduce(jnp.logical_or, need).astype(jnp.int32)

    maxima = [visit(c, first + c, True) for c in range(SB_CHAINS)]

    def body(state):
        j, _ = state
        maxima = [visit(c, first + c - j, False) for c in range(SB_CHAINS)]
        return j + 1, unfinished(j + 1, maxima)

    lax.while_loop(lambda state: state[1] > 0, body, (jnp.int32(1), unfinished(1, maxima)))
    for c in range(SB_CHAINS):
        o_ref[0, c * t:(c + 1) * t, :] = acc_ref[c].astype(o_ref.dtype)


def _sb_attention(qn, kn, vn):
    bsz, _, seq, _ = qn.shape
    t = SB_TILE
    tq = t * SB_CHAINS
    assert seq % tq == 0
    r = jnp.arange(t)
    upper = (r[:, None] > r[None, :])
    uo = jnp.concatenate([upper, jnp.ones((t, LANES), bool)], axis=1).astype(BF16)
    kv_spec = pl.BlockSpec((1, 1, seq, SB_HEAD_DIM), lambda b, h, i: (b, h, 0, 0))
    return pl.pallas_call(
        _sb_body,
        grid=(bsz, SB_HEADS, seq // tq),
        in_specs=[pl.BlockSpec((1, 1, tq, SB_HEAD_DIM), lambda b, h, i: (b, h, i, 0)),
                  kv_spec, kv_spec, _const_spec((t, t + LANES))],
        out_specs=pl.BlockSpec((1, tq, SB_HEAD_DIM), lambda b, h, i: (b, i, h)),
        out_shape=jax.ShapeDtypeStruct((bsz, seq, SB_WIDTH), BF16),
        scratch_shapes=[pltpu.VMEM((SB_CHAINS, t, LANES), F32), pltpu.VMEM((SB_CHAINS, t, SB_HEAD_DIM), F32)],
        compiler_params=_params("parallel", "parallel", "arbitrary"),
        name="sb_attention",
    )(qn, kn, vn, uo)


def _ssd_body(xbcs_ref, zs_ref, dts_ref, cw_ref, cb_ref, dtb_ref, alog_ref, dskip_ref, ng_ref, e_ref, tri_ref,
              shift_ref, os_ref, bufs_ref, sts_ref, ys_ref):
    tail = SSD_CONV_TAIL

    @pl.when(pl.program_id(0) == 0)
    def _():
        bufs_ref[:, 0:tail, :] = jnp.zeros((bufs_ref.shape[0], tail, SSD_CONV_DIM), bufs_ref.dtype)
        sts_ref[...] = jnp.zeros_like(sts_ref)

    for b in range(xbcs_ref.shape[0]):
        _ssd_chunk(xbcs_ref.at[b], zs_ref.at[b], dts_ref.at[b], cw_ref, cb_ref, dtb_ref, alog_ref, dskip_ref, ng_ref,
                   e_ref, tri_ref, shift_ref, os_ref.at[b], bufs_ref.at[b], sts_ref.at[b], ys_ref.at[b])


def _ssd_chunk(xbc_ref, z_ref, dt_ref, cw_ref, cb_ref, dtb_ref, alog_ref, dskip_ref, ng_ref, e_ref, tri_ref,
               shift_ref, o_ref, buf_ref, st_ref, y_ref):
    L = SSD_CHUNK
    G, R, P, N = SSD_GROUPS, SSD_HEADS_PER_GROUP, SSD_HEAD_DIM, SSD_STATE
    tail = SSD_CONV_TAIL
    buf_ref[tail:tail + L, :] = xbc_ref[...]

    def conv_silu_piece(lo, hi):
        delayed = _dot(shift_ref[...], buf_ref[:, lo:hi])
        acc = cb_ref[:, lo:hi] + cw_ref[SSD_CONV - 1:SSD_CONV, lo:hi] * xbc_ref[:, lo:hi].astype(F32)
        for k in range(SSD_CONV - 1):
            acc = acc + cw_ref[k:k + 1, lo:hi] * delayed[k * L:(k + 1) * L, :]
        return _silu(acc)

    def conv_silu(lo, hi):
        step = SSD_CONV_PIECE
        return jnp.concatenate([conv_silu_piece(c, c + step) for c in range(lo, hi, step)], axis=1)

    xs = conv_silu(0, SSD_INNER)
    b_in = conv_silu(SSD_INNER, SSD_INNER + G * N)
    c_in = conv_silu(SSD_INNER + G * N, SSD_CONV_DIM)
    buf_ref[0:tail, :] = buf_ref[L:L + tail, :]

    e01 = e_ref[...]
    dt = _softplus(dt_ref[...] + dtb_ref[...])
    da = dt * (-jnp.exp(alog_ref[...]))
    a_cs = _dot_01_by_f32(tri_ref[...], da)
    a_cs_t = a_cs.T
    a_full = _dot_f32_by_01(a_cs, e01)
    dt_full = _dot_f32_by_01(dt, e01)
    a_last = a_full[L - 1:L, :]
    ea_full = jnp.exp(a_full)
    x_dt = xs * dt_full
    xw = (x_dt * jnp.exp(a_last - a_full)).astype(BF16)
    ea_last = jnp.exp(a_last)

    row = lax.broadcasted_iota(jnp.int32, (L, L), 0)
    col = lax.broadcasted_iota(jnp.int32, (L, L), 1)
    causal = col <= row
    lane = lax.broadcasted_iota(jnp.int32, (L, LANES), 1)
    first_head = lane < P

    for g in range(G):
        gs = slice(g * N, (g + 1) * N)
        cg = c_in[:, gs].astype(BF16)
        bg_f32 = b_in[:, gs]
        cb = _dot_nt(cg, bg_f32.astype(BF16))
        for pair in range(R // 2):
            ms = []
            for r in range(2):
                h = g * R + pair * 2 + r
                seg = a_cs[:, h:h + 1] - a_cs_t[h:h + 1, :]
                decay = jnp.exp(jnp.where(causal, seg, -jnp.inf))
                ms.append((cb * decay).astype(BF16))
            c0 = (g * R + pair * 2) * P
            xp = x_dt[:, c0:c0 + LANES]
            rhs = jnp.concatenate([jnp.where(first_head, xp, 0.0).astype(BF16),
                                   jnp.where(first_head, 0.0, xp).astype(BF16)], axis=0)
            y_ref[:, c0:c0 + LANES] = _dot(jnp.concatenate(ms, axis=1), rhs)
        cs = slice(g * R * P, (g + 1) * R * P)
        state = st_ref[g]
        y_ref[:, cs] += _dot(cg, state.astype(BF16)) * ea_full[:, cs]
        st_ref[g] = state * ea_last[:, cs] + _dot(bg_f32.T.astype(BF16), xw[:, cs])

    y = y_ref[...] + xs * dskip_ref[...]
    y = y * _silu(z_ref[...].astype(F32))
    gw = SSD_INNER // G
    for g in range(G):
        cs = slice(g * gw, (g + 1) * gw)
        o_ref[:, cs] = (_rms(y[:, cs]) * ng_ref[:, cs]).astype(o_ref.dtype)


def _ssd(proj, dt_raw, conv_w, conv_b, dt_bias, a_log, d_skip, norm_g, bsz, seq):
    L = SSD_CHUNK
    nc = seq // L
    t = bsz * seq
    pad = LANES - SSD_HEADS
    heads = jnp.arange(LANES)
    cols = jnp.arange(SSD_INNER) // SSD_HEAD_DIM
    e01 = (heads[:, None] == cols[None, :]).astype(BF16)
    r = jnp.arange(L)
    tri = (r[None, :] <= r[:, None]).astype(BF16)
    assert proj.dtype == BF16, "the 0/1 shift product is exact only on bf16 data"
    delay = (SSD_CONV - 1) - jnp.arange((SSD_CONV - 1) * L) // L
    src = SSD_CONV_TAIL + jnp.arange((SSD_CONV - 1) * L) % L - delay
    shift = (src[:, None] == jnp.arange(SSD_CONV_TAIL + L)[None, :]).astype(BF16)
    rowvec = lambda v: v.reshape(1, -1).astype(F32)
    proj3 = proj.reshape(bsz, seq, proj.shape[-1])
    return pl.pallas_call(
        _ssd_body,
        grid=(nc,),
        in_specs=[
            pl.BlockSpec((bsz, L, SSD_CONV_DIM), lambda c: (0, c, P_XBC // SSD_CONV_DIM)),
            pl.BlockSpec((bsz, L, SSD_INNER), lambda c: (0, c, P_Z // SSD_INNER)),
            pl.BlockSpec((bsz, L, LANES), lambda c: (0, c, 0)),
            _const_spec((SSD_CONV, SSD_CONV_DIM)),
            _const_spec((1, SSD_CONV_DIM)),
            _const_spec((1, LANES)),
            _const_spec((1, LANES)),
            _const_spec((1, SSD_INNER)),
            _const_spec((1, SSD_INNER)),
            _const_spec((LANES, SSD_INNER)),
            _const_spec((L, L)),
            _const_spec(((SSD_CONV - 1) * L, SSD_CONV_TAIL + L)),
        ],
        out_specs=pl.BlockSpec((bsz, L, SSD_INNER), lambda c: (0, c, 0)),
        out_shape=jax.ShapeDtypeStruct((bsz, seq, SSD_INNER), BF16),
        scratch_shapes=[
            pltpu.VMEM((bsz, SSD_CONV_TAIL + L, SSD_CONV_DIM), BF16),
            pltpu.VMEM((bsz, SSD_GROUPS, SSD_STATE, SSD_HEADS_PER_GROUP * SSD_HEAD_DIM), F32),
            pltpu.VMEM((bsz, L, SSD_INNER), F32),
        ],
        compiler_params=_params("arbitrary"),
        name="ssd",
    )(proj3, proj3, dt_raw.reshape(bsz, seq, LANES), conv_w.astype(F32), rowvec(conv_b),
      jnp.pad(rowvec(dt_bias), ((0, 0), (0, pad))), jnp.pad(rowvec(a_log), ((0, 0), (0, pad))),
      rowvec(jnp.repeat(d_skip, SSD_HEAD_DIM)), rowvec(norm_g), e01, tri, shift).reshape(t, SSD_INNER)


def _merge_body(x_ref, yssd_ref, ysb_ref, u_ref, halo_ref, g0_ref, g1_ref, g2_ref,
                wssd_ref, wsb_ref, wpool_ref, wout_ref, pw_ref, ps_ref, o_ref, ext_ref, *, seq):
    tm = x_ref.shape[0]
    H = POOL_HALO
    start = (pl.program_id(0) * tm) % seq
    halo = halo_ref[...].astype(F32)
    ext_ref[0:H, :] = jnp.where(start == 0, jnp.zeros_like(halo), halo)
    ext_ref[H:H + tm, :] = u_ref[...].astype(F32)
    pos = start + lax.broadcasted_iota(jnp.int32, (tm, 1), 0)

    merged = jax.nn.sigmoid(g0_ref[...].astype(F32)) * _dot(yssd_ref[...], wssd_ref[...])
    merged += jax.nn.sigmoid(g1_ref[...].astype(F32)) * _dot(ysb_ref[...], wsb_ref[...])

    ypool = jnp.zeros((tm, D_MODEL), F32)
    for gi, win in enumerate(POOL_WINDOWS):
        cs = slice(gi * POOL_GROUP_DIM, (gi + 1) * POOL_GROUP_DIM)
        cur = ext_ref[H:H + tm, cs]
        wsum = cur
        for k in range(1, win):
            wsum = wsum + ext_ref[H - k:H - k + tm, cs]
        count = jnp.minimum(pos + 1, win).astype(F32)
        pooled = wsum / count - cur
        mixed = _dot(pooled.astype(BF16), pw_ref[gi]) * ps_ref[:, cs]
        ypool += _dot(mixed.astype(BF16), wpool_ref[cs, :])
    merged += jax.nn.sigmoid(g2_ref[...].astype(F32)) * ypool
    o_ref[...] = x_ref[...] + _dot(merged.astype(BF16), wout_ref[...])


def _merge(x2, y_ssd, y_sb, proj, pool_w, pool_scale, w_br_ssd, w_br_sb, w_br_pool, w_out, seq):
    t = x2.shape[0]
    tm = min(512, seq)
    hb = tm // POOL_HALO
    gate = lambda k: pl.BlockSpec((tm, D_MODEL), lambda i: (i, P_GATES // D_MODEL + k))
    return pl.pallas_call(
        functools.partial(_merge_body, seq=seq),
        grid=(t // tm,),
        in_specs=[
            pl.BlockSpec((tm, D_MODEL), lambda i: (i, 0)),
            pl.BlockSpec((tm, SSD_INNER), lambda i: (i, 0)),
            pl.BlockSpec((tm, SB_WIDTH), lambda i: (i, 0)),
            pl.BlockSpec((tm, POOL_WIDTH), lambda i: (i, P_U // POOL_WIDTH)),
            pl.BlockSpec((POOL_HALO, POOL_WIDTH), lambda i: (jnp.maximum(i * hb - 1, 0), P_U // POOL_WIDTH)),
            gate(0), gate(1), gate(2),
            _const_spec((SSD_INNER, D_MODEL)),
            _const_spec((SB_WIDTH, D_MODEL)),
            _const_spec((POOL_WIDTH, D_MODEL)),
            _const_spec((D_MODEL, D_MODEL)),
            _const_spec((POOL_GROUPS, POOL_GROUP_DIM, POOL_GROUP_DIM)),
            _const_spec((1, POOL_WIDTH)),
        ],
        out_specs=pl.BlockSpec((tm, D_MODEL), lambda i: (i, 0)),
        out_shape=jax.ShapeDtypeStruct((t, D_MODEL), F32),
        scratch_shapes=[pltpu.VMEM((tm + POOL_HALO, POOL_WIDTH), F32)],
        compiler_params=_params("parallel"),
        name="merge",
    )(x2, y_ssd, y_sb, proj, proj, proj, proj, proj,
      w_br_ssd.astype(BF16), w_br_sb.astype(BF16), w_br_pool.astype(BF16), w_out.astype(BF16),
      pool_w.astype(BF16), pool_scale.reshape(1, -1).astype(F32))


def _swiglu_into(acc_ref, xb, wg_ref, wu_ref, wd_ref, width):
    for c in range(width // FFN_CHUNK):
        cs = slice(c * FFN_CHUNK, (c + 1) * FFN_CHUNK)
        hidden = _silu(_dot(xb, wg_ref[:, cs])) * _dot(xb, wu_ref[:, cs])
        acc_ref[...] += _dot(hidden.astype(BF16), wd_ref[cs, :])


def _ffn_body(x_ref, g_ref, wg_ref, wu_ref, wd_ref, o_ref):
    x = x_ref[...]
    o_ref[...] = x
    _swiglu_into(o_ref, (_rms(x) * g_ref[...]).astype(BF16), wg_ref, wu_ref, wd_ref, FFN_DENSE)


def _ffn_dense(x2, g, w_gate, w_up, w_down):
    t = x2.shape[0]
    tm = min(512, t)
    return pl.pallas_call(
        _ffn_body,
        grid=(t // tm,),
        in_specs=[
            pl.BlockSpec((tm, D_MODEL), lambda i: (i, 0)),
            _const_spec((1, D_MODEL)),
            _const_spec((D_MODEL, FFN_DENSE)),
            _const_spec((D_MODEL, FFN_DENSE)),
            _const_spec((FFN_DENSE, D_MODEL)),
        ],
        out_specs=pl.BlockSpec((tm, D_MODEL), lambda i: (i, 0)),
        out_shape=jax.ShapeDtypeStruct((t, D_MODEL), F32),
        compiler_params=_params("parallel"),
        name="ffn_dense",
    )(x2, g, w_gate.astype(BF16), w_up.astype(BF16), w_down.astype(BF16))


def _router_body(x_ref, g_ref, rw_ref, h_ref, idx_ref, wt_ref):
    h = _rms(x_ref[...]) * g_ref[...]
    h_ref[...] = h
    rw = rw_ref[...]
    h_hi = h.astype(BF16)
    h_lo = (h - h_hi.astype(F32)).astype(BF16)
    w_hi = rw.astype(BF16)
    w_lo = (rw - w_hi.astype(F32)).astype(BF16)
    logits = _dot(h_hi, w_hi) + (_dot(h_hi, w_lo) + _dot(h_lo, w_hi)) + _dot(h_lo, w_lo)
    lane = lax.broadcasted_iota(jnp.int32, logits.shape, 1)
    lane_f = lane.astype(F32)
    logits = jnp.where(lane < N_EXPERTS, logits, -jnp.inf)
    m1 = jnp.max(logits, axis=-1, keepdims=True)
    i1 = jnp.min(jnp.where(logits == m1, lane_f, float(LANES)), axis=-1, keepdims=True)
    rest = jnp.where(lane_f == i1, -jnp.inf, logits)
    m2 = jnp.max(rest, axis=-1, keepdims=True)
    i2 = jnp.min(jnp.where(rest == m2, lane_f, float(LANES)), axis=-1, keepdims=True)
    e = jnp.exp(m2 - m1)
    w1 = 1.0 / (1.0 + e)
    idx_ref[...] = jnp.where(lane == 0, i1, jnp.where(lane == 1, i2, 0.0)).astype(jnp.int32)
    wt_ref[...] = jnp.where(lane == 0, w1, jnp.where(lane == 1, e * w1, 0.0))


def _router(x2, g, router_w):
    t = x2.shape[0]
    tm = min(512, t)
    rw = jnp.pad(router_w.astype(F32), ((0, 0), (0, LANES - N_EXPERTS)))
    row = pl.BlockSpec((tm, D_MODEL), lambda i: (i, 0))
    small = pl.BlockSpec((tm, LANES), lambda i: (i, 0))
    return pl.pallas_call(
        _router_body,
        grid=(t // tm,),
        in_specs=[row, _const_spec((1, D_MODEL)), _const_spec((D_MODEL, LANES))],
        out_specs=[row, small, small],
        out_shape=[jax.ShapeDtypeStruct((t, D_MODEL), F32),
                   jax.ShapeDtypeStruct((t, LANES), jnp.int32),
                   jax.ShapeDtypeStruct((t, LANES), F32)],
        compiler_params=_params("parallel"),
        name="router",
    )(x2, g, rw)


def _dispatch_body(pends_ref, counts_ref, slot_ref, h_ref, xb_hbm, zero_ref, sem):
    tm = h_ref.shape[0]

    def zero_block(start):
        fill = pltpu.make_async_copy(zero_ref, xb_hbm.at[pl.ds(pl.multiple_of(start, MOE_BLOCK), MOE_BLOCK)], sem)
        fill.start()
        fill.wait()

    @pl.when(pl.program_id(0) == 0)
    def _():
        zero_ref[...] = jnp.zeros_like(zero_ref)
        for e in range(N_EXPERTS):
            pl.when(counts_ref[e] > 0)(functools.partial(zero_block, pends_ref[e] - MOE_BLOCK))
        for j in range(N_EXPERTS):
            start = pends_ref[N_EXPERTS - 1] + j * MOE_BLOCK
            pl.when(start < xb_hbm.shape[0])(functools.partial(zero_block, start))

    for k in range(TOP_K):
        for r in range(tm):
            pltpu.make_async_copy(h_ref.at[pl.ds(r, 1)], xb_hbm.at[pl.ds(slot_ref[0, 0, k * tm + r], 1)], sem).start()
    for k in range(TOP_K):
        pltpu.make_async_copy(h_ref, xb_hbm.at[pl.ds(0, tm)], sem).wait()


def _dispatch(h, slot_tiles, pends, counts, cap):
    t = h.shape[0]
    nt = slot_tiles.shape[0]
    tm = t // nt
    grid_spec = pltpu.PrefetchScalarGridSpec(
        num_scalar_prefetch=2,
        grid=(nt,),
        in_specs=[
            pl.BlockSpec((1, 1, TOP_K * tm), lambda i, pe, co: (i, 0, 0), memory_space=pltpu.SMEM),
            pl.BlockSpec((tm, D_MODEL), lambda i, pe, co: (i, 0)),
        ],
        out_specs=pl.BlockSpec(memory_space=pl.ANY),
        scratch_shapes=[pltpu.VMEM((MOE_BLOCK, D_MODEL), F32), pltpu.SemaphoreType.DMA],
    )
    return pl.pallas_call(
        _dispatch_body,
        grid_spec=grid_spec,
        out_shape=jax.ShapeDtypeStruct((cap, D_MODEL), F32),
        compiler_params=_params("arbitrary"),
        name="dispatch",
    )(pends.astype(jnp.int32), counts.astype(jnp.int32), slot_tiles, h)


def _experts_body(bexp_ref, nused_ref, xb_ref, wg_ref, wu_ref, wd_ref, o_ref):
    del bexp_ref
    o_ref[...] = jnp.zeros_like(o_ref)

    @pl.when(pl.program_id(0) < nused_ref[0])
    def _():
        _swiglu_into(o_ref, xb_ref[...].astype(BF16), wg_ref, wu_ref, wd_ref, FFN_EXPERT)


def _experts(xb, block_exp, n_used, e_gate, e_up, e_down):
    n_blocks = block_exp.shape[0]
    wspec = lambda shape: pl.BlockSpec((None,) + shape, lambda i, be, nu: (be[i], 0, 0))
    rows = pl.BlockSpec((MOE_BLOCK, D_MODEL), lambda i, be, nu: (i, 0))
    used_rows = pl.BlockSpec((MOE_BLOCK, D_MODEL), lambda i, be, nu: (jnp.minimum(i, nu[0] - 1), 0))
    grid_spec = pltpu.PrefetchScalarGridSpec(
        num_scalar_prefetch=2,
        grid=(n_blocks,),
        in_specs=[used_rows, wspec((D_MODEL, FFN_EXPERT)), wspec((D_MODEL, FFN_EXPERT)),
                  wspec((FFN_EXPERT, D_MODEL))],
        out_specs=rows,
    )
    return pl.pallas_call(
        _experts_body,
        grid_spec=grid_spec,
        out_shape=jax.ShapeDtypeStruct((n_blocks * MOE_BLOCK, D_MODEL), F32),
        compiler_params=_params("arbitrary"),
        name="experts",
    )(block_exp, n_used, xb, e_gate.astype(BF16), e_up.astype(BF16), e_down.astype(BF16))


def _combine_body(slot_ref, next_ref, x_ref, wt_ref, yb_hbm, o_ref, buf_ref, sem):
    tm = x_ref.shape[0]
    n = TOP_K * tm
    i = pl.program_id(0)

    def fetch(idx_ref, b):
        for r in range(n):
            pltpu.make_async_copy(yb_hbm.at[pl.ds(idx_ref[0, 0, r], 1)], buf_ref.at[b, pl.ds(r, 1)], sem.at[b]).start()

    @pl.when(i == 0)
    def _():
        fetch(slot_ref, 0)

    for cur in range(2):
        @pl.when(i % 2 == cur)
        def _():
            @pl.when(i + 1 < pl.num_programs(0))
            def _():
                fetch(next_ref, 1 - cur)

            pltpu.make_async_copy(yb_hbm.at[pl.ds(0, n)], buf_ref.at[cur], sem.at[cur]).wait()
            wt = wt_ref[...]
            o_ref[...] = x_ref[...] + (wt[:, 0:1] * buf_ref[cur, 0:tm, :] + wt[:, 1:2] * buf_ref[cur, tm:n, :])


def _combine(x2, wt, slot_tiles, yb):
    t = x2.shape[0]
    nt = slot_tiles.shape[0]
    tm = t // nt
    slots = lambda shift: pl.BlockSpec((1, 1, TOP_K * tm), lambda i: (jnp.minimum(i + shift, nt - 1), 0, 0),
                                       memory_space=pltpu.SMEM)
    return pl.pallas_call(
        _combine_body,
        grid=(nt,),
        in_specs=[
            slots(0), slots(1),
            pl.BlockSpec((tm, D_MODEL), lambda i: (i, 0)),
            pl.BlockSpec((tm, LANES), lambda i: (i, 0)),
            pl.BlockSpec(memory_space=pl.ANY),
        ],
        out_specs=pl.BlockSpec((tm, D_MODEL), lambda i: (i, 0)),
        out_shape=jax.ShapeDtypeStruct((t, D_MODEL), F32),
        scratch_shapes=[pltpu.VMEM((2, TOP_K * tm, D_MODEL), F32), pltpu.SemaphoreType.DMA((2,))],
        compiler_params=_params("arbitrary"),
        name="combine",
    )(slot_tiles, slot_tiles, x2, wt, yb)


def _moe(x2, g, router_w, e_gate, e_up, e_down):
    t = x2.shape[0]
    h, idx, wt = _router(x2, g, router_w)
    n_assign = t * TOP_K
    cap = -(-n_assign // MOE_BLOCK) * MOE_BLOCK + N_EXPERTS * MOE_BLOCK
    n_blocks = cap // MOE_BLOCK
    expert_flat = idx[:, :TOP_K].reshape(-1)
    onehot = (expert_flat[:, None] == jnp.arange(N_EXPERTS, dtype=jnp.int32)[None, :]).astype(jnp.int32)
    running = jnp.cumsum(onehot, axis=0)
    counts = running[-1]
    rank = jnp.sum(running * onehot, axis=1) - 1
    padded = (counts + MOE_BLOCK - 1) // MOE_BLOCK * MOE_BLOCK
    pends = jnp.cumsum(padded)
    pstarts = pends - padded
    dest = (pstarts[expert_flat] + rank).astype(jnp.int32)
    block_exp = jnp.minimum(jnp.searchsorted(pends, jnp.arange(n_blocks) * MOE_BLOCK, side="right"),
                            N_EXPERTS - 1).astype(jnp.int32)
    n_used = (pends[-1:] // MOE_BLOCK).astype(jnp.int32)
    def slot_tiles(tile):
        tm = min(tile, t)
        return dest.reshape(t // tm, tm, TOP_K).transpose(0, 2, 1).reshape(t // tm, 1, TOP_K * tm)

    xb = _dispatch(h, slot_tiles(DISPATCH_TILE), pends, counts, cap)
    yb = _experts(xb, block_exp, n_used, e_gate, e_up, e_down)
    return _combine(x2, wt, slot_tiles(COMBINE_TILE), yb)


def _permute_w_in(w):
    main = jnp.concatenate([w[:, COL_Z:COL_XBC], w[:, :COL_Z], w[:, COL_POOL:], w[:, COL_V:COL_POOL]],
                           axis=1).astype(BF16)
    qkv = w[:, COL_DT:COL_V].astype(BF16)
    dt = jnp.pad(w[:, COL_XBC:COL_DT], ((0, 0), (0, LANES - SSD_HEADS))).astype(BF16)
    return main, qkv, dt


def kernel(x, mix_norm_g, w_in, conv_w, conv_b, dt_bias, a_log, d_skip, ssd_norm_g, q_norm_g, k_norm_g, pool_w,
           pool_scale, w_br_ssd, w_br_sb, w_br_pool, w_out, ffn_norm_g, ffn_w_gate, ffn_w_up, ffn_w_down,
           router_w, moe_w_gate, moe_w_up, moe_w_down):
    bsz, seq, d = x.shape
    depth = w_in.shape[0]
    x2 = x.reshape(bsz * seq, d)
    rowvec = lambda v: v.reshape(1, -1).astype(F32)
    for layer in range(depth):
        w_main, w_qkv, w_dt = _permute_w_in(w_in[layer])
        proj, dt_raw, qn, kn, vn = _inproj(x2, rowvec(mix_norm_g[layer]), w_main, w_qkv, w_dt,
                                           rowvec(q_norm_g[layer]), rowvec(k_norm_g[layer]), bsz, seq)
        y_sb = _sb_attention(qn, kn, vn).reshape(bsz * seq, SB_WIDTH)
        y_ssd = _ssd(proj, dt_raw, conv_w[layer], conv_b[layer], dt_bias[layer], a_log[layer], d_skip[layer],
                     ssd_norm_g[layer], bsz, seq)
        x2 = _merge(x2, y_ssd, y_sb, proj, pool_w[layer], pool_scale[layer], w_br_ssd[layer], w_br_sb[layer],
                    w_br_pool[layer], w_out[layer], seq)
        i = layer // 2
        if layer % 2 == 0:
            x2 = _ffn_dense(x2, rowvec(ffn_norm_g[layer]), ffn_w_gate[i], ffn_w_up[i], ffn_w_down[i])
        else:
            x2 = _moe(x2, rowvec(ffn_norm_g[layer]), router_w[i], moe_w_gate[i], moe_w_up[i], moe_w_down[i])
    return x2.reshape(bsz, seq, d)
```

```python
import functools
import math

import jax
import jax.numpy as jnp
from jax import lax
from jax.experimental import pallas as pl
from jax.experimental.pallas import tpu as pltpu

F32 = jnp.float32
BF16 = jnp.bfloat16

D_MODEL = 1024
EPS = 1e-6

SSD_INNER = 1024
SSD_HEAD_DIM = 64
SSD_HEADS = 16
SSD_GROUPS = 4
SSD_HEADS_PER_GROUP = 4
SSD_STATE = 128
SSD_CONV = 4
SSD_CHUNK = 128
SSD_CONV_DIM = SSD_INNER + 2 * SSD_GROUPS * SSD_STATE
SSD_CONV_TAIL = 16
SSD_CONV_PIECE = 512

SB_HEADS = 4
SB_HEAD_DIM = 128
SB_WIDTH = SB_HEADS * SB_HEAD_DIM
SB_TILE = 256
SB_CHAINS = 16

POOL_WINDOWS = (2, 4, 8, 16)
POOL_GROUPS = 4
POOL_WIDTH = 512
POOL_GROUP_DIM = 128
POOL_HALO = 16

N_BRANCHES = 3
FFN_DENSE = 2816
N_EXPERTS = 8
TOP_K = 2
FFN_EXPERT = 1792
MOE_BLOCK = 512
DISPATCH_TILE = 512
COMBINE_TILE = 256

COL_Z = SSD_INNER
COL_XBC = COL_Z + SSD_CONV_DIM
COL_DT = COL_XBC + SSD_HEADS
COL_Q = COL_DT + SB_WIDTH
COL_K = COL_Q + SB_WIDTH
COL_V = COL_K + SB_WIDTH
COL_POOL = COL_V + POOL_WIDTH

P_XBC = 0
P_Z = 2048
P_GATES = 3072
P_U = 6144
P_WIDTH = 6656

PROJ_DTYPE = BF16
LANES = 128
VMEM_LIMIT = 56 * 1024 * 1024
FFN_CHUNK = 256
INPROJ_CHUNK = 1024
SB_STOP = -110.0


def _params(*sem):
    return pltpu.CompilerParams(dimension_semantics=sem, vmem_limit_bytes=VMEM_LIMIT)


def _const_spec(shape):
    nd = len(shape)
    return pl.BlockSpec(shape, lambda *_: (0,) * nd, pipeline_mode=pl.Buffered(1))


def _split3(x):
    a = x.astype(BF16)
    r = x - a.astype(F32)
    b = r.astype(BF16)
    c = (r - b.astype(F32)).astype(BF16)
    return a, b, c


def _dot(a, b):
    return jnp.dot(a, b, preferred_element_type=F32)


def _dot_nt(a, b):
    return lax.dot_general(a, b, (((1,), (1,)), ((), ())), preferred_element_type=F32)


def _dot_f32_by_01(x, e01):
    a, b, c = _split3(x)
    return _dot(a, e01) + _dot(b, e01) + _dot(c, e01)


def _dot_01_by_f32(t01, x):
    a, b, c = _split3(x)
    return _dot(t01, a) + _dot(t01, b) + _dot(t01, c)


def _rms(x):
    return x * lax.rsqrt(jnp.mean(x * x, axis=-1, keepdims=True) + EPS)


def _silu(x):
    return x * jax.nn.sigmoid(x)


def _softplus(x):
    return jnp.maximum(x, 0.0) + jnp.log1p(jnp.exp(-jnp.abs(x)))


def _inproj_body(x_ref, g_ref, w_ref, wqkv_ref, wdt_ref, qg_ref, kg_ref, o_ref, dt_ref, qo_ref, ko_ref, vo_ref):
    xb = (_rms(x_ref[...]) * g_ref[...]).astype(BF16)
    dt_ref[...] = _dot(xb, wdt_ref[...])
    for lo in range(0, P_WIDTH, INPROJ_CHUNK):
        cs = slice(lo, min(lo + INPROJ_CHUNK, P_WIDTH))
        o_ref[:, cs] = _dot(xb, w_ref[:, cs]).astype(o_ref.dtype)
    scale = 1.0 / math.sqrt(SB_HEAD_DIM)
    q = _dot(xb, wqkv_ref[:, 0:SB_WIDTH])
    k = _dot(xb, wqkv_ref[:, SB_WIDTH:2 * SB_WIDTH])
    v = _dot(xb, wqkv_ref[:, 2 * SB_WIDTH:3 * SB_WIDTH])
    for h in range(SB_HEADS):
        sl = slice(h * SB_HEAD_DIM, (h + 1) * SB_HEAD_DIM)
        qo_ref[0, h] = ((_rms(q[:, sl]) * qg_ref[...]) * scale).astype(BF16)
        ko_ref[0, h] = (_rms(k[:, sl]) * kg_ref[...]).astype(BF16)
        vo_ref[0, h] = v[:, sl].astype(BF16)


def _inproj(x2, g, w_main, w_qkv, w_dt, qg, kg, bsz, seq):
    t = x2.shape[0]
    tm = min(512, seq)
    ns = seq // tm
    head_spec = pl.BlockSpec((1, SB_HEADS, tm, SB_HEAD_DIM), lambda i: (i // ns, 0, i % ns, 0))
    head_shape = jax.ShapeDtypeStruct((bsz, SB_HEADS, seq, SB_HEAD_DIM), BF16)
    return pl.pallas_call(
        _inproj_body,
        grid=(t // tm,),
        in_specs=[
            pl.BlockSpec((tm, D_MODEL), lambda i: (i, 0)),
            _const_spec((1, D_MODEL)),
            _const_spec((D_MODEL, P_WIDTH)),
            _const_spec((D_MODEL, 3 * SB_WIDTH)),
            _const_spec((D_MODEL, LANES)),
            _const_spec((1, SB_HEAD_DIM)),
            _const_spec((1, SB_HEAD_DIM)),
        ],
        out_specs=[
            pl.BlockSpec((tm, P_WIDTH), lambda i: (i, 0)),
            pl.BlockSpec((tm, LANES), lambda i: (i, 0)),
            head_spec, head_spec, head_spec,
        ],
        out_shape=[
            jax.ShapeDtypeStruct((t, P_WIDTH), PROJ_DTYPE),
            jax.ShapeDtypeStruct((t, LANES), F32),
            head_shape, head_shape, head_shape,
        ],
        compiler_params=_params("parallel"),
        name="inproj",
    )(x2, g, w_main, w_qkv, w_dt, qg, kg)


def _sb_body(q_ref, k_ref, v_ref, uo_ref, o_ref, carry_ref, acc_ref):
    t = SB_TILE
    first = pl.program_id(2) * SB_CHAINS
    uo = uo_ref[...]
    row = lax.broadcasted_iota(jnp.int32, (t, t), 0)
    col = lax.broadcasted_iota(jnp.int32, (t, t), 1)
    mask = col < row

    def visit(c, kb, diag):
        start = pl.multiple_of(jnp.maximum(kb, 0) * t, t)
        q = q_ref[0, 0, c * t:(c + 1) * t, :]
        k = k_ref[0, 0, pl.ds(start, t), :]
        v = v_ref[0, 0, pl.ds(start, t), :]
        z = _dot_nt(q, k)
        neg_log_rest = jnp.maximum(z, 0.0) + jnp.log(1.0 + jnp.exp(-jnp.abs(z)))
        log_beta = z - neg_log_rest
        if diag:
            neg_log_rest = jnp.where(mask, neg_log_rest, 0.0)
        sr = _dot(neg_log_rest.astype(BF16), uo)
        if diag:
            w = jnp.where(mask, jnp.exp(log_beta - sr[:, :t]), 0.0)
            carry = -sr[:, t:]
            acc_ref[c] = _dot(w.astype(BF16), v)
        else:
            prev = carry_ref[c]
            w = jnp.exp(log_beta - sr[:, :t] + jnp.concatenate([prev] * (t // LANES), axis=1))
            w = jnp.where(kb >= 0, w, 0.0)
            carry = prev - sr[:, t:]
            acc_ref[c] += _dot(w.astype(BF16), v)
        carry_ref[c] = carry
        return jnp.max(carry)

    def unfinished(j, maxima):
        need = [jnp.logical_and(first + c - j >= 0, maxima[c] > SB_STOP) for c in range(SB_CHAINS)]
        return functools.reduce(jnp.logical_or, need).astype(jnp.int32)

    maxima = [visit(c, first + c, True) for c in range(SB_CHAINS)]

    def body(state):
        j, _ = state
        maxima = [visit(c, first + c - j, False) for c in range(SB_CHAINS)]
        return j + 1, unfinished(j + 1, maxima)

    lax.while_loop(lambda state: state[1] > 0, body, (jnp.int32(1), unfinished(1, maxima)))
    for c in range(SB_CHAINS):
        o_ref[0, c * t:(c + 1) * t, :] = acc_ref[c].astype(o_ref.dtype)


def _sb_attention(qn, kn, vn):
    bsz, _, seq, _ = qn.shape
    t = SB_TILE
    tq = t * SB_CHAINS
    assert seq % tq == 0
    r = jnp.arange(t)
    upper = (r[:, None] > r[None, :])
    uo = jnp.concatenate([upper, jnp.ones((t, LANES), bool)], axis=1).astype(BF16)
    kv_spec = pl.BlockSpec((1, 1, seq, SB_HEAD_DIM), lambda b, h, i: (b, h, 0, 0))
    return pl.pallas_call(
        _sb_body,
        grid=(bsz, SB_HEADS, seq // tq),
        in_specs=[pl.BlockSpec((1, 1, tq, SB_HEAD_DIM), lambda b, h, i: (b, h, i, 0)),
                  kv_spec, kv_spec, _const_spec((t, t + LANES))],
        out_specs=pl.BlockSpec((1, tq, SB_HEAD_DIM), lambda b, h, i: (b, i, h)),
        out_shape=jax.ShapeDtypeStruct((bsz, seq, SB_WIDTH), BF16),
        scratch_shapes=[pltpu.VMEM((SB_CHAINS, t, LANES), F32), pltpu.VMEM((SB_CHAINS, t, SB_HEAD_DIM), F32)],
        compiler_params=_params("parallel", "parallel", "arbitrary"),
        name="sb_attention",
    )(qn, kn, vn, uo)


def _ssd_body(xbcs_ref, zs_ref, dts_ref, cw_ref, cb_ref, dtb_ref, alog_ref, dskip_ref, ng_ref, e_ref, tri_ref,
              shift_ref, os_ref, bufs_ref, sts_ref, ys_ref):
    tail = SSD_CONV_TAIL

    @pl.when(pl.program_id(0) == 0)
    def _():
        bufs_ref[:, 0:tail, :] = jnp.zeros((bufs_ref.shape[0], tail, SSD_CONV_DIM), bufs_ref.dtype)
        sts_ref[...] = jnp.zeros_like(sts_ref)

    for b in range(xbcs_ref.shape[0]):
        _ssd_chunk(xbcs_ref.at[b], zs_ref.at[b], dts_ref.at[b], cw_ref, cb_ref, dtb_ref, alog_ref, dskip_ref, ng_ref,
                   e_ref, tri_ref, shift_ref, os_ref.at[b], bufs_ref.at[b], sts_ref.at[b], ys_ref.at[b])


def _ssd_chunk(xbc_ref, z_ref, dt_ref, cw_ref, cb_ref, dtb_ref, alog_ref, dskip_ref, ng_ref, e_ref, tri_ref,
               shift_ref, o_ref, buf_ref, st_ref, y_ref):
    L = SSD_CHUNK
    G, R, P, N = SSD_GROUPS, SSD_HEADS_PER_GROUP, SSD_HEAD_DIM, SSD_STATE
    tail = SSD_CONV_TAIL
    buf_ref[tail:tail + L, :] = xbc_ref[...]

    def conv_silu_piece(lo, hi):
        delayed = _dot(shift_ref[...], buf_ref[:, lo:hi])
        acc = cb_ref[:, lo:hi] + cw_ref[SSD_CONV - 1:SSD_CONV, lo:hi] * xbc_ref[:, lo:hi].astype(F32)
        for k in range(SSD_CONV - 1):
            acc = acc + cw_ref[k:k + 1, lo:hi] * delayed[k * L:(k + 1) * L, :]
        return _silu(acc)

    def conv_silu(lo, hi):
        step = SSD_CONV_PIECE
        return jnp.concatenate([conv_silu_piece(c, c + step) for c in range(lo, hi, step)], axis=1)

    xs = conv_silu(0, SSD_INNER)
    b_in = conv_silu(SSD_INNER, SSD_INNER + G * N)
    c_in = conv_silu(SSD_INNER + G * N, SSD_CONV_DIM)
    buf_ref[0:tail, :] = buf_ref[L:L + tail, :]

    e01 = e_ref[...]
    dt = _softplus(dt_ref[...] + dtb_ref[...])
    da = dt * (-jnp.exp(alog_ref[...]))
    a_cs = _dot_01_by_f32(tri_ref[...], da)
    a_cs_t = a_cs.T
    a_full = _dot_f32_by_01(a_cs, e01)
    dt_full = _dot_f32_by_01(dt, e01)
    a_last = a_full[L - 1:L, :]
    ea_full = jnp.exp(a_full)
    x_dt = xs * dt_full
    xw = (x_dt * jnp.exp(a_last - a_full)).astype(BF16)
    ea_last = jnp.exp(a_last)

    row = lax.broadcasted_iota(jnp.int32, (L, L), 0)
    col = lax.broadcasted_iota(jnp.int32, (L, L), 1)
    causal = col <= row
    lane = lax.broadcasted_iota(jnp.int32, (L, LANES), 1)
    first_head = lane < P

    for g in range(G):
        gs = slice(g * N, (g + 1) * N)
        cg = c_in[:, gs].astype(BF16)
        bg_f32 = b_in[:, gs]
        cb = _dot_nt(cg, bg_f32.astype(BF16))
        for pair in range(R // 2):
            ms = []
            for r in range(2):
                h = g * R + pair * 2 + r
                seg = a_cs[:, h:h + 1] - a_cs_t[h:h + 1, :]
                decay = jnp.exp(jnp.where(causal, seg, -jnp.inf))
                ms.append((cb * decay).astype(BF16))
            c0 = (g * R + pair * 2) * P
            xp = x_dt[:, c0:c0 + LANES]
            rhs = jnp.concatenate([jnp.where(first_head, xp, 0.0).astype(BF16),
                                   jnp.where(first_head, 0.0, xp).astype(BF16)], axis=0)
            y_ref[:, c0:c0 + LANES] = _dot(jnp.concatenate(ms, axis=1), rhs)
        cs = slice(g * R * P, (g + 1) * R * P)
        state = st_ref[g]
        y_ref[:, cs] += _dot(cg, state.astype(BF16)) * ea_full[:, cs]
        st_ref[g] = state * ea_last[:, cs] + _dot(bg_f32.T.astype(BF16), xw[:, cs])

    y = y_ref[...] + xs * dskip_ref[...]
    y = y * _silu(z_ref[...].astype(F32))
    gw = SSD_INNER // G
    for g in range(G):
        cs = slice(g * gw, (g + 1) * gw)
        o_ref[:, cs] = (_rms(y[:, cs]) * ng_ref[:, cs]).astype(o_ref.dtype)


def _ssd(proj, dt_raw, conv_w, conv_b, dt_bias, a_log, d_skip, norm_g, bsz, seq):
    L = SSD_CHUNK
    nc = seq // L
    t = bsz * seq
    pad = LANES - SSD_HEADS
    heads = jnp.arange(LANES)
    cols = jnp.arange(SSD_INNER) // SSD_HEAD_DIM
    e01 = (heads[:, None] == cols[None, :]).astype(BF16)
    r = jnp.arange(L)
    tri = (r[None, :] <= r[:, None]).astype(BF16)
    assert proj.dtype == BF16, "the 0/1 shift product is exact only on bf16 data"
    delay = (SSD_CONV - 1) - jnp.arange((SSD_CONV - 1) * L) // L
    src = SSD_CONV_TAIL + jnp.arange((SSD_CONV - 1) * L) % L - delay
    shift = (src[:, None] == jnp.arange(SSD_CONV_TAIL + L)[None, :]).astype(BF16)
    rowvec = lambda v: v.reshape(1, -1).astype(F32)
    proj3 = proj.reshape(bsz, seq, proj.shape[-1])
    return pl.pallas_call(
        _ssd_body,
        grid=(nc,),
        in_specs=[
            pl.BlockSpec((bsz, L, SSD_CONV_DIM), lambda c: (0, c, P_XBC // SSD_CONV_DIM)),
            pl.BlockSpec((bsz, L, SSD_INNER), lambda c: (0, c, P_Z // SSD_INNER)),
            pl.BlockSpec((bsz, L, LANES), lambda c: (0, c, 0)),
            _const_spec((SSD_CONV, SSD_CONV_DIM)),
            _const_spec((1, SSD_CONV_DIM)),
            _const_spec((1, LANES)),
            _const_spec((1, LANES)),
            _const_spec((1, SSD_INNER)),
            _const_spec((1, SSD_INNER)),
            _const_spec((LANES, SSD_INNER)),
            _const_spec((L, L)),
            _const_spec(((SSD_CONV - 1) * L, SSD_CONV_TAIL + L)),
        ],
        out_specs=pl.BlockSpec((bsz, L, SSD_INNER), lambda c: (0, c, 0)),
        out_shape=jax.ShapeDtypeStruct((bsz, seq, SSD_INNER), BF16),
        scratch_shapes=[
            pltpu.VMEM((bsz, SSD_CONV_TAIL + L, SSD_CONV_DIM), BF16),
            pltpu.VMEM((bsz, SSD_GROUPS, SSD_STATE, SSD_HEADS_PER_GROUP * SSD_HEAD_DIM), F32),
            pltpu.VMEM((bsz, L, SSD_INNER), F32),
        ],
        compiler_params=_params("arbitrary"),
        name="ssd",
    )(proj3, proj3, dt_raw.reshape(bsz, seq, LANES), conv_w.astype(F32), rowvec(conv_b),
      jnp.pad(rowvec(dt_bias), ((0, 0), (0, pad))), jnp.pad(rowvec(a_log), ((0, 0), (0, pad))),
      rowvec(jnp.repeat(d_skip, SSD_HEAD_DIM)), rowvec(norm_g), e01, tri, shift).reshape(t, SSD_INNER)


def _merge_body(x_ref, yssd_ref, ysb_ref, u_ref, halo_ref, g0_ref, g1_ref, g2_ref,
                wssd_ref, wsb_ref, wpool_ref, wout_ref, pw_ref, ps_ref, o_ref, ext_ref, *, seq):
    tm = x_ref.shape[0]
    H = POOL_HALO
    start = (pl.program_id(0) * tm) % seq
    halo = halo_ref[...].astype(F32)
    ext_ref[0:H, :] = jnp.where(start == 0, jnp.zeros_like(halo), halo)
    ext_ref[H:H + tm, :] = u_ref[...].astype(F32)
    pos = start + lax.broadcasted_iota(jnp.int32, (tm, 1), 0)

    merged = jax.nn.sigmoid(g0_ref[...].astype(F32)) * _dot(yssd_ref[...], wssd_ref[...])
    merged += jax.nn.sigmoid(g1_ref[...].astype(F32)) * _dot(ysb_ref[...], wsb_ref[...])

    mixed = []
    for gi, win in enumerate(POOL_WINDOWS):
        cs = slice(gi * POOL_GROUP_DIM, (gi + 1) * POOL_GROUP_DIM)
        cur = ext_ref[H:H + tm, cs]
        wsum = cur
        for k in range(1, win):
            wsum = wsum + ext_ref[H - k:H - k + tm, cs]
        count = jnp.minimum(pos + 1, win).astype(F32)
        pooled = wsum / count - cur
        mixed.append((_dot(pooled.astype(BF16), pw_ref[gi]) * ps_ref[:, cs]).astype(BF16))
    ypool = _dot(jnp.concatenate(mixed, axis=1), wpool_ref[...])
    merged += jax.nn.sigmoid(g2_ref[...].astype(F32)) * ypool
    o_ref[...] = x_ref[...] + _dot(merged.astype(BF16), wout_ref[...])


def _merge(x2, y_ssd, y_sb, proj, pool_w, pool_scale, w_br_ssd, w_br_sb, w_br_pool, w_out, seq):
    t = x2.shape[0]
    tm = min(512, seq)
    hb = tm // POOL_HALO
    gate = lambda k: pl.BlockSpec((tm, D_MODEL), lambda i: (i, P_GATES // D_MODEL + k))
    return pl.pallas_call(
        functools.partial(_merge_body, seq=seq),
        grid=(t // tm,),
        in_specs=[
            pl.BlockSpec((tm, D_MODEL), lambda i: (i, 0)),
            pl.BlockSpec((tm, SSD_INNER), lambda i: (i, 0)),
            pl.BlockSpec((tm, SB_WIDTH), lambda i: (i, 0)),
            pl.BlockSpec((tm, POOL_WIDTH), lambda i: (i, P_U // POOL_WIDTH)),
            pl.BlockSpec((POOL_HALO, POOL_WIDTH), lambda i: (jnp.maximum(i * hb - 1, 0), P_U // POOL_WIDTH)),
            gate(0), gate(1), gate(2),
            _const_spec((SSD_INNER, D_MODEL)),
            _const_spec((SB_WIDTH, D_MODEL)),
            _const_spec((POOL_WIDTH, D_MODEL)),
            _const_spec((D_MODEL, D_MODEL)),
            _const_spec((POOL_GROUPS, POOL_GROUP_DIM, POOL_GROUP_DIM)),
            _const_spec((1, POOL_WIDTH)),
        ],
        out_specs=pl.BlockSpec((tm, D_MODEL), lambda i: (i, 0)),
        out_shape=jax.ShapeDtypeStruct((t, D_MODEL), F32),
        scratch_shapes=[pltpu.VMEM((tm + POOL_HALO, POOL_WIDTH), F32)],
        compiler_params=_params("parallel"),
        name="merge",
    )(x2, y_ssd, y_sb, proj, proj, proj, proj, proj,
      w_br_ssd.astype(BF16), w_br_sb.astype(BF16), w_br_pool.astype(BF16), w_out.astype(BF16),
      pool_w.astype(BF16), pool_scale.reshape(1, -1).astype(F32))


def _swiglu_into(acc_ref, xb, wg_ref, wu_ref, wd_ref, width):
    for c in range(width // FFN_CHUNK):
        cs = slice(c * FFN_CHUNK, (c + 1) * FFN_CHUNK)
        hidden = _silu(_dot(xb, wg_ref[:, cs])) * _dot(xb, wu_ref[:, cs])
        acc_ref[...] += _dot(hidden.astype(BF16), wd_ref[cs, :])


def _ffn_body(x_ref, g_ref, wg_ref, wu_ref, wd_ref, o_ref):
    x = x_ref[...]
    o_ref[...] = x
    _swiglu_into(o_ref, (_rms(x) * g_ref[...]).astype(BF16), wg_ref, wu_ref, wd_ref, FFN_DENSE)


def _ffn_dense(x2, g, w_gate, w_up, w_down):
    t = x2.shape[0]
    tm = min(512, t)
    return pl.pallas_call(
        _ffn_body,
        grid=(t // tm,),
        in_specs=[
            pl.BlockSpec((tm, D_MODEL), lambda i: (i, 0)),
            _const_spec((1, D_MODEL)),
            _const_spec((D_MODEL, FFN_DENSE)),
            _const_spec((D_MODEL, FFN_DENSE)),
            _const_spec((FFN_DENSE, D_MODEL)),
        ],
        out_specs=pl.BlockSpec((tm, D_MODEL), lambda i: (i, 0)),
        out_shape=jax.ShapeDtypeStruct((t, D_MODEL), F32),
        compiler_params=_params("parallel"),
        name="ffn_dense",
    )(x2, g, w_gate.astype(BF16), w_up.astype(BF16), w_down.astype(BF16))


def _router_body(x_ref, g_ref, rw_ref, h_ref, idx_ref, wt_ref):
    h = _rms(x_ref[...]) * g_ref[...]
    h_ref[...] = h
    rw = rw_ref[...]
    h_hi = h.astype(BF16)
    h_lo = (h - h_hi.astype(F32)).astype(BF16)
    w_hi = rw.astype(BF16)
    w_lo = (rw - w_hi.astype(F32)).astype(BF16)
    logits = _dot(h_hi, w_hi) + (_dot(h_hi, w_lo) + _dot(h_lo, w_hi)) + _dot(h_lo, w_lo)
    lane = lax.broadcasted_iota(jnp.int32, logits.shape, 1)
    lane_f = lane.astype(F32)
    logits = jnp.where(lane < N_EXPERTS, logits, -jnp.inf)
    m1 = jnp.max(logits, axis=-1, keepdims=True)
    i1 = jnp.min(jnp.where(logits == m1, lane_f, float(LANES)), axis=-1, keepdims=True)
    rest = jnp.where(lane_f == i1, -jnp.inf, logits)
    m2 = jnp.max(rest, axis=-1, keepdims=True)
    i2 = jnp.min(jnp.where(rest == m2, lane_f, float(LANES)), axis=-1, keepdims=True)
    e = jnp.exp(m2 - m1)
    w1 = 1.0 / (1.0 + e)
    idx_ref[...] = jnp.where(lane == 0, i1, jnp.where(lane == 1, i2, 0.0)).astype(jnp.int32)
    wt_ref[...] = jnp.where(lane == 0, w1, jnp.where(lane == 1, e * w1, 0.0))


def _router(x2, g, router_w):
    t = x2.shape[0]
    tm = min(512, t)
    rw = jnp.pad(router_w.astype(F32), ((0, 0), (0, LANES - N_EXPERTS)))
    row = pl.BlockSpec((tm, D_MODEL), lambda i: (i, 0))
    small = pl.BlockSpec((tm, LANES), lambda i: (i, 0))
    return pl.pallas_call(
        _router_body,
        grid=(t // tm,),
        in_specs=[row, _const_spec((1, D_MODEL)), _const_spec((D_MODEL, LANES))],
        out_specs=[row, small, small],
        out_shape=[jax.ShapeDtypeStruct((t, D_MODEL), F32),
                   jax.ShapeDtypeStruct((t, LANES), jnp.int32),
                   jax.ShapeDtypeStruct((t, LANES), F32)],
        compiler_params=_params("parallel"),
        name="router",
    )(x2, g, rw)


def _dispatch_body(pends_ref, counts_ref, slot_ref, h_ref, xb_hbm, zero_ref, sem):
    tm = h_ref.shape[0]

    def zero_block(start):
        fill = pltpu.make_async_copy(zero_ref, xb_hbm.at[pl.ds(pl.multiple_of(start, MOE_BLOCK), MOE_BLOCK)], sem)
        fill.start()
        fill.wait()

    @pl.when(pl.program_id(0) == 0)
    def _():
        zero_ref[...] = jnp.zeros_like(zero_ref)
        for e in range(N_EXPERTS):
            pl.when(counts_ref[e] > 0)(functools.partial(zero_block, pends_ref[e] - MOE_BLOCK))
        for j in range(N_EXPERTS):
            start = pends_ref[N_EXPERTS - 1] + j * MOE_BLOCK
            pl.when(start < xb_hbm.shape[0])(functools.partial(zero_block, start))

    for k in range(TOP_K):
        for r in range(tm):
            pltpu.make_async_copy(h_ref.at[pl.ds(r, 1)], xb_hbm.at[pl.ds(slot_ref[0, 0, k * tm + r], 1)], sem).start()
    for k in range(TOP_K):
        pltpu.make_async_copy(h_ref, xb_hbm.at[pl.ds(0, tm)], sem).wait()


def _dispatch(h, slot_tiles, pends, counts, cap):
    t = h.shape[0]
    nt = slot_tiles.shape[0]
    tm = t // nt
    grid_spec = pltpu.PrefetchScalarGridSpec(
        num_scalar_prefetch=2,
        grid=(nt,),
        in_specs=[
            pl.BlockSpec((1, 1, TOP_K * tm), lambda i, pe, co: (i, 0, 0), memory_space=pltpu.SMEM),
            pl.BlockSpec((tm, D_MODEL), lambda i, pe, co: (i, 0)),
        ],
        out_specs=pl.BlockSpec(memory_space=pl.ANY),
        scratch_shapes=[pltpu.VMEM((MOE_BLOCK, D_MODEL), F32), pltpu.SemaphoreType.DMA],
    )
    return pl.pallas_call(
        _dispatch_body,
        grid_spec=grid_spec,
        out_shape=jax.ShapeDtypeStruct((cap, D_MODEL), F32),
        compiler_params=_params("arbitrary"),
        name="dispatch",
    )(pends.astype(jnp.int32), counts.astype(jnp.int32), slot_tiles, h)


def _experts_body(bexp_ref, nused_ref, xb_ref, wg_ref, wu_ref, wd_ref, o_ref):
    del bexp_ref
    o_ref[...] = jnp.zeros_like(o_ref)

    @pl.when(pl.program_id(0) < nused_ref[0])
    def _():
        _swiglu_into(o_ref, xb_ref[...].astype(BF16), wg_ref, wu_ref, wd_ref, FFN_EXPERT)


def _experts(xb, block_exp, n_used, e_gate, e_up, e_down):
    n_blocks = block_exp.shape[0]
    wspec = lambda shape: pl.BlockSpec((None,) + shape, lambda i, be, nu: (be[i], 0, 0))
    rows = pl.BlockSpec((MOE_BLOCK, D_MODEL), lambda i, be, nu: (i, 0))
    used_rows = pl.BlockSpec((MOE_BLOCK, D_MODEL), lambda i, be, nu: (jnp.minimum(i, nu[0] - 1), 0))
    grid_spec = pltpu.PrefetchScalarGridSpec(
        num_scalar_prefetch=2,
        grid=(n_blocks,),
        in_specs=[used_rows, wspec((D_MODEL, FFN_EXPERT)), wspec((D_MODEL, FFN_EXPERT)),
                  wspec((FFN_EXPERT, D_MODEL))],
        out_specs=rows,
    )
    return pl.pallas_call(
        _experts_body,
        grid_spec=grid_spec,
        out_shape=jax.ShapeDtypeStruct((n_blocks * MOE_BLOCK, D_MODEL), F32),
        compiler_params=_params("arbitrary"),
        name="experts",
    )(block_exp, n_used, xb, e_gate.astype(BF16), e_up.astype(BF16), e_down.astype(BF16))


def _combine_body(slot_ref, next_ref, x_ref, wt_ref, yb_hbm, o_ref, buf_ref, sem):
    tm = x_ref.shape[0]
    n = TOP_K * tm
    i = pl.program_id(0)

    def fetch(idx_ref, b):
        for r in range(n):
            pltpu.make_async_copy(yb_hbm.at[pl.ds(idx_ref[0, 0, r], 1)], buf_ref.at[b, pl.ds(r, 1)], sem.at[b]).start()

    @pl.when(i == 0)
    def _():
        fetch(slot_ref, 0)

    for cur in range(2):
        @pl.when(i % 2 == cur)
        def _():
            @pl.when(i + 1 < pl.num_programs(0))
            def _():
                fetch(next_ref, 1 - cur)

            pltpu.make_async_copy(yb_hbm.at[pl.ds(0, n)], buf_ref.at[cur], sem.at[cur]).wait()
            wt = wt_ref[...]
            o_ref[...] = x_ref[...] + (wt[:, 0:1] * buf_ref[cur, 0:tm, :] + wt[:, 1:2] * buf_ref[cur, tm:n, :])


def _combine(x2, wt, slot_tiles, yb):
    t = x2.shape[0]
    nt = slot_tiles.shape[0]
    tm = t // nt
    slots = lambda shift: pl.BlockSpec((1, 1, TOP_K * tm), lambda i: (jnp.minimum(i + shift, nt - 1), 0, 0),
                                       memory_space=pltpu.SMEM)
    return pl.pallas_call(
        _combine_body,
        grid=(nt,),
        in_specs=[
            slots(0), slots(1),
            pl.BlockSpec((tm, D_MODEL), lambda i: (i, 0)),
            pl.BlockSpec((tm, LANES), lambda i: (i, 0)),
            pl.BlockSpec(memory_space=pl.ANY),
        ],
        out_specs=pl.BlockSpec((tm, D_MODEL), lambda i: (i, 0)),
        out_shape=jax.ShapeDtypeStruct((t, D_MODEL), F32),
        scratch_shapes=[pltpu.VMEM((2, TOP_K * tm, D_MODEL), F32), pltpu.SemaphoreType.DMA((2,))],
        compiler_params=_params("arbitrary"),
        name="combine",
    )(slot_tiles, slot_tiles, x2, wt, yb)


def _moe(x2, g, router_w, e_gate, e_up, e_down):
    t = x2.shape[0]
    h, idx, wt = _router(x2, g, router_w)
    n_assign = t * TOP_K
    cap = -(-n_assign // MOE_BLOCK) * MOE_BLOCK + N_EXPERTS * MOE_BLOCK
    n_blocks = cap // MOE_BLOCK
    expert_flat = idx[:, :TOP_K].reshape(-1)
    onehot = (expert_flat[:, None] == jnp.arange(N_EXPERTS, dtype=jnp.int32)[None, :]).astype(jnp.int32)
    running = jnp.cumsum(onehot, axis=0)
    counts = running[-1]
    rank = jnp.sum(running * onehot, axis=1) - 1
    padded = (counts + MOE_BLOCK - 1) // MOE_BLOCK * MOE_BLOCK
    pends = jnp.cumsum(padded)
    pstarts = pends - padded
    dest = (pstarts[expert_flat] + rank).astype(jnp.int32)
    block_exp = jnp.minimum(jnp.searchsorted(pends, jnp.arange(n_blocks) * MOE_BLOCK, side="right"),
                            N_EXPERTS - 1).astype(jnp.int32)
    n_used = (pends[-1:] // MOE_BLOCK).astype(jnp.int32)
    def slot_tiles(tile):
        tm = min(tile, t)
        return dest.reshape(t // tm, tm, TOP_K).transpose(0, 2, 1).reshape(t // tm, 1, TOP_K * tm)

    xb = _dispatch(h, slot_tiles(DISPATCH_TILE), pends, counts, cap)
    yb = _experts(xb, block_exp, n_used, e_gate, e_up, e_down)
    return _combine(x2, wt, slot_tiles(COMBINE_TILE), yb)


def _permute_w_in(w):
    main = jnp.concatenate([w[:, COL_Z:COL_XBC], w[:, :COL_Z], w[:, COL_POOL:], w[:, COL_V:COL_POOL]],
                           axis=1).astype(BF16)
    qkv = w[:, COL_DT:COL_V].astype(BF16)
    dt = jnp.pad(w[:, COL_XBC:COL_DT], ((0, 0), (0, LANES - SSD_HEADS))).astype(BF16)
    return main, qkv, dt


def kernel(x, mix_norm_g, w_in, conv_w, conv_b, dt_bias, a_log, d_skip, ssd_norm_g, q_norm_g, k_norm_g, pool_w,
           pool_scale, w_br_ssd, w_br_sb, w_br_pool, w_out, ffn_norm_g, ffn_w_gate, ffn_w_up, ffn_w_down,
           router_w, moe_w_gate, moe_w_up, moe_w_down):
    bsz, seq, d = x.shape
    depth = w_in.shape[0]
    x2 = x.reshape(bsz * seq, d)
    rowvec = lambda v: v.reshape(1, -1).astype(F32)
    for layer in range(depth):
        w_main, w_qkv, w_dt = _permute_w_in(w_in[layer])
        proj, dt_raw, qn, kn, vn = _inproj(x2, rowvec(mix_norm_g[layer]), w_main, w_qkv, w_dt,
                                           rowvec(q_norm_g[layer]), rowvec(k_norm_g[layer]), bsz, seq)
        y_sb = _sb_attention(qn, kn, vn).reshape(bsz * seq, SB_WIDTH)
        y_ssd = _ssd(proj, dt_raw, conv_w[layer], conv_b[layer], dt_bias[layer], a_log[layer], d_skip[layer],
                     ssd_norm_g[layer], bsz, seq)
        x2 = _merge(x2, y_ssd, y_sb, proj, pool_w[layer], pool_scale[layer], w_br_ssd[layer], w_br_sb[layer],
                    w_br_pool[layer], w_out[layer], seq)
        i = layer // 2
        if layer % 2 == 0:
            x2 = _ffn_dense(x2, rowvec(ffn_norm_g[layer]), ffn_w_gate[i], ffn_w_up[i], ffn_w_down[i])
        else:
            x2 = _moe(x2, rowvec(ffn_norm_g[layer]), router_w[i], moe_w_gate[i], moe_w_up[i], moe_w_down[i])
    return x2.reshape(bsz, seq, d)
```

```python
import functools
import math

import jax
import jax.numpy as jnp
from jax import lax
from jax.experimental import pallas as pl
from jax.experimental.pallas import tpu as pltpu

F32 = jnp.float32
BF16 = jnp.bfloat16

D_MODEL = 1024
EPS = 1e-6

SSD_INNER = 1024
SSD_HEAD_DIM = 64
SSD_HEADS = 16
SSD_GROUPS = 4
SSD_HEADS_PER_GROUP = 4
SSD_STATE = 128
SSD_CONV = 4
SSD_CHUNK = 128
SSD_CONV_DIM = SSD_INNER + 2 * SSD_GROUPS * SSD_STATE
SSD_CONV_TAIL = 16
SSD_CONV_PIECE = 512

SB_HEADS = 4
SB_HEAD_DIM = 128
SB_WIDTH = SB_HEADS * SB_HEAD_DIM
SB_TILE = 256
SB_CHAINS = 16

POOL_WINDOWS = (2, 4, 8, 16)
POOL_GROUPS = 4
POOL_WIDTH = 512
POOL_GROUP_DIM = 128
POOL_HALO = 16

N_BRANCHES = 3
FFN_DENSE = 2816
N_EXPERTS = 8
TOP_K = 2
FFN_EXPERT = 1792
MOE_BLOCK = 512
DISPATCH_TILE = 1024
COMBINE_TILE = 256

COL_Z = SSD_INNER
COL_XBC = COL_Z + SSD_CONV_DIM
COL_DT = COL_XBC + SSD_HEADS
COL_Q = COL_DT + SB_WIDTH
COL_K = COL_Q + SB_WIDTH
COL_V = COL_K + SB_WIDTH
COL_POOL = COL_V + POOL_WIDTH

P_XBC = 0
P_Z = 2048
P_GATES = 3072
P_U = 6144
P_WIDTH = 6656

PROJ_DTYPE = BF16
LANES = 128
VMEM_LIMIT = 56 * 1024 * 1024
FFN_CHUNK = 256
INPROJ_CHUNK = 1024
MERGE_CHUNK = 256
SB_STOP = -110.0


def _params(*sem):
    return pltpu.CompilerParams(dimension_semantics=sem, vmem_limit_bytes=VMEM_LIMIT)


def _const_spec(shape):
    nd = len(shape)
    return pl.BlockSpec(shape, lambda *_: (0,) * nd, pipeline_mode=pl.Buffered(1))


def _split3(x):
    a = x.astype(BF16)
    r = x - a.astype(F32)
    b = r.astype(BF16)
    c = (r - b.astype(F32)).astype(BF16)
    return a, b, c


def _dot(a, b):
    return jnp.dot(a, b, preferred_element_type=F32)


def _dot_nt(a, b):
    return lax.dot_general(a, b, (((1,), (1,)), ((), ())), preferred_element_type=F32)


def _dot_f32_by_01(x, e01):
    a, b, c = _split3(x)
    return _dot(a, e01) + _dot(b, e01) + _dot(c, e01)


def _dot_01_by_f32(t01, x):
    a, b, c = _split3(x)
    return _dot(t01, a) + _dot(t01, b) + _dot(t01, c)


def _rms(x):
    return x * lax.rsqrt(jnp.mean(x * x, axis=-1, keepdims=True) + EPS)


def _silu(x):
    return x * jax.nn.sigmoid(x)


def _softplus(x):
    return jnp.maximum(x, 0.0) + jnp.log1p(jnp.exp(-jnp.abs(x)))


def _inproj_body(x_ref, g_ref, w_ref, wqkv_ref, wdt_ref, qg_ref, kg_ref, o_ref, dt_ref, qo_ref, ko_ref, vo_ref):
    xb = (_rms(x_ref[...]) * g_ref[...]).astype(BF16)
    dt_ref[...] = _dot(xb, wdt_ref[...])
    for lo in range(0, P_WIDTH, INPROJ_CHUNK):
        cs = slice(lo, min(lo + INPROJ_CHUNK, P_WIDTH))
        o_ref[:, cs] = _dot(xb, w_ref[:, cs]).astype(o_ref.dtype)
    scale = 1.0 / math.sqrt(SB_HEAD_DIM)
    q = _dot(xb, wqkv_ref[:, 0:SB_WIDTH])
    k = _dot(xb, wqkv_ref[:, SB_WIDTH:2 * SB_WIDTH])
    v = _dot(xb, wqkv_ref[:, 2 * SB_WIDTH:3 * SB_WIDTH])
    for h in range(SB_HEADS):
        sl = slice(h * SB_HEAD_DIM, (h + 1) * SB_HEAD_DIM)
        qo_ref[0, h] = ((_rms(q[:, sl]) * qg_ref[...]) * scale).astype(BF16)
        ko_ref[0, h] = (_rms(k[:, sl]) * kg_ref[...]).astype(BF16)
        vo_ref[0, h] = v[:, sl].astype(BF16)


def _inproj(x2, g, w_main, w_qkv, w_dt, qg, kg, bsz, seq):
    t = x2.shape[0]
    tm = min(512, seq)
    ns = seq // tm
    head_spec = pl.BlockSpec((1, SB_HEADS, tm, SB_HEAD_DIM), lambda i: (i // ns, 0, i % ns, 0))
    head_shape = jax.ShapeDtypeStruct((bsz, SB_HEADS, seq, SB_HEAD_DIM), BF16)
    return pl.pallas_call(
        _inproj_body,
        grid=(t // tm,),
        in_specs=[
            pl.BlockSpec((tm, D_MODEL), lambda i: (i, 0)),
            _const_spec((1, D_MODEL)),
            _const_spec((D_MODEL, P_WIDTH)),
            _const_spec((D_MODEL, 3 * SB_WIDTH)),
            _const_spec((D_MODEL, LANES)),
            _const_spec((1, SB_HEAD_DIM)),
            _const_spec((1, SB_HEAD_DIM)),
        ],
        out_specs=[
            pl.BlockSpec((tm, P_WIDTH), lambda i: (i, 0)),
            pl.BlockSpec((tm, LANES), lambda i: (i, 0)),
            head_spec, head_spec, head_spec,
        ],
        out_shape=[
            jax.ShapeDtypeStruct((t, P_WIDTH), PROJ_DTYPE),
            jax.ShapeDtypeStruct((t, LANES), F32),
            head_shape, head_shape, head_shape,
        ],
        compiler_params=_params("parallel"),
        name="inproj",
    )(x2, g, w_main, w_qkv, w_dt, qg, kg)


def _sb_body(q_ref, k_ref, v_ref, uo_ref, o_ref, carry_ref, acc_ref):
    t = SB_TILE
    first = pl.program_id(2) * SB_CHAINS
    uo = uo_ref[...]
    row = lax.broadcasted_iota(jnp.int32, (t, t), 0)
    col = lax.broadcasted_iota(jnp.int32, (t, t), 1)
    mask = col < row

    def visit(c, kb, diag):
        start = pl.multiple_of(jnp.maximum(kb, 0) * t, t)
        q = q_ref[0, 0, c * t:(c + 1) * t, :]
        k = k_ref[0, 0, pl.ds(start, t), :]
        v = v_ref[0, 0, pl.ds(start, t), :]
        z = _dot_nt(q, k)
        neg_log_rest = jnp.maximum(z, 0.0) + jnp.log(1.0 + jnp.exp(-jnp.abs(z)))
        log_beta = z - neg_log_rest
        if diag:
            neg_log_rest = jnp.where(mask, neg_log_rest, 0.0)
        sr = _dot(neg_log_rest.astype(BF16), uo)
        if diag:
            w = jnp.where(mask, jnp.exp(log_beta - sr[:, :t]), 0.0)
            carry = -sr[:, t:]
            acc_ref[c] = _dot(w.astype(BF16), v)
        else:
            prev = carry_ref[c]
            w = jnp.exp(log_beta - sr[:, :t] + jnp.concatenate([prev] * (t // LANES), axis=1))
            w = jnp.where(kb >= 0, w, 0.0)
            carry = prev - sr[:, t:]
            acc_ref[c] += _dot(w.astype(BF16), v)
        carry_ref[c] = carry
        return jnp.max(carry)

    def unfinished(j, maxima):
        need = [jnp.logical_and(first + c - j >= 0, maxima[c] > SB_STOP) for c in range(SB_CHAINS)]
        return functools.reduce(jnp.logical_or, need).astype(jnp.int32)

    maxima = [visit(c, first + c, True) for c in range(SB_CHAINS)]

    def body(state):
        j, _ = state
        maxima = [visit(c, first + c - j, False) for c in range(SB_CHAINS)]
        return j + 1, unfinished(j + 1, maxima)

    lax.while_loop(lambda state: state[1] > 0, body, (jnp.int32(1), unfinished(1, maxima)))
    for c in range(SB_CHAINS):
        o_ref[0, c * t:(c + 1) * t, :] = acc_ref[c].astype(o_ref.dtype)


def _sb_attention(qn, kn, vn):
    bsz, _, seq, _ = qn.shape
    t = SB_TILE
    tq = t * SB_CHAINS
    assert seq % tq == 0
    r = jnp.arange(t)
    upper = (r[:, None] > r[None, :])
    uo = jnp.concatenate([upper, jnp.ones((t, LANES), bool)], axis=1).astype(BF16)
    kv_spec = pl.BlockSpec((1, 1, seq, SB_HEAD_DIM), lambda b, h, i: (b, h, 0, 0))
    return pl.pallas_call(
        _sb_body,
        grid=(bsz, SB_HEADS, seq // tq),
        in_specs=[pl.BlockSpec((1, 1, tq, SB_HEAD_DIM), lambda b, h, i: (b, h, i, 0)),
                  kv_spec, kv_spec, _const_spec((t, t + LANES))],
        out_specs=pl.BlockSpec((1, tq, SB_HEAD_DIM), lambda b, h, i: (b, i, h)),
        out_shape=jax.ShapeDtypeStruct((bsz, seq, SB_WIDTH), BF16),
        scratch_shapes=[pltpu.VMEM((SB_CHAINS, t, LANES), F32), pltpu.VMEM((SB_CHAINS, t, SB_HEAD_DIM), F32)],
        compiler_params=_params("parallel", "parallel", "arbitrary"),
        name="sb_attention",
    )(qn, kn, vn, uo)


def _ssd_body(xbcs_ref, zs_ref, dts_ref, cw_ref, cb_ref, dtb_ref, alog_ref, dskip_ref, ng_ref, e_ref, tri_ref,
              shift_ref, os_ref, bufs_ref, sts_ref, ys_ref):
    tail = SSD_CONV_TAIL

    @pl.when(pl.program_id(0) == 0)
    def _():
        bufs_ref[:, 0:tail, :] = jnp.zeros((bufs_ref.shape[0], tail, SSD_CONV_DIM), bufs_ref.dtype)
        sts_ref[...] = jnp.zeros_like(sts_ref)

    for b in range(xbcs_ref.shape[0]):
        _ssd_chunk(xbcs_ref.at[b], zs_ref.at[b], dts_ref.at[b], cw_ref, cb_ref, dtb_ref, alog_ref, dskip_ref, ng_ref,
                   e_ref, tri_ref, shift_ref, os_ref.at[b], bufs_ref.at[b], sts_ref.at[b], ys_ref.at[b])


def _ssd_chunk(xbc_ref, z_ref, dt_ref, cw_ref, cb_ref, dtb_ref, alog_ref, dskip_ref, ng_ref, e_ref, tri_ref,
               shift_ref, o_ref, buf_ref, st_ref, y_ref):
    L = SSD_CHUNK
    G, R, P, N = SSD_GROUPS, SSD_HEADS_PER_GROUP, SSD_HEAD_DIM, SSD_STATE
    tail = SSD_CONV_TAIL
    buf_ref[tail:tail + L, :] = xbc_ref[...]

    def conv_silu_piece(lo, hi):
        delayed = _dot(shift_ref[...], buf_ref[:, lo:hi])
        acc = cb_ref[:, lo:hi] + cw_ref[SSD_CONV - 1:SSD_CONV, lo:hi] * xbc_ref[:, lo:hi].astype(F32)
        for k in range(SSD_CONV - 1):
            acc = acc + cw_ref[k:k + 1, lo:hi] * delayed[k * L:(k + 1) * L, :]
        return _silu(acc)

    def conv_silu(lo, hi):
        step = SSD_CONV_PIECE
        return jnp.concatenate([conv_silu_piece(c, c + step) for c in range(lo, hi, step)], axis=1)

    xs = conv_silu(0, SSD_INNER)
    b_in = conv_silu(SSD_INNER, SSD_INNER + G * N)
    c_in = conv_silu(SSD_INNER + G * N, SSD_CONV_DIM)
    buf_ref[0:tail, :] = buf_ref[L:L + tail, :]

    e01 = e_ref[...]
    dt = _softplus(dt_ref[...] + dtb_ref[...])
    da = dt * (-jnp.exp(alog_ref[...]))
    a_cs = _dot_01_by_f32(tri_ref[...], da)
    a_cs_t = a_cs.T
    a_full = _dot_f32_by_01(a_cs, e01)
    dt_full = _dot_f32_by_01(dt, e01)
    a_last = a_full[L - 1:L, :]
    ea_full = jnp.exp(a_full)
    x_dt = xs * dt_full
    xw = (x_dt * jnp.exp(a_last - a_full)).astype(BF16)
    ea_last = jnp.exp(a_last)

    row = lax.broadcasted_iota(jnp.int32, (L, L), 0)
    col = lax.broadcasted_iota(jnp.int32, (L, L), 1)
    causal = col <= row
    lane = lax.broadcasted_iota(jnp.int32, (L, LANES), 1)
    first_head = lane < P

    for g in range(G):
        gs = slice(g * N, (g + 1) * N)
        cg = c_in[:, gs].astype(BF16)
        bg_f32 = b_in[:, gs]
        cb = _dot_nt(cg, bg_f32.astype(BF16))
        for pair in range(R // 2):
            ms = []
            for r in range(2):
                h = g * R + pair * 2 + r
                seg = a_cs[:, h:h + 1] - a_cs_t[h:h + 1, :]
                decay = jnp.exp(jnp.where(causal, seg, -jnp.inf))
                ms.append((cb * decay).astype(BF16))
            c0 = (g * R + pair * 2) * P
            xp = x_dt[:, c0:c0 + LANES]
            rhs = jnp.concatenate([jnp.where(first_head, xp, 0.0).astype(BF16),
                                   jnp.where(first_head, 0.0, xp).astype(BF16)], axis=0)
            y_ref[:, c0:c0 + LANES] = _dot(jnp.concatenate(ms, axis=1), rhs)
        cs = slice(g * R * P, (g + 1) * R * P)
        state = st_ref[g]
        y_ref[:, cs] += _dot(cg, state.astype(BF16)) * ea_full[:, cs]
        st_ref[g] = state * ea_last[:, cs] + _dot(bg_f32.T.astype(BF16), xw[:, cs])

    y = y_ref[...] + xs * dskip_ref[...]
    y = y * _silu(z_ref[...].astype(F32))
    gw = SSD_INNER // G
    for g in range(G):
        cs = slice(g * gw, (g + 1) * gw)
        o_ref[:, cs] = (_rms(y[:, cs]) * ng_ref[:, cs]).astype(o_ref.dtype)


def _ssd(proj, dt_raw, conv_w, conv_b, dt_bias, a_log, d_skip, norm_g, bsz, seq):
    L = SSD_CHUNK
    nc = seq // L
    t = bsz * seq
    pad = LANES - SSD_HEADS
    heads = jnp.arange(LANES)
    cols = jnp.arange(SSD_INNER) // SSD_HEAD_DIM
    e01 = (heads[:, None] == cols[None, :]).astype(BF16)
    r = jnp.arange(L)
    tri = (r[None, :] <= r[:, None]).astype(BF16)
    assert proj.dtype == BF16, "the 0/1 shift product is exact only on bf16 data"
    delay = (SSD_CONV - 1) - jnp.arange((SSD_CONV - 1) * L) // L
    src = SSD_CONV_TAIL + jnp.arange((SSD_CONV - 1) * L) % L - delay
    shift = (src[:, None] == jnp.arange(SSD_CONV_TAIL + L)[None, :]).astype(BF16)
    rowvec = lambda v: v.reshape(1, -1).astype(F32)
    proj3 = proj.reshape(bsz, seq, proj.shape[-1])
    return pl.pallas_call(
        _ssd_body,
        grid=(nc,),
        in_specs=[
            pl.BlockSpec((bsz, L, SSD_CONV_DIM), lambda c: (0, c, P_XBC // SSD_CONV_DIM)),
            pl.BlockSpec((bsz, L, SSD_INNER), lambda c: (0, c, P_Z // SSD_INNER)),
            pl.BlockSpec((bsz, L, LANES), lambda c: (0, c, 0)),
            _const_spec((SSD_CONV, SSD_CONV_DIM)),
            _const_spec((1, SSD_CONV_DIM)),
            _const_spec((1, LANES)),
            _const_spec((1, LANES)),
            _const_spec((1, SSD_INNER)),
            _const_spec((1, SSD_INNER)),
            _const_spec((LANES, SSD_INNER)),
            _const_spec((L, L)),
            _const_spec(((SSD_CONV - 1) * L, SSD_CONV_TAIL + L)),
        ],
        out_specs=pl.BlockSpec((bsz, L, SSD_INNER), lambda c: (0, c, 0)),
        out_shape=jax.ShapeDtypeStruct((bsz, seq, SSD_INNER), BF16),
        scratch_shapes=[
            pltpu.VMEM((bsz, SSD_CONV_TAIL + L, SSD_CONV_DIM), BF16),
            pltpu.VMEM((bsz, SSD_GROUPS, SSD_STATE, SSD_HEADS_PER_GROUP * SSD_HEAD_DIM), F32),
            pltpu.VMEM((bsz, L, SSD_INNER), F32),
        ],
        compiler_params=_params("arbitrary"),
        name="ssd",
    )(proj3, proj3, dt_raw.reshape(bsz, seq, LANES), conv_w.astype(F32), rowvec(conv_b),
      jnp.pad(rowvec(dt_bias), ((0, 0), (0, pad))), jnp.pad(rowvec(a_log), ((0, 0), (0, pad))),
      rowvec(jnp.repeat(d_skip, SSD_HEAD_DIM)), rowvec(norm_g), e01, tri, shift).reshape(t, SSD_INNER)


def _merge_body(x_ref, yssd_ref, ysb_ref, u_ref, halo_ref, g0_ref, g1_ref, g2_ref,
                wssd_ref, wsb_ref, wpool_ref, wout_ref, pw_ref, ps_ref, o_ref, ext_ref, *, seq):
    tm = x_ref.shape[0]
    H = POOL_HALO
    start = (pl.program_id(0) * tm) % seq
    halo = halo_ref[...].astype(F32)
    ext_ref[0:H, :] = jnp.where(start == 0, jnp.zeros_like(halo), halo)
    ext_ref[H:H + tm, :] = u_ref[...].astype(F32)
    pos = start + lax.broadcasted_iota(jnp.int32, (tm, 1), 0)

    mixed = []
    for gi, win in enumerate(POOL_WINDOWS):
        cs = slice(gi * POOL_GROUP_DIM, (gi + 1) * POOL_GROUP_DIM)
        cur = ext_ref[H:H + tm, cs]
        wsum = cur
        for k in range(1, win):
            wsum = wsum + ext_ref[H - k:H - k + tm, cs]
        count = jnp.minimum(pos + 1, win).astype(F32)
        pooled = wsum / count - cur
        mixed.append((_dot(pooled.astype(BF16), pw_ref[gi]) * ps_ref[:, cs]).astype(BF16))
    ypool = jnp.concatenate(mixed, axis=1)
    yssd = yssd_ref[...]
    ysb = ysb_ref[...]
    merged = []
    for lo in range(0, D_MODEL, MERGE_CHUNK):
        cs = slice(lo, lo + MERGE_CHUNK)
        part = jax.nn.sigmoid(g0_ref[:, cs].astype(F32)) * _dot(yssd, wssd_ref[:, cs])
        part += jax.nn.sigmoid(g1_ref[:, cs].astype(F32)) * _dot(ysb, wsb_ref[:, cs])
        part += jax.nn.sigmoid(g2_ref[:, cs].astype(F32)) * _dot(ypool, wpool_ref[:, cs])
        merged.append(part.astype(BF16))
    o_ref[...] = x_ref[...] + _dot(jnp.concatenate(merged, axis=1), wout_ref[...])


def _merge(x2, y_ssd, y_sb, proj, pool_w, pool_scale, w_br_ssd, w_br_sb, w_br_pool, w_out, seq):
    t = x2.shape[0]
    tm = min(512, seq)
    hb = tm // POOL_HALO
    gate = lambda k: pl.BlockSpec((tm, D_MODEL), lambda i: (i, P_GATES // D_MODEL + k))
    return pl.pallas_call(
        functools.partial(_merge_body, seq=seq),
        grid=(t // tm,),
        in_specs=[
            pl.BlockSpec((tm, D_MODEL), lambda i: (i, 0)),
            pl.BlockSpec((tm, SSD_INNER), lambda i: (i, 0)),
            pl.BlockSpec((tm, SB_WIDTH), lambda i: (i, 0)),
            pl.BlockSpec((tm, POOL_WIDTH), lambda i: (i, P_U // POOL_WIDTH)),
            pl.BlockSpec((POOL_HALO, POOL_WIDTH), lambda i: (jnp.maximum(i * hb - 1, 0), P_U // POOL_WIDTH)),
            gate(0), gate(1), gate(2),
            _const_spec((SSD_INNER, D_MODEL)),
            _const_spec((SB_WIDTH, D_MODEL)),
            _const_spec((POOL_WIDTH, D_MODEL)),
            _const_spec((D_MODEL, D_MODEL)),
            _const_spec((POOL_GROUPS, POOL_GROUP_DIM, POOL_GROUP_DIM)),
            _const_spec((1, POOL_WIDTH)),
        ],
        out_specs=pl.BlockSpec((tm, D_MODEL), lambda i: (i, 0)),
        out_shape=jax.ShapeDtypeStruct((t, D_MODEL), F32),
        scratch_shapes=[pltpu.VMEM((tm + POOL_HALO, POOL_WIDTH), F32)],
        compiler_params=_params("parallel"),
        name="merge",
    )(x2, y_ssd, y_sb, proj, proj, proj, proj, proj,
      w_br_ssd.astype(BF16), w_br_sb.astype(BF16), w_br_pool.astype(BF16), w_out.astype(BF16),
      pool_w.astype(BF16), pool_scale.reshape(1, -1).astype(F32))


def _swiglu_into(acc_ref, xb, wg_ref, wu_ref, wd_ref, width):
    for c in range(width // FFN_CHUNK):
        cs = slice(c * FFN_CHUNK, (c + 1) * FFN_CHUNK)
        hidden = _silu(_dot(xb, wg_ref[:, cs])) * _dot(xb, wu_ref[:, cs])
        acc_ref[...] += _dot(hidden.astype(BF16), wd_ref[cs, :])


def _ffn_body(x_ref, g_ref, wg_ref, wu_ref, wd_ref, o_ref):
    x = x_ref[...]
    o_ref[...] = x
    _swiglu_into(o_ref, (_rms(x) * g_ref[...]).astype(BF16), wg_ref, wu_ref, wd_ref, FFN_DENSE)


def _ffn_dense(x2, g, w_gate, w_up, w_down):
    t = x2.shape[0]
    tm = min(512, t)
    return pl.pallas_call(
        _ffn_body,
        grid=(t // tm,),
        in_specs=[
            pl.BlockSpec((tm, D_MODEL), lambda i: (i, 0)),
            _const_spec((1, D_MODEL)),
            _const_spec((D_MODEL, FFN_DENSE)),
            _const_spec((D_MODEL, FFN_DENSE)),
            _const_spec((FFN_DENSE, D_MODEL)),
        ],
        out_specs=pl.BlockSpec((tm, D_MODEL), lambda i: (i, 0)),
        out_shape=jax.ShapeDtypeStruct((t, D_MODEL), F32),
        compiler_params=_params("parallel"),
        name="ffn_dense",
    )(x2, g, w_gate.astype(BF16), w_up.astype(BF16), w_down.astype(BF16))


def _router_body(x_ref, g_ref, rw_ref, h_ref, idx_ref, wt_ref):
    h = _rms(x_ref[...]) * g_ref[...]
    h_ref[...] = h
    rw = rw_ref[...]
    h_hi = h.astype(BF16)
    h_lo = (h - h_hi.astype(F32)).astype(BF16)
    w_hi = rw.astype(BF16)
    w_lo = (rw - w_hi.astype(F32)).astype(BF16)
    logits = _dot(h_hi, w_hi) + (_dot(h_hi, w_lo) + _dot(h_lo, w_hi))
    lane = lax.broadcasted_iota(jnp.int32, logits.shape, 1)
    lane_f = lane.astype(F32)
    logits = jnp.where(lane < N_EXPERTS, logits, -jnp.inf)
    m1 = jnp.max(logits, axis=-1, keepdims=True)
    i1 = jnp.min(jnp.where(logits == m1, lane_f, float(LANES)), axis=-1, keepdims=True)
    rest = jnp.where(lane_f == i1, -jnp.inf, logits)
    m2 = jnp.max(rest, axis=-1, keepdims=True)
    i2 = jnp.min(jnp.where(rest == m2, lane_f, float(LANES)), axis=-1, keepdims=True)
    e = jnp.exp(m2 - m1)
    w1 = 1.0 / (1.0 + e)
    idx_ref[...] = jnp.where(lane == 0, i1, jnp.where(lane == 1, i2, 0.0)).astype(jnp.int32)
    wt_ref[...] = jnp.where(lane == 0, w1, jnp.where(lane == 1, e * w1, 0.0))


def _router(x2, g, router_w):
    t = x2.shape[0]
    tm = min(512, t)
    rw = jnp.pad(router_w.astype(F32), ((0, 0), (0, LANES - N_EXPERTS)))
    row = pl.BlockSpec((tm, D_MODEL), lambda i: (i, 0))
    small = pl.BlockSpec((tm, LANES), lambda i: (i, 0))
    return pl.pallas_call(
        _router_body,
        grid=(t // tm,),
        in_specs=[row, _const_spec((1, D_MODEL)), _const_spec((D_MODEL, LANES))],
        out_specs=[row, small, small],
        out_shape=[jax.ShapeDtypeStruct((t, D_MODEL), F32),
                   jax.ShapeDtypeStruct((t, LANES), jnp.int32),
                   jax.ShapeDtypeStruct((t, LANES), F32)],
        compiler_params=_params("parallel"),
        name="router",
    )(x2, g, rw)


def _dispatch_body(pends_ref, counts_ref, slot_ref, h_ref, xb_hbm, zero_ref, sem):
    tm = h_ref.shape[0]

    def zero_block(start):
        fill = pltpu.make_async_copy(zero_ref, xb_hbm.at[pl.ds(pl.multiple_of(start, MOE_BLOCK), MOE_BLOCK)], sem)
        fill.start()
        fill.wait()

    @pl.when(pl.program_id(0) == 0)
    def _():
        zero_ref[...] = jnp.zeros_like(zero_ref)
        for e in range(N_EXPERTS):
            pl.when(counts_ref[e] > 0)(functools.partial(zero_block, pends_ref[e] - MOE_BLOCK))
        for j in range(N_EXPERTS):
            start = pends_ref[N_EXPERTS - 1] + j * MOE_BLOCK
            pl.when(start < xb_hbm.shape[0])(functools.partial(zero_block, start))

    for k in range(TOP_K):
        for r in range(tm):
            pltpu.make_async_copy(h_ref.at[pl.ds(r, 1)], xb_hbm.at[pl.ds(slot_ref[0, 0, k * tm + r], 1)], sem).start()
    for k in range(TOP_K):
        pltpu.make_async_copy(h_ref, xb_hbm.at[pl.ds(0, tm)], sem).wait()


def _dispatch(h, slot_tiles, pends, counts, cap):
    t = h.shape[0]
    nt = slot_tiles.shape[0]
    tm = t // nt
    grid_spec = pltpu.PrefetchScalarGridSpec(
        num_scalar_prefetch=2,
        grid=(nt,),
        in_specs=[
            pl.BlockSpec((1, 1, TOP_K * tm), lambda i, pe, co: (i, 0, 0), memory_space=pltpu.SMEM),
            pl.BlockSpec((tm, D_MODEL), lambda i, pe, co: (i, 0)),
        ],
        out_specs=pl.BlockSpec(memory_space=pl.ANY),
        scratch_shapes=[pltpu.VMEM((MOE_BLOCK, D_MODEL), F32), pltpu.SemaphoreType.DMA],
    )
    return pl.pallas_call(
        _dispatch_body,
        grid_spec=grid_spec,
        out_shape=jax.ShapeDtypeStruct((cap, D_MODEL), F32),
        compiler_params=_params("arbitrary"),
        name="dispatch",
    )(pends.astype(jnp.int32), counts.astype(jnp.int32), slot_tiles, h)


def _experts_body(bexp_ref, nused_ref, xb_ref, wg_ref, wu_ref, wd_ref, o_ref):
    del bexp_ref
    o_ref[...] = jnp.zeros_like(o_ref)

    @pl.when(pl.program_id(0) < nused_ref[0])
    def _():
        _swiglu_into(o_ref, xb_ref[...].astype(BF16), wg_ref, wu_ref, wd_ref, FFN_EXPERT)


def _experts(xb, block_exp, n_used, e_gate, e_up, e_down):
    n_blocks = block_exp.shape[0]
    wspec = lambda shape: pl.BlockSpec((None,) + shape, lambda i, be, nu: (be[i], 0, 0))
    rows = pl.BlockSpec((MOE_BLOCK, D_MODEL), lambda i, be, nu: (i, 0))
    used_rows = pl.BlockSpec((MOE_BLOCK, D_MODEL), lambda i, be, nu: (jnp.minimum(i, nu[0] - 1), 0))
    grid_spec = pltpu.PrefetchScalarGridSpec(
        num_scalar_prefetch=2,
        grid=(n_blocks,),
        in_specs=[used_rows, wspec((D_MODEL, FFN_EXPERT)), wspec((D_MODEL, FFN_EXPERT)),
                  wspec((FFN_EXPERT, D_MODEL))],
        out_specs=rows,
    )
    return pl.pallas_call(
        _experts_body,
        grid_spec=grid_spec,
        out_shape=jax.ShapeDtypeStruct((n_blocks * MOE_BLOCK, D_MODEL), F32),
        compiler_params=_params("arbitrary"),
        name="experts",
    )(block_exp, n_used, xb, e_gate.astype(BF16), e_up.astype(BF16), e_down.astype(BF16))


def _combine_body(slot_ref, next_ref, x_ref, wt_ref, yb_hbm, o_ref, buf_ref, sem):
    tm = x_ref.shape[0]
    n = TOP_K * tm
    i = pl.program_id(0)

    def fetch(idx_ref, b):
        for r in range(n):
            pltpu.make_async_copy(yb_hbm.at[pl.ds(idx_ref[0, 0, r], 1)], buf_ref.at[b, pl.ds(r, 1)], sem.at[b]).start()

    @pl.when(i == 0)
    def _():
        fetch(slot_ref, 0)

    for cur in range(2):
        @pl.when(i % 2 == cur)
        def _():
            @pl.when(i + 1 < pl.num_programs(0))
            def _():
                fetch(next_ref, 1 - cur)

            pltpu.make_async_copy(yb_hbm.at[pl.ds(0, n)], buf_ref.at[cur], sem.at[cur]).wait()
            wt = wt_ref[...]
            o_ref[...] = x_ref[...] + (wt[:, 0:1] * buf_ref[cur, 0:tm, :] + wt[:, 1:2] * buf_ref[cur, tm:n, :])


def _combine(x2, wt, slot_tiles, yb):
    t = x2.shape[0]
    nt = slot_tiles.shape[0]
    tm = t // nt
    slots = lambda shift: pl.BlockSpec((1, 1, TOP_K * tm), lambda i: (jnp.minimum(i + shift, nt - 1), 0, 0),
                                       memory_space=pltpu.SMEM)
    return pl.pallas_call(
        _combine_body,
        grid=(nt,),
        in_specs=[
            slots(0), slots(1),
            pl.BlockSpec((tm, D_MODEL), lambda i: (i, 0)),
            pl.BlockSpec((tm, LANES), lambda i: (i, 0)),
            pl.BlockSpec(memory_space=pl.ANY),
        ],
        out_specs=pl.BlockSpec((tm, D_MODEL), lambda i: (i, 0)),
        out_shape=jax.ShapeDtypeStruct((t, D_MODEL), F32),
        scratch_shapes=[pltpu.VMEM((2, TOP_K * tm, D_MODEL), F32), pltpu.SemaphoreType.DMA((2,))],
        compiler_params=_params("arbitrary"),
        name="combine",
    )(slot_tiles, slot_tiles, x2, wt, yb)


def _moe(x2, g, router_w, e_gate, e_up, e_down):
    t = x2.shape[0]
    h, idx, wt = _router(x2, g, router_w)
    n_assign = t * TOP_K
    cap = -(-n_assign // MOE_BLOCK) * MOE_BLOCK + N_EXPERTS * MOE_BLOCK
    n_blocks = cap // MOE_BLOCK
    expert_flat = idx[:, :TOP_K].reshape(-1)
    onehot = (expert_flat[:, None] == jnp.arange(N_EXPERTS, dtype=jnp.int32)[None, :]).astype(jnp.int32)
    running = jnp.cumsum(onehot, axis=0)
    counts = running[-1]
    rank = jnp.sum(running * onehot, axis=1) - 1
    padded = (counts + MOE_BLOCK - 1) // MOE_BLOCK * MOE_BLOCK
    pends = jnp.cumsum(padded)
    pstarts = pends - padded
    dest = (pstarts[expert_flat] + rank).astype(jnp.int32)
    block_exp = jnp.minimum(jnp.searchsorted(pends, jnp.arange(n_blocks) * MOE_BLOCK, side="right"),
                            N_EXPERTS - 1).astype(jnp.int32)
    n_used = (pends[-1:] // MOE_BLOCK).astype(jnp.int32)
    def slot_tiles(tile):
        tm = min(tile, t)
        return dest.reshape(t // tm, tm, TOP_K).transpose(0, 2, 1).reshape(t // tm, 1, TOP_K * tm)

    xb = _dispatch(h, slot_tiles(DISPATCH_TILE), pends, counts, cap)
    yb = _experts(xb, block_exp, n_used, e_gate, e_up, e_down)
    return _combine(x2, wt, slot_tiles(COMBINE_TILE), yb)


def _permute_w_in(w):
    main = jnp.concatenate([w[:, COL_Z:COL_XBC], w[:, :COL_Z], w[:, COL_POOL:], w[:, COL_V:COL_POOL]],
                           axis=1).astype(BF16)
    qkv = w[:, COL_DT:COL_V].astype(BF16)
    dt = jnp.pad(w[:, COL_XBC:COL_DT], ((0, 0), (0, LANES - SSD_HEADS))).astype(BF16)
    return main, qkv, dt


def kernel(x, mix_norm_g, w_in, conv_w, conv_b, dt_bias, a_log, d_skip, ssd_norm_g, q_norm_g, k_norm_g, pool_w,
           pool_scale, w_br_ssd, w_br_sb, w_br_pool, w_out, ffn_norm_g, ffn_w_gate, ffn_w_up, ffn_w_down,
           router_w, moe_w_gate, moe_w_up, moe_w_down):
    bsz, seq, d = x.shape
    depth = w_in.shape[0]
    x2 = x.reshape(bsz * seq, d)
    rowvec = lambda v: v.reshape(1, -1).astype(F32)
    for layer in range(depth):
        w_main, w_qkv, w_dt = _permute_w_in(w_in[layer])
        proj, dt_raw, qn, kn, vn = _inproj(x2, rowvec(mix_norm_g[layer]), w_main, w_qkv, w_dt,
                                           rowvec(q_norm_g[layer]), rowvec(k_norm_g[layer]), bsz, seq)
        y_sb = _sb_attention(qn, kn, vn).reshape(bsz * seq, SB_WIDTH)
        y_ssd = _ssd(proj, dt_raw, conv_w[layer], conv_b[layer], dt_bias[layer], a_log[layer], d_skip[layer],
                     ssd_norm_g[layer], bsz, seq)
        x2 = _merge(x2, y_ssd, y_sb, proj, pool_w[layer], pool_scale[layer], w_br_ssd[layer], w_br_sb[layer],
                    w_br_pool[layer], w_out[layer], seq)
        i = layer // 2
        if layer % 2 == 0:
            x2 = _ffn_dense(x2, rowvec(ffn_norm_g[layer]), ffn_w_gate[i], ffn_w_up[i], ffn_w_down[i])
        else:
            x2 = _moe(x2, rowvec(ffn_norm_g[layer]), router_w[i], moe_w_gate[i], moe_w_up[i], moe_w_down[i])
    return x2.reshape(bsz, seq, d)
```

```python
import functools
import math

import jax
import jax.numpy as jnp
from jax import lax
from jax.experimental import pallas as pl
from jax.experimental.pallas import tpu as pltpu

F32 = jnp.float32
BF16 = jnp.bfloat16

D_MODEL = 1024
EPS = 1e-6

SSD_INNER = 1024
SSD_HEAD_DIM = 64
SSD_HEADS = 16
SSD_GROUPS = 4
SSD_HEADS_PER_GROUP = 4
SSD_STATE = 128
SSD_CONV = 4
SSD_CHUNK = 128
SSD_CONV_DIM = SSD_INNER + 2 * SSD_GROUPS * SSD_STATE
SSD_CONV_TAIL = 16
SSD_CONV_PIECE = 512

SB_HEADS = 4
SB_HEAD_DIM = 128
SB_WIDTH = SB_HEADS * SB_HEAD_DIM
SB_TILE = 256
SB_CHAINS = 16

POOL_WINDOWS = (2, 4, 8, 16)
POOL_GROUPS = 4
POOL_WIDTH = 512
POOL_GROUP_DIM = 128
POOL_HALO = 16

N_BRANCHES = 3
FFN_DENSE = 2816
N_EXPERTS = 8
TOP_K = 2
FFN_EXPERT = 1792
MOE_BLOCK = 512
DISPATCH_TILE = 1024
COMBINE_TILE = 256

COL_Z = SSD_INNER
COL_XBC = COL_Z + SSD_CONV_DIM
COL_DT = COL_XBC + SSD_HEADS
COL_Q = COL_DT + SB_WIDTH
COL_K = COL_Q + SB_WIDTH
COL_V = COL_K + SB_WIDTH
COL_POOL = COL_V + POOL_WIDTH

P_XBC = 0
P_Z = 2048
P_GATES = 3072
P_U = 6144
P_WIDTH = 6656

PROJ_DTYPE = BF16
LANES = 128
SUBLANES = 8
VMEM_LIMIT = 56 * 1024 * 1024
FFN_CHUNK = 256
INPROJ_CHUNK = 1024
MERGE_CHUNK = 256
SB_STOP = -110.0


def _params(*sem):
    return pltpu.CompilerParams(dimension_semantics=sem, vmem_limit_bytes=VMEM_LIMIT)


def _const_spec(shape):
    nd = len(shape)
    return pl.BlockSpec(shape, lambda *_: (0,) * nd, pipeline_mode=pl.Buffered(1))


def _split3(x):
    a = x.astype(BF16)
    r = x - a.astype(F32)
    b = r.astype(BF16)
    c = (r - b.astype(F32)).astype(BF16)
    return a, b, c


def _dot(a, b):
    return jnp.dot(a, b, preferred_element_type=F32)


def _dot_nt(a, b):
    return lax.dot_general(a, b, (((1,), (1,)), ((), ())), preferred_element_type=F32)


def _dot_f32_by_01(x, e01):
    a, b, c = _split3(x)
    return _dot(a, e01) + _dot(b, e01) + _dot(c, e01)


def _dot_01_by_f32(t01, x):
    a, b, c = _split3(x)
    return _dot(t01, a) + _dot(t01, b) + _dot(t01, c)


def _rms(x):
    return x * lax.rsqrt(jnp.mean(x * x, axis=-1, keepdims=True) + EPS)


def _silu(x):
    return x * jax.nn.sigmoid(x)


def _softplus(x):
    return jnp.maximum(x, 0.0) + jnp.log1p(jnp.exp(-jnp.abs(x)))


def _inproj_body(x_ref, g_ref, w_ref, wqkv_ref, wdt_ref, qg_ref, kg_ref, o_ref, dt_ref, qo_ref, ko_ref, vo_ref):
    xb = (_rms(x_ref[...]) * g_ref[...]).astype(BF16)
    dt_ref[...] = _dot(xb, wdt_ref[...])
    for lo in range(0, P_WIDTH, INPROJ_CHUNK):
        cs = slice(lo, min(lo + INPROJ_CHUNK, P_WIDTH))
        o_ref[:, cs] = _dot(xb, w_ref[:, cs]).astype(o_ref.dtype)
    scale = 1.0 / math.sqrt(SB_HEAD_DIM)
    q = _dot(xb, wqkv_ref[:, 0:SB_WIDTH])
    k = _dot(xb, wqkv_ref[:, SB_WIDTH:2 * SB_WIDTH])
    v = _dot(xb, wqkv_ref[:, 2 * SB_WIDTH:3 * SB_WIDTH])
    for h in range(SB_HEADS):
        sl = slice(h * SB_HEAD_DIM, (h + 1) * SB_HEAD_DIM)
        qo_ref[0, h] = ((_rms(q[:, sl]) * qg_ref[...]) * scale).astype(BF16)
        ko_ref[0, h] = (_rms(k[:, sl]) * kg_ref[...]).astype(BF16)
        vo_ref[0, h] = v[:, sl].astype(BF16)


def _inproj(x2, g, w_main, w_qkv, w_dt, qg, kg, bsz, seq):
    t = x2.shape[0]
    tm = min(512, seq)
    ns = seq // tm
    head_spec = pl.BlockSpec((1, SB_HEADS, tm, SB_HEAD_DIM), lambda i: (i // ns, 0, i % ns, 0))
    head_shape = jax.ShapeDtypeStruct((bsz, SB_HEADS, seq, SB_HEAD_DIM), BF16)
    return pl.pallas_call(
        _inproj_body,
        grid=(t // tm,),
        in_specs=[
            pl.BlockSpec((tm, D_MODEL), lambda i: (i, 0)),
            _const_spec((1, D_MODEL)),
            _const_spec((D_MODEL, P_WIDTH)),
            _const_spec((D_MODEL, 3 * SB_WIDTH)),
            _const_spec((D_MODEL, LANES)),
            _const_spec((1, SB_HEAD_DIM)),
            _const_spec((1, SB_HEAD_DIM)),
        ],
        out_specs=[
            pl.BlockSpec((tm, P_WIDTH), lambda i: (i, 0)),
            pl.BlockSpec((tm, LANES), lambda i: (i, 0)),
            head_spec, head_spec, head_spec,
        ],
        out_shape=[
            jax.ShapeDtypeStruct((t, P_WIDTH), PROJ_DTYPE),
            jax.ShapeDtypeStruct((t, LANES), F32),
            head_shape, head_shape, head_shape,
        ],
        compiler_params=_params("parallel"),
        name="inproj",
    )(x2, g, w_main, w_qkv, w_dt, qg, kg)


def _sb_body(q_ref, k_ref, v_ref, uo_ref, o_ref, carry_ref, acc_ref):
    t = SB_TILE
    first = pl.program_id(2) * SB_CHAINS
    uo = uo_ref[...]
    row = lax.broadcasted_iota(jnp.int32, (t, t), 0)
    col = lax.broadcasted_iota(jnp.int32, (t, t), 1)
    mask = col < row

    def visit(c, kb, diag):
        start = pl.multiple_of(jnp.maximum(kb, 0) * t, t)
        q = q_ref[0, 0, c * t:(c + 1) * t, :]
        k = k_ref[0, 0, pl.ds(start, t), :]
        v = v_ref[0, 0, pl.ds(start, t), :]
        z = _dot_nt(q, k)
        neg_log_rest = jnp.maximum(z, 0.0) + jnp.log(1.0 + jnp.exp(-jnp.abs(z)))
        log_beta = z - neg_log_rest
        if diag:
            neg_log_rest = jnp.where(mask, neg_log_rest, 0.0)
        sr = _dot(neg_log_rest.astype(BF16), uo)
        if diag:
            w = jnp.where(mask, jnp.exp(log_beta - sr[:, :t]), 0.0)
            carry = -sr[:, t:]
            acc_ref[c] = _dot(w.astype(BF16), v)
        else:
            prev = carry_ref[c]
            w = jnp.exp(log_beta - sr[:, :t] + jnp.concatenate([prev] * (t // LANES), axis=1))
            w = jnp.where(kb >= 0, w, 0.0)
            carry = prev - sr[:, t:]
            acc_ref[c] += _dot(w.astype(BF16), v)
        carry_ref[c] = carry
        return jnp.max(carry)

    def unfinished(j, maxima):
        need = [jnp.logical_and(first + c - j >= 0, maxima[c] > SB_STOP) for c in range(SB_CHAINS)]
        return functools.reduce(jnp.logical_or, need).astype(jnp.int32)

    maxima = [visit(c, first + c, True) for c in range(SB_CHAINS)]

    def body(state):
        j, _ = state
        maxima = [visit(c, first + c - j, False) for c in range(SB_CHAINS)]
        return j + 1, unfinished(j + 1, maxima)

    lax.while_loop(lambda state: state[1] > 0, body, (jnp.int32(1), unfinished(1, maxima)))
    for c in range(SB_CHAINS):
        o_ref[0, c * t:(c + 1) * t, :] = acc_ref[c].astype(o_ref.dtype)


def _sb_attention(qn, kn, vn):
    bsz, _, seq, _ = qn.shape
    t = SB_TILE
    tq = t * SB_CHAINS
    assert seq % tq == 0
    r = jnp.arange(t)
    upper = (r[:, None] > r[None, :])
    uo = jnp.concatenate([upper, jnp.ones((t, LANES), bool)], axis=1).astype(BF16)
    kv_spec = pl.BlockSpec((1, 1, seq, SB_HEAD_DIM), lambda b, h, i: (b, h, 0, 0))
    return pl.pallas_call(
        _sb_body,
        grid=(bsz, SB_HEADS, seq // tq),
        in_specs=[pl.BlockSpec((1, 1, tq, SB_HEAD_DIM), lambda b, h, i: (b, h, i, 0)),
                  kv_spec, kv_spec, _const_spec((t, t + LANES))],
        out_specs=pl.BlockSpec((1, tq, SB_HEAD_DIM), lambda b, h, i: (b, i, h)),
        out_shape=jax.ShapeDtypeStruct((bsz, seq, SB_WIDTH), BF16),
        scratch_shapes=[pltpu.VMEM((SB_CHAINS, t, LANES), F32), pltpu.VMEM((SB_CHAINS, t, SB_HEAD_DIM), F32)],
        compiler_params=_params("parallel", "parallel", "arbitrary"),
        name="sb_attention",
    )(qn, kn, vn, uo)


def _ssd_body(xbcs_ref, zs_ref, dts_ref, cw_ref, cb_ref, dtb_ref, alog_ref, dskip_ref, ng_ref, e_ref, tri_ref,
              shift_ref, os_ref, bufs_ref, sts_ref, ys_ref):
    tail = SSD_CONV_TAIL

    @pl.when(pl.program_id(0) == 0)
    def _():
        bufs_ref[:, 0:tail, :] = jnp.zeros((bufs_ref.shape[0], tail, SSD_CONV_DIM), bufs_ref.dtype)
        sts_ref[...] = jnp.zeros_like(sts_ref)

    for b in range(xbcs_ref.shape[0]):
        _ssd_chunk(xbcs_ref.at[b], zs_ref.at[b], dts_ref.at[b], cw_ref, cb_ref, dtb_ref, alog_ref, dskip_ref, ng_ref,
                   e_ref, tri_ref, shift_ref, os_ref.at[b], bufs_ref.at[b], sts_ref.at[b], ys_ref.at[b])


def _ssd_chunk(xbc_ref, z_ref, dt_ref, cw_ref, cb_ref, dtb_ref, alog_ref, dskip_ref, ng_ref, e_ref, tri_ref,
               shift_ref, o_ref, buf_ref, st_ref, y_ref):
    L = SSD_CHUNK
    G, R, P, N = SSD_GROUPS, SSD_HEADS_PER_GROUP, SSD_HEAD_DIM, SSD_STATE
    tail = SSD_CONV_TAIL
    buf_ref[tail:tail + L, :] = xbc_ref[...]

    def conv_silu_piece(lo, hi):
        delayed = _dot(shift_ref[...], buf_ref[:, lo:hi])
        acc = cb_ref[:, lo:hi] + cw_ref[SSD_CONV - 1:SSD_CONV, lo:hi] * xbc_ref[:, lo:hi].astype(F32)
        for k in range(SSD_CONV - 1):
            acc = acc + cw_ref[k:k + 1, lo:hi] * delayed[k * L:(k + 1) * L, :]
        return _silu(acc)

    def conv_silu(lo, hi):
        step = SSD_CONV_PIECE
        return jnp.concatenate([conv_silu_piece(c, c + step) for c in range(lo, hi, step)], axis=1)

    xs = conv_silu(0, SSD_INNER)
    b_in = conv_silu(SSD_INNER, SSD_INNER + G * N)
    c_in = conv_silu(SSD_INNER + G * N, SSD_CONV_DIM)
    buf_ref[0:tail, :] = buf_ref[L:L + tail, :]

    e01 = e_ref[...]
    dt = _softplus(dt_ref[...] + dtb_ref[...])
    da = dt * (-jnp.exp(alog_ref[...]))
    a_cs = _dot_01_by_f32(tri_ref[...], da)
    a_cs_t = a_cs.T
    a_full = _dot_f32_by_01(a_cs, e01)
    dt_full = _dot_f32_by_01(dt, e01)
    a_last = a_full[L - 1:L, :]
    ea_full = jnp.exp(a_full)
    x_dt = xs * dt_full
    xw = (x_dt * jnp.exp(a_last - a_full)).astype(BF16)
    ea_last = jnp.exp(a_last)

    row = lax.broadcasted_iota(jnp.int32, (L, L), 0)
    col = lax.broadcasted_iota(jnp.int32, (L, L), 1)
    causal = col <= row
    lane = lax.broadcasted_iota(jnp.int32, (L, LANES), 1)
    first_head = lane < P

    for g in range(G):
        gs = slice(g * N, (g + 1) * N)
        cg = c_in[:, gs].astype(BF16)
        bg_f32 = b_in[:, gs]
        cb = _dot_nt(cg, bg_f32.astype(BF16))
        for pair in range(R // 2):
            ms = []
            for r in range(2):
                h = g * R + pair * 2 + r
                seg = a_cs[:, h:h + 1] - a_cs_t[h:h + 1, :]
                decay = jnp.exp(jnp.where(causal, seg, -jnp.inf))
                ms.append((cb * decay).astype(BF16))
            c0 = (g * R + pair * 2) * P
            xp = x_dt[:, c0:c0 + LANES]
            rhs = jnp.concatenate([jnp.where(first_head, xp, 0.0).astype(BF16),
                                   jnp.where(first_head, 0.0, xp).astype(BF16)], axis=0)
            y_ref[:, c0:c0 + LANES] = _dot(jnp.concatenate(ms, axis=1), rhs)
        cs = slice(g * R * P, (g + 1) * R * P)
        state = st_ref[g]
        y_ref[:, cs] += _dot(cg, state.astype(BF16)) * ea_full[:, cs]
        st_ref[g] = state * ea_last[:, cs] + _dot(bg_f32.T.astype(BF16), xw[:, cs])

    y = y_ref[...] + xs * dskip_ref[...]
    y = y * _silu(z_ref[...].astype(F32))
    gw = SSD_INNER // G
    for g in range(G):
        cs = slice(g * gw, (g + 1) * gw)
        o_ref[:, cs] = (_rms(y[:, cs]) * ng_ref[:, cs]).astype(o_ref.dtype)


def _ssd(proj, dt_raw, conv_w, conv_b, dt_bias, a_log, d_skip, norm_g, bsz, seq):
    L = SSD_CHUNK
    nc = seq // L
    t = bsz * seq
    pad = LANES - SSD_HEADS
    heads = jnp.arange(LANES)
    cols = jnp.arange(SSD_INNER) // SSD_HEAD_DIM
    e01 = (heads[:, None] == cols[None, :]).astype(BF16)
    r = jnp.arange(L)
    tri = (r[None, :] <= r[:, None]).astype(BF16)
    assert proj.dtype == BF16, "the 0/1 shift product is exact only on bf16 data"
    delay = (SSD_CONV - 1) - jnp.arange((SSD_CONV - 1) * L) // L
    src = SSD_CONV_TAIL + jnp.arange((SSD_CONV - 1) * L) % L - delay
    shift = (src[:, None] == jnp.arange(SSD_CONV_TAIL + L)[None, :]).astype(BF16)
    rowvec = lambda v: v.reshape(1, -1).astype(F32)
    proj3 = proj.reshape(bsz, seq, proj.shape[-1])
    return pl.pallas_call(
        _ssd_body,
        grid=(nc,),
        in_specs=[
            pl.BlockSpec((bsz, L, SSD_CONV_DIM), lambda c: (0, c, P_XBC // SSD_CONV_DIM)),
            pl.BlockSpec((bsz, L, SSD_INNER), lambda c: (0, c, P_Z // SSD_INNER)),
            pl.BlockSpec((bsz, L, LANES), lambda c: (0, c, 0)),
            _const_spec((SSD_CONV, SSD_CONV_DIM)),
            _const_spec((1, SSD_CONV_DIM)),
            _const_spec((1, LANES)),
            _const_spec((1, LANES)),
            _const_spec((1, SSD_INNER)),
            _const_spec((1, SSD_INNER)),
            _const_spec((LANES, SSD_INNER)),
            _const_spec((L, L)),
            _const_spec(((SSD_CONV - 1) * L, SSD_CONV_TAIL + L)),
        ],
        out_specs=pl.BlockSpec((bsz, L, SSD_INNER), lambda c: (0, c, 0)),
        out_shape=jax.ShapeDtypeStruct((bsz, seq, SSD_INNER), BF16),
        scratch_shapes=[
            pltpu.VMEM((bsz, SSD_CONV_TAIL + L, SSD_CONV_DIM), BF16),
            pltpu.VMEM((bsz, SSD_GROUPS, SSD_STATE, SSD_HEADS_PER_GROUP * SSD_HEAD_DIM), F32),
            pltpu.VMEM((bsz, L, SSD_INNER), F32),
        ],
        compiler_params=_params("arbitrary"),
        name="ssd",
    )(proj3, proj3, dt_raw.reshape(bsz, seq, LANES), conv_w.astype(F32), rowvec(conv_b),
      jnp.pad(rowvec(dt_bias), ((0, 0), (0, pad))), jnp.pad(rowvec(a_log), ((0, 0), (0, pad))),
      rowvec(jnp.repeat(d_skip, SSD_HEAD_DIM)), rowvec(norm_g), e01, tri, shift).reshape(t, SSD_INNER)


def _merge_body(x_ref, yssd_ref, ysb_ref, u_ref, halo_ref, g0_ref, g1_ref, g2_ref,
                wssd_ref, wsb_ref, wpool_ref, wout_ref, pw_ref, ps_ref, o_ref, ext_ref, *, seq):
    tm = x_ref.shape[0]
    H = POOL_HALO
    start = (pl.program_id(0) * tm) % seq
    halo = halo_ref[...].astype(F32)
    ext_ref[0:H, :] = jnp.where(start == 0, jnp.zeros_like(halo), halo)
    ext_ref[H:H + tm, :] = u_ref[...].astype(F32)
    pos = start + lax.broadcasted_iota(jnp.int32, (tm, 1), 0)

    mixed = []
    for gi, win in enumerate(POOL_WINDOWS):
        cs = slice(gi * POOL_GROUP_DIM, (gi + 1) * POOL_GROUP_DIM)
        cur = ext_ref[H:H + tm, cs]
        wsum = cur
        for k in range(1, win):
            wsum = wsum + ext_ref[H - k:H - k + tm, cs]
        count = jnp.minimum(pos + 1, win).astype(F32)
        pooled = wsum / count - cur
        mixed.append((_dot(pooled.astype(BF16), pw_ref[gi]) * ps_ref[:, cs]).astype(BF16))
    ypool = jnp.concatenate(mixed, axis=1)
    yssd = yssd_ref[...]
    ysb = ysb_ref[...]
    merged = []
    for lo in range(0, D_MODEL, MERGE_CHUNK):
        cs = slice(lo, lo + MERGE_CHUNK)
        part = jax.nn.sigmoid(g0_ref[:, cs].astype(F32)) * _dot(yssd, wssd_ref[:, cs])
        part += jax.nn.sigmoid(g1_ref[:, cs].astype(F32)) * _dot(ysb, wsb_ref[:, cs])
        part += jax.nn.sigmoid(g2_ref[:, cs].astype(F32)) * _dot(ypool, wpool_ref[:, cs])
        merged.append(part.astype(BF16))
    o_ref[...] = x_ref[...] + _dot(jnp.concatenate(merged, axis=1), wout_ref[...])


def _merge(x2, y_ssd, y_sb, proj, pool_w, pool_scale, w_br_ssd, w_br_sb, w_br_pool, w_out, seq):
    t = x2.shape[0]
    tm = min(512, seq)
    hb = tm // POOL_HALO
    gate = lambda k: pl.BlockSpec((tm, D_MODEL), lambda i: (i, P_GATES // D_MODEL + k))
    return pl.pallas_call(
        functools.partial(_merge_body, seq=seq),
        grid=(t // tm,),
        in_specs=[
            pl.BlockSpec((tm, D_MODEL), lambda i: (i, 0)),
            pl.BlockSpec((tm, SSD_INNER), lambda i: (i, 0)),
            pl.BlockSpec((tm, SB_WIDTH), lambda i: (i, 0)),
            pl.BlockSpec((tm, POOL_WIDTH), lambda i: (i, P_U // POOL_WIDTH)),
            pl.BlockSpec((POOL_HALO, POOL_WIDTH), lambda i: (jnp.maximum(i * hb - 1, 0), P_U // POOL_WIDTH)),
            gate(0), gate(1), gate(2),
            _const_spec((SSD_INNER, D_MODEL)),
            _const_spec((SB_WIDTH, D_MODEL)),
            _const_spec((POOL_WIDTH, D_MODEL)),
            _const_spec((D_MODEL, D_MODEL)),
            _const_spec((POOL_GROUPS, POOL_GROUP_DIM, POOL_GROUP_DIM)),
            _const_spec((1, POOL_WIDTH)),
        ],
        out_specs=pl.BlockSpec((tm, D_MODEL), lambda i: (i, 0)),
        out_shape=jax.ShapeDtypeStruct((t, D_MODEL), F32),
        scratch_shapes=[pltpu.VMEM((tm + POOL_HALO, POOL_WIDTH), F32)],
        compiler_params=_params("parallel"),
        name="merge",
    )(x2, y_ssd, y_sb, proj, proj, proj, proj, proj,
      w_br_ssd.astype(BF16), w_br_sb.astype(BF16), w_br_pool.astype(BF16), w_out.astype(BF16),
      pool_w.astype(BF16), pool_scale.reshape(1, -1).astype(F32))


def _swiglu_into(acc_ref, xb, wg_ref, wu_ref, wd_ref, width):
    for c in range(width // FFN_CHUNK):
        cs = slice(c * FFN_CHUNK, (c + 1) * FFN_CHUNK)
        hidden = _silu(_dot(xb, wg_ref[:, cs])) * _dot(xb, wu_ref[:, cs])
        acc_ref[...] += _dot(hidden.astype(BF16), wd_ref[cs, :])


def _ffn_body(x_ref, g_ref, wg_ref, wu_ref, wd_ref, o_ref):
    x = x_ref[...]
    o_ref[...] = x
    _swiglu_into(o_ref, (_rms(x) * g_ref[...]).astype(BF16), wg_ref, wu_ref, wd_ref, FFN_DENSE)


def _ffn_dense(x2, g, w_gate, w_up, w_down):
    t = x2.shape[0]
    tm = min(512, t)
    return pl.pallas_call(
        _ffn_body,
        grid=(t // tm,),
        in_specs=[
            pl.BlockSpec((tm, D_MODEL), lambda i: (i, 0)),
            _const_spec((1, D_MODEL)),
            _const_spec((D_MODEL, FFN_DENSE)),
            _const_spec((D_MODEL, FFN_DENSE)),
            _const_spec((FFN_DENSE, D_MODEL)),
        ],
        out_specs=pl.BlockSpec((tm, D_MODEL), lambda i: (i, 0)),
        out_shape=jax.ShapeDtypeStruct((t, D_MODEL), F32),
        compiler_params=_params("parallel"),
        name="ffn_dense",
    )(x2, g, w_gate.astype(BF16), w_up.astype(BF16), w_down.astype(BF16))


def _router_body(x_ref, g_ref, rw_ref, h_ref, idx_ref, wt_ref):
    h = _rms(x_ref[...]) * g_ref[...]
    h_ref[...] = h
    rw = rw_ref[...]
    h_hi = h.astype(BF16)
    h_lo = (h - h_hi.astype(F32)).astype(BF16)
    w_hi = rw.astype(BF16)
    w_lo = (rw - w_hi.astype(F32)).astype(BF16)
    logits = _dot(h_hi, w_hi) + (_dot(h_hi, w_lo) + _dot(h_lo, w_hi))
    lane = lax.broadcasted_iota(jnp.int32, logits.shape, 1)
    lane_f = lane.astype(F32)
    logits = jnp.where(lane < N_EXPERTS, logits, -jnp.inf)
    m1 = jnp.max(logits, axis=-1, keepdims=True)
    i1 = jnp.min(jnp.where(logits == m1, lane_f, float(LANES)), axis=-1, keepdims=True)
    rest = jnp.where(lane_f == i1, -jnp.inf, logits)
    m2 = jnp.max(rest, axis=-1, keepdims=True)
    i2 = jnp.min(jnp.where(rest == m2, lane_f, float(LANES)), axis=-1, keepdims=True)
    e = jnp.exp(m2 - m1)
    w1 = 1.0 / (1.0 + e)
    idx = jnp.where(lane == 0, i1, jnp.where(lane == 1, i2, 0.0))
    idx_ref[...] = idx.T[0:idx_ref.shape[0], :].astype(jnp.int32)
    wt_ref[...] = jnp.where(lane == 0, w1, jnp.where(lane == 1, e * w1, 0.0))


def _router(x2, g, router_w):
    t = x2.shape[0]
    tm = min(512, t)
    rw = jnp.pad(router_w.astype(F32), ((0, 0), (0, LANES - N_EXPERTS)))
    row = pl.BlockSpec((tm, D_MODEL), lambda i: (i, 0))
    small = pl.BlockSpec((tm, LANES), lambda i: (i, 0))
    return pl.pallas_call(
        _router_body,
        grid=(t // tm,),
        in_specs=[row, _const_spec((1, D_MODEL)), _const_spec((D_MODEL, LANES))],
        out_specs=[row, pl.BlockSpec((SUBLANES, tm), lambda i: (0, i)), small],
        out_shape=[jax.ShapeDtypeStruct((t, D_MODEL), F32),
                   jax.ShapeDtypeStruct((SUBLANES, t), jnp.int32),
                   jax.ShapeDtypeStruct((t, LANES), F32)],
        compiler_params=_params("parallel"),
        name="router",
    )(x2, g, rw)


def _dispatch_body(pends_ref, counts_ref, slot_ref, h_ref, xb_hbm, zero_ref, sem):
    tm = h_ref.shape[0]

    def zero_block(start):
        fill = pltpu.make_async_copy(zero_ref, xb_hbm.at[pl.ds(pl.multiple_of(start, MOE_BLOCK), MOE_BLOCK)], sem)
        fill.start()
        fill.wait()

    @pl.when(pl.program_id(0) == 0)
    def _():
        zero_ref[...] = jnp.zeros_like(zero_ref)
        for e in range(N_EXPERTS):
            pl.when(counts_ref[e] > 0)(functools.partial(zero_block, pends_ref[e] - MOE_BLOCK))
        for j in range(N_EXPERTS):
            start = pends_ref[N_EXPERTS - 1] + j * MOE_BLOCK
            pl.when(start < xb_hbm.shape[0])(functools.partial(zero_block, start))

    for k in range(TOP_K):
        for r in range(tm):
            pltpu.make_async_copy(h_ref.at[pl.ds(r, 1)], xb_hbm.at[pl.ds(slot_ref[0, 0, k * tm + r], 1)], sem).start()
    for k in range(TOP_K):
        pltpu.make_async_copy(h_ref, xb_hbm.at[pl.ds(0, tm)], sem).wait()


def _dispatch(h, slot_tiles, pends, counts, cap):
    t = h.shape[0]
    nt = slot_tiles.shape[0]
    tm = t // nt
    grid_spec = pltpu.PrefetchScalarGridSpec(
        num_scalar_prefetch=2,
        grid=(nt,),
        in_specs=[
            pl.BlockSpec((1, 1, TOP_K * tm), lambda i, pe, co: (i, 0, 0), memory_space=pltpu.SMEM),
            pl.BlockSpec((tm, D_MODEL), lambda i, pe, co: (i, 0)),
        ],
        out_specs=pl.BlockSpec(memory_space=pl.ANY),
        scratch_shapes=[pltpu.VMEM((MOE_BLOCK, D_MODEL), F32), pltpu.SemaphoreType.DMA],
    )
    return pl.pallas_call(
        _dispatch_body,
        grid_spec=grid_spec,
        out_shape=jax.ShapeDtypeStruct((cap, D_MODEL), F32),
        compiler_params=_params("arbitrary"),
        name="dispatch",
    )(pends.astype(jnp.int32), counts.astype(jnp.int32), slot_tiles, h)


def _experts_body(bexp_ref, nused_ref, xb_ref, wg_ref, wu_ref, wd_ref, o_ref):
    del bexp_ref
    o_ref[...] = jnp.zeros_like(o_ref)

    @pl.when(pl.program_id(0) < nused_ref[0])
    def _():
        _swiglu_into(o_ref, xb_ref[...].astype(BF16), wg_ref, wu_ref, wd_ref, FFN_EXPERT)


def _experts(xb, block_exp, n_used, e_gate, e_up, e_down):
    n_blocks = block_exp.shape[0]
    wspec = lambda shape: pl.BlockSpec((None,) + shape, lambda i, be, nu: (be[i], 0, 0))
    rows = pl.BlockSpec((MOE_BLOCK, D_MODEL), lambda i, be, nu: (i, 0))
    used_rows = pl.BlockSpec((MOE_BLOCK, D_MODEL), lambda i, be, nu: (jnp.minimum(i, nu[0] - 1), 0))
    grid_spec = pltpu.PrefetchScalarGridSpec(
        num_scalar_prefetch=2,
        grid=(n_blocks,),
        in_specs=[used_rows, wspec((D_MODEL, FFN_EXPERT)), wspec((D_MODEL, FFN_EXPERT)),
                  wspec((FFN_EXPERT, D_MODEL))],
        out_specs=rows,
    )
    return pl.pallas_call(
        _experts_body,
        grid_spec=grid_spec,
        out_shape=jax.ShapeDtypeStruct((n_blocks * MOE_BLOCK, D_MODEL), F32),
        compiler_params=_params("arbitrary"),
        name="experts",
    )(block_exp, n_used, xb, e_gate.astype(BF16), e_up.astype(BF16), e_down.astype(BF16))


def _combine_body(slot_ref, next_ref, x_ref, wt_ref, yb_hbm, o_ref, buf_ref, sem):
    tm = x_ref.shape[0]
    n = TOP_K * tm
    i = pl.program_id(0)

    def fetch(idx_ref, b):
        for r in range(n):
            pltpu.make_async_copy(yb_hbm.at[pl.ds(idx_ref[0, 0, r], 1)], buf_ref.at[b, pl.ds(r, 1)], sem.at[b]).start()

    @pl.when(i == 0)
    def _():
        fetch(slot_ref, 0)

    for cur in range(2):
        @pl.when(i % 2 == cur)
        def _():
            @pl.when(i + 1 < pl.num_programs(0))
            def _():
                fetch(next_ref, 1 - cur)

            pltpu.make_async_copy(yb_hbm.at[pl.ds(0, n)], buf_ref.at[cur], sem.at[cur]).wait()
            wt = wt_ref[...]
            o_ref[...] = x_ref[...] + (wt[:, 0:1] * buf_ref[cur, 0:tm, :] + wt[:, 1:2] * buf_ref[cur, tm:n, :])


def _combine(x2, wt, slot_tiles, yb):
    t = x2.shape[0]
    nt = slot_tiles.shape[0]
    tm = t // nt
    slots = lambda shift: pl.BlockSpec((1, 1, TOP_K * tm), lambda i: (jnp.minimum(i + shift, nt - 1), 0, 0),
                                       memory_space=pltpu.SMEM)
    return pl.pallas_call(
        _combine_body,
        grid=(nt,),
        in_specs=[
            slots(0), slots(1),
            pl.BlockSpec((tm, D_MODEL), lambda i: (i, 0)),
            pl.BlockSpec((tm, LANES), lambda i: (i, 0)),
            pl.BlockSpec(memory_space=pl.ANY),
        ],
        out_specs=pl.BlockSpec((tm, D_MODEL), lambda i: (i, 0)),
        out_shape=jax.ShapeDtypeStruct((t, D_MODEL), F32),
        scratch_shapes=[pltpu.VMEM((2, TOP_K * tm, D_MODEL), F32), pltpu.SemaphoreType.DMA((2,))],
        compiler_params=_params("arbitrary"),
        name="combine",
    )(slot_tiles, slot_tiles, x2, wt, yb)


def _moe(x2, g, router_w, e_gate, e_up, e_down):
    t = x2.shape[0]
    h, idx, wt = _router(x2, g, router_w)
    n_assign = t * TOP_K
    cap = -(-n_assign // MOE_BLOCK) * MOE_BLOCK + N_EXPERTS * MOE_BLOCK
    n_blocks = cap // MOE_BLOCK
    expert_flat = idx[:TOP_K].reshape(-1)
    onehot = (expert_flat[None, :] == jnp.arange(N_EXPERTS, dtype=jnp.int32)[:, None]).astype(jnp.int32)
    running = jnp.cumsum(onehot, axis=1)
    counts = running[:, -1]
    rank = jnp.sum(running * onehot, axis=0) - 1
    padded = (counts + MOE_BLOCK - 1) // MOE_BLOCK * MOE_BLOCK
    pends = jnp.cumsum(padded)
    pstarts = pends - padded
    dest = (jnp.sum(pstarts[:, None] * onehot, axis=0) + rank).astype(jnp.int32)
    block_exp = jnp.minimum(jnp.searchsorted(pends, jnp.arange(n_blocks) * MOE_BLOCK, side="right"),
                            N_EXPERTS - 1).astype(jnp.int32)
    n_used = (pends[-1:] // MOE_BLOCK).astype(jnp.int32)
    def slot_tiles(tile):
        tm = min(tile, t)
        return dest.reshape(TOP_K, t // tm, tm).transpose(1, 0, 2).reshape(t // tm, 1, TOP_K * tm)

    xb = _dispatch(h, slot_tiles(DISPATCH_TILE), pends, counts, cap)
    yb = _experts(xb, block_exp, n_used, e_gate, e_up, e_down)
    return _combine(x2, wt, slot_tiles(COMBINE_TILE), yb)


def _permute_w_in(w):
    w = w.astype(BF16)
    main = jnp.concatenate([w[:, COL_Z:COL_XBC], w[:, :COL_Z], w[:, COL_POOL:], w[:, COL_V:COL_POOL]], axis=1)
    qkv = w[:, COL_DT:COL_V]
    dt = jnp.pad(w[:, COL_XBC:COL_DT], ((0, 0), (0, LANES - SSD_HEADS)))
    return main, qkv, dt


def kernel(x, mix_norm_g, w_in, conv_w, conv_b, dt_bias, a_log, d_skip, ssd_norm_g, q_norm_g, k_norm_g, pool_w,
           pool_scale, w_br_ssd, w_br_sb, w_br_pool, w_out, ffn_norm_g, ffn_w_gate, ffn_w_up, ffn_w_down,
           router_w, moe_w_gate, moe_w_up, moe_w_down):
    bsz, seq, d = x.shape
    depth = w_in.shape[0]
    x2 = x.reshape(bsz * seq, d)
    rowvec = lambda v: v.reshape(1, -1).astype(F32)
    for layer in range(depth):
        w_main, w_qkv, w_dt = _permute_w_in(w_in[layer])
        proj, dt_raw, qn, kn, vn = _inproj(x2, rowvec(mix_norm_g[layer]), w_main, w_qkv, w_dt,
                                           rowvec(q_norm_g[layer]), rowvec(k_norm_g[layer]), bsz, seq)
        y_sb = _sb_attention(qn, kn, vn).reshape(bsz * seq, SB_WIDTH)
        y_ssd = _ssd(proj, dt_raw, conv_w[layer], conv_b[layer], dt_bias[layer], a_log[layer], d_skip[layer],
                     ssd_norm_g[layer], bsz, seq)
        x2 = _merge(x2, y_ssd, y_sb, proj, pool_w[layer], pool_scale[layer], w_br_ssd[layer], w_br_sb[layer],
                    w_br_pool[layer], w_out[layer], seq)
        i = layer // 2
        if layer % 2 == 0:
            x2 = _ffn_dense(x2, rowvec(ffn_norm_g[layer]), ffn_w_gate[i], ffn_w_up[i], ffn_w_down[i])
        else:
            x2 = _moe(x2, rowvec(ffn_norm_g[layer]), router_w[i], moe_w_gate[i], moe_w_up[i], moe_w_down[i])
    return x2.reshape(bsz, seq, d)
```

```python
import functools
import math

import jax
import jax.numpy as jnp
from jax import lax
from jax.experimental import pallas as pl
from jax.experimental.pallas import tpu as pltpu

F32 = jnp.float32
BF16 = jnp.bfloat16

D_MODEL = 1024
EPS = 1e-6

SSD_INNER = 1024
SSD_HEAD_DIM = 64
SSD_HEADS = 16
SSD_GROUPS = 4
SSD_HEADS_PER_GROUP = 4
SSD_STATE = 128
SSD_CONV = 4
SSD_CHUNK = 128
SSD_CONV_DIM = SSD_INNER + 2 * SSD_GROUPS * SSD_STATE
SSD_CONV_TAIL = 16
SSD_CONV_PIECE = 512

SB_HEADS = 4
SB_HEAD_DIM = 128
SB_WIDTH = SB_HEADS * SB_HEAD_DIM
SB_TILE = 256
SB_CHAINS = 16

POOL_WINDOWS = (2, 4, 8, 16)
POOL_GROUPS = 4
POOL_WIDTH = 512
POOL_GROUP_DIM = 128
POOL_HALO = 16

N_BRANCHES = 3
FFN_DENSE = 2816
N_EXPERTS = 8
TOP_K = 2
FFN_EXPERT = 1792
MOE_BLOCK = 512
DISPATCH_TILE = 1024
COMBINE_TILE = 256

COL_Z = SSD_INNER
COL_XBC = COL_Z + SSD_CONV_DIM
COL_DT = COL_XBC + SSD_HEADS
COL_Q = COL_DT + SB_WIDTH
COL_K = COL_Q + SB_WIDTH
COL_V = COL_K + SB_WIDTH
COL_POOL = COL_V + POOL_WIDTH

P_XBC = 0
P_Z = 2048
P_GATES = 3072
P_U = 6144
P_WIDTH = 6656

PROJ_DTYPE = BF16
LANES = 128
SUBLANES = 8
VMEM_LIMIT = 56 * 1024 * 1024
FFN_CHUNK = 256
INPROJ_CHUNK = 1024
MERGE_CHUNK = 256
SB_STOP = -110.0


def _params(*sem):
    return pltpu.CompilerParams(dimension_semantics=sem, vmem_limit_bytes=VMEM_LIMIT)


def _const_spec(shape):
    nd = len(shape)
    return pl.BlockSpec(shape, lambda *_: (0,) * nd, pipeline_mode=pl.Buffered(1))


def _split3(x):
    a = x.astype(BF16)
    r = x - a.astype(F32)
    b = r.astype(BF16)
    c = (r - b.astype(F32)).astype(BF16)
    return a, b, c


def _dot(a, b):
    return jnp.dot(a, b, preferred_element_type=F32)


def _dot_nt(a, b):
    return lax.dot_general(a, b, (((1,), (1,)), ((), ())), preferred_element_type=F32)


def _dot_f32_by_01(x, e01):
    a, b, c = _split3(x)
    return _dot(a, e01) + _dot(b, e01) + _dot(c, e01)


def _dot_01_by_f32(t01, x):
    a, b, c = _split3(x)
    return _dot(t01, a) + _dot(t01, b) + _dot(t01, c)


def _rms(x):
    return x * lax.rsqrt(jnp.mean(x * x, axis=-1, keepdims=True) + EPS)


def _silu(x):
    return x * jax.nn.sigmoid(x)


def _softplus(x):
    return jnp.maximum(x, 0.0) + jnp.log1p(jnp.exp(-jnp.abs(x)))


def _inproj_body(x_ref, g_ref, w_ref, wqkv_ref, wdt_ref, qg_ref, kg_ref, o_ref, dt_ref, qo_ref, ko_ref, vo_ref):
    xb = (_rms(x_ref[...]) * g_ref[...]).astype(BF16)
    dt_ref[...] = _dot(xb, wdt_ref[...])
    for lo in range(0, P_WIDTH, INPROJ_CHUNK):
        cs = slice(lo, min(lo + INPROJ_CHUNK, P_WIDTH))
        o_ref[:, cs] = _dot(xb, w_ref[:, cs]).astype(o_ref.dtype)
    scale = 1.0 / math.sqrt(SB_HEAD_DIM)
    q = _dot(xb, wqkv_ref[:, 0:SB_WIDTH])
    k = _dot(xb, wqkv_ref[:, SB_WIDTH:2 * SB_WIDTH])
    v = _dot(xb, wqkv_ref[:, 2 * SB_WIDTH:3 * SB_WIDTH])
    for h in range(SB_HEADS):
        sl = slice(h * SB_HEAD_DIM, (h + 1) * SB_HEAD_DIM)
        qo_ref[0, h] = ((_rms(q[:, sl]) * qg_ref[...]) * scale).astype(BF16)
        ko_ref[0, h] = (_rms(k[:, sl]) * kg_ref[...]).astype(BF16)
        vo_ref[0, h] = v[:, sl].astype(BF16)


def _inproj(x2, g, w_main, w_qkv, w_dt, qg, kg, bsz, seq):
    t = x2.shape[0]
    tm = min(512, seq)
    ns = seq // tm
    head_spec = pl.BlockSpec((1, SB_HEADS, tm, SB_HEAD_DIM), lambda i: (i // ns, 0, i % ns, 0))
    head_shape = jax.ShapeDtypeStruct((bsz, SB_HEADS, seq, SB_HEAD_DIM), BF16)
    return pl.pallas_call(
        _inproj_body,
        grid=(t // tm,),
        in_specs=[
            pl.BlockSpec((tm, D_MODEL), lambda i: (i, 0)),
            _const_spec((1, D_MODEL)),
            _const_spec((D_MODEL, P_WIDTH)),
            _const_spec((D_MODEL, 3 * SB_WIDTH)),
            _const_spec((D_MODEL, LANES)),
            _const_spec((1, SB_HEAD_DIM)),
            _const_spec((1, SB_HEAD_DIM)),
        ],
        out_specs=[
            pl.BlockSpec((tm, P_WIDTH), lambda i: (i, 0)),
            pl.BlockSpec((tm, LANES), lambda i: (i, 0)),
            head_spec, head_spec, head_spec,
        ],
        out_shape=[
            jax.ShapeDtypeStruct((t, P_WIDTH), PROJ_DTYPE),
            jax.ShapeDtypeStruct((t, LANES), F32),
            head_shape, head_shape, head_shape,
        ],
        compiler_params=_params("parallel"),
        name="inproj",
    )(x2, g, w_main, w_qkv, w_dt, qg, kg)


def _sb_body(q_ref, k_ref, v_ref, uo_ref, o_ref, carry_ref, acc_ref):
    t = SB_TILE
    first = pl.program_id(2) * SB_CHAINS
    uo = uo_ref[...]
    row = lax.broadcasted_iota(jnp.int32, (t, t), 0)
    col = lax.broadcasted_iota(jnp.int32, (t, t), 1)
    mask = col < row

    def visit(c, kb, diag):
        start = pl.multiple_of(jnp.maximum(kb, 0) * t, t)
        q = q_ref[0, 0, c * t:(c + 1) * t, :]
        k = k_ref[0, 0, pl.ds(start, t), :]
        v = v_ref[0, 0, pl.ds(start, t), :]
        z = _dot_nt(q, k)
        neg_log_rest = jnp.maximum(z, 0.0) + jnp.log(1.0 + jnp.exp(-jnp.abs(z)))
        log_beta = z - neg_log_rest
        if diag:
            neg_log_rest = jnp.where(mask, neg_log_rest, 0.0)
        sr = _dot(neg_log_rest.astype(BF16), uo)
        if diag:
            w = jnp.where(mask, jnp.exp(log_beta - sr[:, :t]), 0.0)
            carry = -sr[:, t:]
            acc_ref[c] = _dot(w.astype(BF16), v)
        else:
            prev = carry_ref[c]
            w = jnp.exp(log_beta - sr[:, :t] + jnp.concatenate([prev] * (t // LANES), axis=1))
            w = jnp.where(kb >= 0, w, 0.0)
            carry = prev - sr[:, t:]
            acc_ref[c] += _dot(w.astype(BF16), v)
        carry_ref[c] = carry
        return jnp.max(carry)

    def unfinished(j, maxima):
        need = [jnp.logical_and(first + c - j >= 0, maxima[c] > SB_STOP) for c in range(SB_CHAINS)]
        return functools.reduce(jnp.logical_or, need).astype(jnp.int32)

    maxima = [visit(c, first + c, True) for c in range(SB_CHAINS)]

    def body(state):
        j, _ = state
        maxima = [visit(c, first + c - j, False) for c in range(SB_CHAINS)]
        return j + 1, unfinished(j + 1, maxima)

    lax.while_loop(lambda state: state[1] > 0, body, (jnp.int32(1), unfinished(1, maxima)))
    for c in range(SB_CHAINS):
        o_ref[0, c * t:(c + 1) * t, :] = acc_ref[c].astype(o_ref.dtype)


def _sb_attention(qn, kn, vn):
    bsz, _, seq, _ = qn.shape
    t = SB_TILE
    tq = t * SB_CHAINS
    assert seq % tq == 0
    r = jnp.arange(t)
    upper = (r[:, None] > r[None, :])
    uo = jnp.concatenate([upper, jnp.ones((t, LANES), bool)], axis=1).astype(BF16)
    kv_spec = pl.BlockSpec((1, 1, seq, SB_HEAD_DIM), lambda b, h, i: (b, h, 0, 0))
    return pl.pallas_call(
        _sb_body,
        grid=(bsz, SB_HEADS, seq // tq),
        in_specs=[pl.BlockSpec((1, 1, tq, SB_HEAD_DIM), lambda b, h, i: (b, h, i, 0)),
                  kv_spec, kv_spec, _const_spec((t, t + LANES))],
        out_specs=pl.BlockSpec((1, tq, SB_HEAD_DIM), lambda b, h, i: (b, i, h)),
        out_shape=jax.ShapeDtypeStruct((bsz, seq, SB_WIDTH), BF16),
        scratch_shapes=[pltpu.VMEM((SB_CHAINS, t, LANES), F32), pltpu.VMEM((SB_CHAINS, t, SB_HEAD_DIM), F32)],
        compiler_params=_params("parallel", "parallel", "arbitrary"),
        name="sb_attention",
    )(qn, kn, vn, uo)


def _ssd_body(xbcs_ref, zs_ref, dts_ref, cw_ref, cb_ref, dtb_ref, alog_ref, dskip_ref, ng_ref, e_ref, tri_ref,
              shift_ref, os_ref, bufs_ref, sts_ref, ys_ref):
    tail = SSD_CONV_TAIL

    @pl.when(pl.program_id(0) == 0)
    def _():
        bufs_ref[:, 0:tail, :] = jnp.zeros((bufs_ref.shape[0], tail, SSD_CONV_DIM), bufs_ref.dtype)
        sts_ref[...] = jnp.zeros_like(sts_ref)

    for b in range(xbcs_ref.shape[0]):
        _ssd_chunk(xbcs_ref.at[b], zs_ref.at[b], dts_ref.at[b], cw_ref, cb_ref, dtb_ref, alog_ref, dskip_ref, ng_ref,
                   e_ref, tri_ref, shift_ref, os_ref.at[b], bufs_ref.at[b], sts_ref.at[b], ys_ref.at[b])


def _ssd_chunk(xbc_ref, z_ref, dt_ref, cw_ref, cb_ref, dtb_ref, alog_ref, dskip_ref, ng_ref, e_ref, tri_ref,
               shift_ref, o_ref, buf_ref, st_ref, y_ref):
    L = SSD_CHUNK
    G, R, P, N = SSD_GROUPS, SSD_HEADS_PER_GROUP, SSD_HEAD_DIM, SSD_STATE
    tail = SSD_CONV_TAIL
    buf_ref[tail:tail + L, :] = xbc_ref[...]

    def conv_silu_piece(lo, hi):
        delayed = _dot(shift_ref[...], buf_ref[:, lo:hi])
        acc = cb_ref[:, lo:hi] + cw_ref[SSD_CONV - 1:SSD_CONV, lo:hi] * xbc_ref[:, lo:hi].astype(F32)
        for k in range(SSD_CONV - 1):
            acc = acc + cw_ref[k:k + 1, lo:hi] * delayed[k * L:(k + 1) * L, :]
        return _silu(acc)

    def conv_silu(lo, hi):
        step = SSD_CONV_PIECE
        return jnp.concatenate([conv_silu_piece(c, c + step) for c in range(lo, hi, step)], axis=1)

    xs = conv_silu(0, SSD_INNER)
    b_in = conv_silu(SSD_INNER, SSD_INNER + G * N)
    c_in = conv_silu(SSD_INNER + G * N, SSD_CONV_DIM)
    buf_ref[0:tail, :] = buf_ref[L:L + tail, :]

    e01 = e_ref[...]
    dt = _softplus(dt_ref[...] + dtb_ref[...])
    da = dt * (-jnp.exp(alog_ref[...]))
    a_cs = _dot_01_by_f32(tri_ref[...], da)
    a_cs_t = a_cs.T
    a_full = _dot_f32_by_01(a_cs, e01)
    dt_full = _dot_f32_by_01(dt, e01)
    a_last = a_full[L - 1:L, :]
    ea_full = jnp.exp(a_full)
    x_dt = xs * dt_full
    xw = (x_dt * jnp.exp(a_last - a_full)).astype(BF16)
    ea_last = jnp.exp(a_last)

    row = lax.broadcasted_iota(jnp.int32, (L, L), 0)
    col = lax.broadcasted_iota(jnp.int32, (L, L), 1)
    causal = col <= row
    lane = lax.broadcasted_iota(jnp.int32, (L, LANES), 1)
    first_head = lane < P

    for g in range(G):
        gs = slice(g * N, (g + 1) * N)
        cg = c_in[:, gs].astype(BF16)
        bg_f32 = b_in[:, gs]
        cb = _dot_nt(cg, bg_f32.astype(BF16))
        for pair in range(R // 2):
            ms = []
            for r in range(2):
                h = g * R + pair * 2 + r
                seg = a_cs[:, h:h + 1] - a_cs_t[h:h + 1, :]
                decay = jnp.exp(jnp.where(causal, seg, -jnp.inf))
                ms.append((cb * decay).astype(BF16))
            c0 = (g * R + pair * 2) * P
            xp = x_dt[:, c0:c0 + LANES]
            rhs = jnp.concatenate([jnp.where(first_head, xp, 0.0).astype(BF16),
                                   jnp.where(first_head, 0.0, xp).astype(BF16)], axis=0)
            y_ref[:, c0:c0 + LANES] = _dot(jnp.concatenate(ms, axis=1), rhs)
        cs = slice(g * R * P, (g + 1) * R * P)
        state = st_ref[g]
        y_ref[:, cs] += _dot(cg, state.astype(BF16)) * ea_full[:, cs]
        st_ref[g] = state * ea_last[:, cs] + _dot(bg_f32.T.astype(BF16), xw[:, cs])

    y = y_ref[...] + xs * dskip_ref[...]
    y = y * _silu(z_ref[...].astype(F32))
    gw = SSD_INNER // G
    for g in range(G):
        cs = slice(g * gw, (g + 1) * gw)
        o_ref[:, cs] = (_rms(y[:, cs]) * ng_ref[:, cs]).astype(o_ref.dtype)


def _ssd(proj, dt_raw, conv_w, conv_b, dt_bias, a_log, d_skip, norm_g, bsz, seq):
    L = SSD_CHUNK
    nc = seq // L
    t = bsz * seq
    pad = LANES - SSD_HEADS
    heads = jnp.arange(LANES)
    cols = jnp.arange(SSD_INNER) // SSD_HEAD_DIM
    e01 = (heads[:, None] == cols[None, :]).astype(BF16)
    r = jnp.arange(L)
    tri = (r[None, :] <= r[:, None]).astype(BF16)
    assert proj.dtype == BF16, "the 0/1 shift product is exact only on bf16 data"
    delay = (SSD_CONV - 1) - jnp.arange((SSD_CONV - 1) * L) // L
    src = SSD_CONV_TAIL + jnp.arange((SSD_CONV - 1) * L) % L - delay
    shift = (src[:, None] == jnp.arange(SSD_CONV_TAIL + L)[None, :]).astype(BF16)
    rowvec = lambda v: v.reshape(1, -1).astype(F32)
    proj3 = proj.reshape(bsz, seq, proj.shape[-1])
    return pl.pallas_call(
        _ssd_body,
        grid=(nc,),
        in_specs=[
            pl.BlockSpec((bsz, L, SSD_CONV_DIM), lambda c: (0, c, P_XBC // SSD_CONV_DIM)),
            pl.BlockSpec((bsz, L, SSD_INNER), lambda c: (0, c, P_Z // SSD_INNER)),
            pl.BlockSpec((bsz, L, LANES), lambda c: (0, c, 0)),
            _const_spec((SSD_CONV, SSD_CONV_DIM)),
            _const_spec((1, SSD_CONV_DIM)),
            _const_spec((1, LANES)),
            _const_spec((1, LANES)),
            _const_spec((1, SSD_INNER)),
            _const_spec((1, SSD_INNER)),
            _const_spec((LANES, SSD_INNER)),
            _const_spec((L, L)),
            _const_spec(((SSD_CONV - 1) * L, SSD_CONV_TAIL + L)),
        ],
        out_specs=pl.BlockSpec((bsz, L, SSD_INNER), lambda c: (0, c, 0)),
        out_shape=jax.ShapeDtypeStruct((bsz, seq, SSD_INNER), BF16),
        scratch_shapes=[
            pltpu.VMEM((bsz, SSD_CONV_TAIL + L, SSD_CONV_DIM), BF16),
            pltpu.VMEM((bsz, SSD_GROUPS, SSD_STATE, SSD_HEADS_PER_GROUP * SSD_HEAD_DIM), F32),
            pltpu.VMEM((bsz, L, SSD_INNER), F32),
        ],
        compiler_params=_params("arbitrary"),
        name="ssd",
    )(proj3, proj3, dt_raw.reshape(bsz, seq, LANES), conv_w.astype(F32), rowvec(conv_b),
      jnp.pad(rowvec(dt_bias), ((0, 0), (0, pad))), jnp.pad(rowvec(a_log), ((0, 0), (0, pad))),
      rowvec(jnp.repeat(d_skip, SSD_HEAD_DIM)), rowvec(norm_g), e01, tri, shift).reshape(t, SSD_INNER)


def _merge_body(x_ref, yssd_ref, ysb_ref, u_ref, halo_ref, g0_ref, g1_ref, g2_ref,
                wssd_ref, wsb_ref, wpool_ref, wout_ref, pw_ref, ps_ref, o_ref, ext_ref, *, seq):
    tm = x_ref.shape[0]
    H = POOL_HALO
    start = (pl.program_id(0) * tm) % seq
    halo = halo_ref[...].astype(F32)
    ext_ref[0:H, :] = jnp.where(start == 0, jnp.zeros_like(halo), halo)
    ext_ref[H:H + tm, :] = u_ref[...].astype(F32)
    pos = start + lax.broadcasted_iota(jnp.int32, (tm, 1), 0)

    mixed = []
    for gi, win in enumerate(POOL_WINDOWS):
        cs = slice(gi * POOL_GROUP_DIM, (gi + 1) * POOL_GROUP_DIM)
        cur = ext_ref[H:H + tm, cs]
        wsum = cur
        for k in range(1, win):
            wsum = wsum + ext_ref[H - k:H - k + tm, cs]
        count = jnp.minimum(pos + 1, win).astype(F32)
        pooled = wsum / count - cur
        mixed.append((_dot(pooled.astype(BF16), pw_ref[gi]) * ps_ref[:, cs]).astype(BF16))
    ypool = jnp.concatenate(mixed, axis=1)
    yssd = yssd_ref[...]
    ysb = ysb_ref[...]
    merged = []
    for lo in range(0, D_MODEL, MERGE_CHUNK):
        cs = slice(lo, lo + MERGE_CHUNK)
        part = jax.nn.sigmoid(g0_ref[:, cs].astype(F32)) * _dot(yssd, wssd_ref[:, cs])
        part += jax.nn.sigmoid(g1_ref[:, cs].astype(F32)) * _dot(ysb, wsb_ref[:, cs])
        part += jax.nn.sigmoid(g2_ref[:, cs].astype(F32)) * _dot(ypool, wpool_ref[:, cs])
        merged.append(part.astype(BF16))
    o_ref[...] = x_ref[...] + _dot(jnp.concatenate(merged, axis=1), wout_ref[...])


def _merge(x2, y_ssd, y_sb, proj, pool_w, pool_scale, w_br_ssd, w_br_sb, w_br_pool, w_out, seq):
    t = x2.shape[0]
    tm = min(512, seq)
    hb = tm // POOL_HALO
    gate = lambda k: pl.BlockSpec((tm, D_MODEL), lambda i: (i, P_GATES // D_MODEL + k))
    return pl.pallas_call(
        functools.partial(_merge_body, seq=seq),
        grid=(t // tm,),
        in_specs=[
            pl.BlockSpec((tm, D_MODEL), lambda i: (i, 0)),
            pl.BlockSpec((tm, SSD_INNER), lambda i: (i, 0)),
            pl.BlockSpec((tm, SB_WIDTH), lambda i: (i, 0)),
            pl.BlockSpec((tm, POOL_WIDTH), lambda i: (i, P_U // POOL_WIDTH)),
            pl.BlockSpec((POOL_HALO, POOL_WIDTH), lambda i: (jnp.maximum(i * hb - 1, 0), P_U // POOL_WIDTH)),
            gate(0), gate(1), gate(2),
            _const_spec((SSD_INNER, D_MODEL)),
            _const_spec((SB_WIDTH, D_MODEL)),
            _const_spec((POOL_WIDTH, D_MODEL)),
            _const_spec((D_MODEL, D_MODEL)),
            _const_spec((POOL_GROUPS, POOL_GROUP_DIM, POOL_GROUP_DIM)),
            _const_spec((1, POOL_WIDTH)),
        ],
        out_specs=pl.BlockSpec((tm, D_MODEL), lambda i: (i, 0)),
        out_shape=jax.ShapeDtypeStruct((t, D_MODEL), F32),
        scratch_shapes=[pltpu.VMEM((tm + POOL_HALO, POOL_WIDTH), F32)],
        compiler_params=_params("parallel"),
        name="merge",
    )(x2, y_ssd, y_sb, proj, proj, proj, proj, proj,
      w_br_ssd.astype(BF16), w_br_sb.astype(BF16), w_br_pool.astype(BF16), w_out.astype(BF16),
      pool_w.astype(BF16), pool_scale.reshape(1, -1).astype(F32))


def _swiglu_into(acc_ref, xb, wg_ref, wu_ref, wd_ref, width):
    for c in range(width // FFN_CHUNK):
        cs = slice(c * FFN_CHUNK, (c + 1) * FFN_CHUNK)
        hidden = _silu(_dot(xb, wg_ref[:, cs])) * _dot(xb, wu_ref[:, cs])
        acc_ref[...] += _dot(hidden.astype(BF16), wd_ref[cs, :])


def _ffn_body(x_ref, g_ref, wg_ref, wu_ref, wd_ref, o_ref):
    x = x_ref[...]
    o_ref[...] = x
    _swiglu_into(o_ref, (_rms(x) * g_ref[...]).astype(BF16), wg_ref, wu_ref, wd_ref, FFN_DENSE)


def _ffn_dense(x2, g, w_gate, w_up, w_down):
    t = x2.shape[0]
    tm = min(512, t)
    return pl.pallas_call(
        _ffn_body,
        grid=(t // tm,),
        in_specs=[
            pl.BlockSpec((tm, D_MODEL), lambda i: (i, 0)),
            _const_spec((1, D_MODEL)),
            _const_spec((D_MODEL, FFN_DENSE)),
            _const_spec((D_MODEL, FFN_DENSE)),
            _const_spec((FFN_DENSE, D_MODEL)),
        ],
        out_specs=pl.BlockSpec((tm, D_MODEL), lambda i: (i, 0)),
        out_shape=jax.ShapeDtypeStruct((t, D_MODEL), F32),
        compiler_params=_params("parallel"),
        name="ffn_dense",
    )(x2, g, w_gate.astype(BF16), w_up.astype(BF16), w_down.astype(BF16))


def _router_body(x_ref, g_ref, rw_ref, h_ref, idx_ref, wt_ref):
    h = _rms(x_ref[...]) * g_ref[...]
    h_ref[...] = h
    rw = rw_ref[...]
    h_hi = h.astype(BF16)
    h_lo = (h - h_hi.astype(F32)).astype(BF16)
    w_hi = rw.astype(BF16)
    w_lo = (rw - w_hi.astype(F32)).astype(BF16)
    logits = _dot(h_hi, w_hi) + (_dot(h_hi, w_lo) + _dot(h_lo, w_hi))
    lane = lax.broadcasted_iota(jnp.int32, logits.shape, 1)
    lane_f = lane.astype(F32)
    logits = jnp.where(lane < N_EXPERTS, logits, -jnp.inf)
    m1 = jnp.max(logits, axis=-1, keepdims=True)
    i1 = jnp.min(jnp.where(logits == m1, lane_f, float(LANES)), axis=-1, keepdims=True)
    rest = jnp.where(lane_f == i1, -jnp.inf, logits)
    m2 = jnp.max(rest, axis=-1, keepdims=True)
    i2 = jnp.min(jnp.where(rest == m2, lane_f, float(LANES)), axis=-1, keepdims=True)
    e = jnp.exp(m2 - m1)
    w1 = 1.0 / (1.0 + e)
    idx = jnp.where(lane == 0, i1, jnp.where(lane == 1, i2, 0.0))
    idx_ref[...] = idx.T[0:idx_ref.shape[0], :].astype(jnp.int32)
    wt_ref[...] = jnp.where(lane == 0, w1, jnp.where(lane == 1, e * w1, 0.0))


def _router(x2, g, router_w):
    t = x2.shape[0]
    tm = min(512, t)
    rw = jnp.pad(router_w.astype(F32), ((0, 0), (0, LANES - N_EXPERTS)))
    row = pl.BlockSpec((tm, D_MODEL), lambda i: (i, 0))
    small = pl.BlockSpec((tm, LANES), lambda i: (i, 0))
    return pl.pallas_call(
        _router_body,
        grid=(t // tm,),
        in_specs=[row, _const_spec((1, D_MODEL)), _const_spec((D_MODEL, LANES))],
        out_specs=[row, pl.BlockSpec((SUBLANES, tm), lambda i: (0, i)), small],
        out_shape=[jax.ShapeDtypeStruct((t, D_MODEL), F32),
                   jax.ShapeDtypeStruct((SUBLANES, t), jnp.int32),
                   jax.ShapeDtypeStruct((t, LANES), F32)],
        compiler_params=_params("parallel"),
        name="router",
    )(x2, g, rw)


def _dispatch_body(pends_ref, counts_ref, slot_ref, h_ref, xb_hbm, zero_ref, sem):
    tm = h_ref.shape[0]

    def zero_block(start):
        fill = pltpu.make_async_copy(zero_ref, xb_hbm.at[pl.ds(pl.multiple_of(start, MOE_BLOCK), MOE_BLOCK)], sem)
        fill.start()
        fill.wait()

    @pl.when(pl.program_id(0) == 0)
    def _():
        zero_ref[...] = jnp.zeros_like(zero_ref)
        for e in range(N_EXPERTS):
            pl.when(counts_ref[e] > 0)(functools.partial(zero_block, pends_ref[e] - MOE_BLOCK))
        for j in range(N_EXPERTS):
            start = pends_ref[N_EXPERTS - 1] + j * MOE_BLOCK
            pl.when(start < xb_hbm.shape[0])(functools.partial(zero_block, start))

    for k in range(TOP_K):
        for r in range(tm):
            pltpu.make_async_copy(h_ref.at[pl.ds(r, 1)], xb_hbm.at[pl.ds(slot_ref[0, 0, k * tm + r], 1)], sem).start()
    for k in range(TOP_K):
        pltpu.make_async_copy(h_ref, xb_hbm.at[pl.ds(0, tm)], sem).wait()


def _dispatch(h, slot_tiles, pends, counts, cap):
    t = h.shape[0]
    nt = slot_tiles.shape[0]
    tm = t // nt
    grid_spec = pltpu.PrefetchScalarGridSpec(
        num_scalar_prefetch=2,
        grid=(nt,),
        in_specs=[
            pl.BlockSpec((1, 1, TOP_K * tm), lambda i, pe, co: (i, 0, 0), memory_space=pltpu.SMEM),
            pl.BlockSpec((tm, D_MODEL), lambda i, pe, co: (i, 0)),
        ],
        out_specs=pl.BlockSpec(memory_space=pl.ANY),
        scratch_shapes=[pltpu.VMEM((MOE_BLOCK, D_MODEL), F32), pltpu.SemaphoreType.DMA],
    )
    return pl.pallas_call(
        _dispatch_body,
        grid_spec=grid_spec,
        out_shape=jax.ShapeDtypeStruct((cap, D_MODEL), F32),
        compiler_params=_params("arbitrary"),
        name="dispatch",
    )(pends.astype(jnp.int32), counts.astype(jnp.int32), slot_tiles, h)


def _experts_body(bexp_ref, nused_ref, xb_ref, wg_ref, wu_ref, wd_ref, o_ref):
    del bexp_ref
    o_ref[...] = jnp.zeros_like(o_ref)

    @pl.when(pl.program_id(0) < nused_ref[0])
    def _():
        _swiglu_into(o_ref, xb_ref[...].astype(BF16), wg_ref, wu_ref, wd_ref, FFN_EXPERT)


def _experts(xb, block_exp, n_used, e_gate, e_up, e_down):
    n_blocks = block_exp.shape[0]
    wspec = lambda shape: pl.BlockSpec((None,) + shape, lambda i, be, nu: (be[i], 0, 0))
    rows = pl.BlockSpec((MOE_BLOCK, D_MODEL), lambda i, be, nu: (i, 0))
    used_rows = pl.BlockSpec((MOE_BLOCK, D_MODEL), lambda i, be, nu: (jnp.minimum(i, nu[0] - 1), 0))
    grid_spec = pltpu.PrefetchScalarGridSpec(
        num_scalar_prefetch=2,
        grid=(n_blocks,),
        in_specs=[used_rows, wspec((D_MODEL, FFN_EXPERT)), wspec((D_MODEL, FFN_EXPERT)),
                  wspec((FFN_EXPERT, D_MODEL))],
        out_specs=rows,
    )
    return pl.pallas_call(
        _experts_body,
        grid_spec=grid_spec,
        out_shape=jax.ShapeDtypeStruct((n_blocks * MOE_BLOCK, D_MODEL), F32),
        compiler_params=_params("arbitrary"),
        name="experts",
    )(block_exp, n_used, xb, e_gate.astype(BF16), e_up.astype(BF16), e_down.astype(BF16))


def _combine_body(slot_ref, next_ref, x_ref, wt_ref, yb_hbm, o_ref, buf_ref, sem):
    tm = x_ref.shape[0]
    n = TOP_K * tm
    i = pl.program_id(0)

    def fetch(idx_ref, b):
        for r in range(n):
            pltpu.make_async_copy(yb_hbm.at[pl.ds(idx_ref[0, 0, r], 1)], buf_ref.at[b, pl.ds(r, 1)], sem.at[b]).start()

    @pl.when(i == 0)
    def _():
        fetch(slot_ref, 0)

    for cur in range(2):
        @pl.when(i % 2 == cur)
        def _():
            @pl.when(i + 1 < pl.num_programs(0))
            def _():
                fetch(next_ref, 1 - cur)

            pltpu.make_async_copy(yb_hbm.at[pl.ds(0, n)], buf_ref.at[cur], sem.at[cur]).wait()
            wt = wt_ref[...]
            o_ref[...] = x_ref[...] + (wt[:, 0:1] * buf_ref[cur, 0:tm, :] + wt[:, 1:2] * buf_ref[cur, tm:n, :])


def _combine(x2, wt, slot_tiles, yb):
    t = x2.shape[0]
    nt = slot_tiles.shape[0]
    tm = t // nt
    slots = lambda shift: pl.BlockSpec((1, 1, TOP_K * tm), lambda i: (jnp.minimum(i + shift, nt - 1), 0, 0),
                                       memory_space=pltpu.SMEM)
    return pl.pallas_call(
        _combine_body,
        grid=(nt,),
        in_specs=[
            slots(0), slots(1),
            pl.BlockSpec((tm, D_MODEL), lambda i: (i, 0)),
            pl.BlockSpec((tm, LANES), lambda i: (i, 0)),
            pl.BlockSpec(memory_space=pl.ANY),
        ],
        out_specs=pl.BlockSpec((tm, D_MODEL), lambda i: (i, 0)),
        out_shape=jax.ShapeDtypeStruct((t, D_MODEL), F32),
        scratch_shapes=[pltpu.VMEM((2, TOP_K * tm, D_MODEL), F32), pltpu.SemaphoreType.DMA((2,))],
        compiler_params=_params("arbitrary"),
        name="combine",
    )(slot_tiles, slot_tiles, x2, wt, yb)


def _moe(x2, g, router_w, e_gate, e_up, e_down):
    t = x2.shape[0]
    h, idx, wt = _router(x2, g, router_w)
    n_assign = t * TOP_K
    cap = -(-n_assign // MOE_BLOCK) * MOE_BLOCK + N_EXPERTS * MOE_BLOCK
    n_blocks = cap // MOE_BLOCK
    experts = jnp.arange(N_EXPERTS, dtype=jnp.int32)[:, None]
    onehot = [(idx[k][None, :] == experts).astype(jnp.int32) for k in range(TOP_K)]
    running = [jnp.cumsum(oh, axis=1) for oh in onehot]
    totals = [run[:, -1] for run in running]
    counts = sum(totals)
    padded = (counts + MOE_BLOCK - 1) // MOE_BLOCK * MOE_BLOCK
    pends = jnp.cumsum(padded)
    pstarts = pends - padded
    dest = []
    before = pstarts
    for k in range(TOP_K):
        dest.append(jnp.sum((before[:, None] + running[k] - 1) * onehot[k], axis=0).astype(jnp.int32))
        before = before + totals[k]
    block_exp = jnp.minimum(jnp.searchsorted(pends, jnp.arange(n_blocks) * MOE_BLOCK, side="right"),
                            N_EXPERTS - 1).astype(jnp.int32)
    n_used = (pends[-1:] // MOE_BLOCK).astype(jnp.int32)
    def slot_tiles(tile):
        tm = min(tile, t)
        return jnp.concatenate([d.reshape(t // tm, tm) for d in dest], axis=1).reshape(t // tm, 1, TOP_K * tm)

    xb = _dispatch(h, slot_tiles(DISPATCH_TILE), pends, counts, cap)
    yb = _experts(xb, block_exp, n_used, e_gate, e_up, e_down)
    return _combine(x2, wt, slot_tiles(COMBINE_TILE), yb)


def _permute_w_in(w):
    w = w.astype(BF16)
    main = jnp.concatenate([w[:, COL_Z:COL_XBC], w[:, :COL_Z], w[:, COL_POOL:], w[:, COL_V:COL_POOL]], axis=1)
    qkv = w[:, COL_DT:COL_V]
    dt = jnp.pad(w[:, COL_XBC:COL_DT], ((0, 0), (0, LANES - SSD_HEADS)))
    return main, qkv, dt


def kernel(x, mix_norm_g, w_in, conv_w, conv_b, dt_bias, a_log, d_skip, ssd_norm_g, q_norm_g, k_norm_g, pool_w,
           pool_scale, w_br_ssd, w_br_sb, w_br_pool, w_out, ffn_norm_g, ffn_w_gate, ffn_w_up, ffn_w_down,
           router_w, moe_w_gate, moe_w_up, moe_w_down):
    bsz, seq, d = x.shape
    depth = w_in.shape[0]
    x2 = x.reshape(bsz * seq, d)
    rowvec = lambda v: v.reshape(1, -1).astype(F32)
    for layer in range(depth):
        w_main, w_qkv, w_dt = _permute_w_in(w_in[layer])
        proj, dt_raw, qn, kn, vn = _inproj(x2, rowvec(mix_norm_g[layer]), w_main, w_qkv, w_dt,
                                           rowvec(q_norm_g[layer]), rowvec(k_norm_g[layer]), bsz, seq)
        y_sb = _sb_attention(qn, kn, vn).reshape(bsz * seq, SB_WIDTH)
        y_ssd = _ssd(proj, dt_raw, conv_w[layer], conv_b[layer], dt_bias[layer], a_log[layer], d_skip[layer],
                     ssd_norm_g[layer], bsz, seq)
        x2 = _merge(x2, y_ssd, y_sb, proj, pool_w[layer], pool_scale[layer], w_br_ssd[layer], w_br_sb[layer],
                    w_br_pool[layer], w_out[layer], seq)
        i = layer // 2
        if layer % 2 == 0:
            x2 = _ffn_dense(x2, rowvec(ffn_norm_g[layer]), ffn_w_gate[i], ffn_w_up[i], ffn_w_down[i])
        else:
            x2 = _moe(x2, rowvec(ffn_norm_g[layer]), router_w[i], moe_w_gate[i], moe_w_up[i], moe_w_down[i])
    return x2.reshape(bsz, seq, d)
```

```python
import functools
import math

import jax
import jax.numpy as jnp
from jax import lax
from jax.experimental import pallas as pl
from jax.experimental.pallas import tpu as pltpu

F32 = jnp.float32
BF16 = jnp.bfloat16

D_MODEL = 1024
EPS = 1e-6

SSD_INNER = 1024
SSD_HEAD_DIM = 64
SSD_HEADS = 16
SSD_GROUPS = 4
SSD_HEADS_PER_GROUP = 4
SSD_STATE = 128
SSD_CONV = 4
SSD_CHUNK = 128
SSD_CONV_DIM = SSD_INNER + 2 * SSD_GROUPS * SSD_STATE
SSD_CONV_TAIL = 16
SSD_CONV_PIECE = 512

SB_HEADS = 4
SB_HEAD_DIM = 128
SB_WIDTH = SB_HEADS * SB_HEAD_DIM
SB_TILE = 256
SB_CHAINS = 32

POOL_WINDOWS = (2, 4, 8, 16)
POOL_GROUPS = 4
POOL_WIDTH = 512
POOL_GROUP_DIM = 128
POOL_HALO = 16

N_BRANCHES = 3
FFN_DENSE = 2816
N_EXPERTS = 8
TOP_K = 2
FFN_EXPERT = 1792
MOE_BLOCK = 512
DISPATCH_TILE = 2048
COMBINE_TILE = 256

COL_Z = SSD_INNER
COL_XBC = COL_Z + SSD_CONV_DIM
COL_DT = COL_XBC + SSD_HEADS
COL_Q = COL_DT + SB_WIDTH
COL_K = COL_Q + SB_WIDTH
COL_V = COL_K + SB_WIDTH
COL_POOL = COL_V + POOL_WIDTH

P_XBC = 0
P_Z = 2048
P_GATES = 3072
P_U = 6144
P_WIDTH = 6656

PROJ_DTYPE = BF16
LANES = 128
SUBLANES = 8
VMEM_LIMIT = 56 * 1024 * 1024
FFN_CHUNK = 256
INPROJ_CHUNK = 1024
MERGE_CHUNK = 256
SB_STOP = -110.0


def _params(*sem):
    return pltpu.CompilerParams(dimension_semantics=sem, vmem_limit_bytes=VMEM_LIMIT)


def _const_spec(shape):
    nd = len(shape)
    return pl.BlockSpec(shape, lambda *_: (0,) * nd, pipeline_mode=pl.Buffered(1))


def _split3(x):
    a = x.astype(BF16)
    r = x - a.astype(F32)
    b = r.astype(BF16)
    c = (r - b.astype(F32)).astype(BF16)
    return a, b, c


def _dot(a, b):
    return jnp.dot(a, b, preferred_element_type=F32)


def _dot_nt(a, b):
    return lax.dot_general(a, b, (((1,), (1,)), ((), ())), preferred_element_type=F32)


def _dot_f32_by_01(x, e01):
    a, b, c = _split3(x)
    return _dot(a, e01) + _dot(b, e01) + _dot(c, e01)


def _dot_01_by_f32(t01, x):
    a, b, c = _split3(x)
    return _dot(t01, a) + _dot(t01, b) + _dot(t01, c)


def _rms(x):
    return x * lax.rsqrt(jnp.mean(x * x, axis=-1, keepdims=True) + EPS)


def _silu(x):
    return x * jax.nn.sigmoid(x)


def _softplus(x):
    return jnp.maximum(x, 0.0) + jnp.log1p(jnp.exp(-jnp.abs(x)))


def _inproj_body(x_ref, g_ref, w_ref, wqkv_ref, wdt_ref, qg_ref, kg_ref, o_ref, dt_ref, qo_ref, ko_ref, vo_ref):
    xb = (_rms(x_ref[...]) * g_ref[...]).astype(BF16)
    dt_ref[...] = _dot(xb, wdt_ref[...])
    for lo in range(0, P_WIDTH, INPROJ_CHUNK):
        cs = slice(lo, min(lo + INPROJ_CHUNK, P_WIDTH))
        o_ref[:, cs] = _dot(xb, w_ref[:, cs]).astype(o_ref.dtype)
    scale = 1.0 / math.sqrt(SB_HEAD_DIM)
    q = _dot(xb, wqkv_ref[:, 0:SB_WIDTH])
    k = _dot(xb, wqkv_ref[:, SB_WIDTH:2 * SB_WIDTH])
    v = _dot(xb, wqkv_ref[:, 2 * SB_WIDTH:3 * SB_WIDTH])
    for h in range(SB_HEADS):
        sl = slice(h * SB_HEAD_DIM, (h + 1) * SB_HEAD_DIM)
        qo_ref[0, h] = ((_rms(q[:, sl]) * qg_ref[...]) * scale).astype(BF16)
        ko_ref[0, h] = (_rms(k[:, sl]) * kg_ref[...]).astype(BF16)
        vo_ref[0, h] = v[:, sl].astype(BF16)


def _inproj(x2, g, w_main, w_qkv, w_dt, qg, kg, bsz, seq):
    t = x2.shape[0]
    tm = min(512, seq)
    ns = seq // tm
    head_spec = pl.BlockSpec((1, SB_HEADS, tm, SB_HEAD_DIM), lambda i: (i // ns, 0, i % ns, 0))
    head_shape = jax.ShapeDtypeStruct((bsz, SB_HEADS, seq, SB_HEAD_DIM), BF16)
    return pl.pallas_call(
        _inproj_body,
        grid=(t // tm,),
        in_specs=[
            pl.BlockSpec((tm, D_MODEL), lambda i: (i, 0)),
            _const_spec((1, D_MODEL)),
            _const_spec((D_MODEL, P_WIDTH)),
            _const_spec((D_MODEL, 3 * SB_WIDTH)),
            _const_spec((D_MODEL, LANES)),
            _const_spec((1, SB_HEAD_DIM)),
            _const_spec((1, SB_HEAD_DIM)),
        ],
        out_specs=[
            pl.BlockSpec((tm, P_WIDTH), lambda i: (i, 0)),
            pl.BlockSpec((tm, LANES), lambda i: (i, 0)),
            head_spec, head_spec, head_spec,
        ],
        out_shape=[
            jax.ShapeDtypeStruct((t, P_WIDTH), PROJ_DTYPE),
            jax.ShapeDtypeStruct((t, LANES), F32),
            head_shape, head_shape, head_shape,
        ],
        compiler_params=_params("parallel"),
        name="inproj",
    )(x2, g, w_main, w_qkv, w_dt, qg, kg)


def _sb_body(q_ref, k_ref, v_ref, uo_ref, o_ref, carry_ref, acc_ref):
    t = SB_TILE
    first = pl.program_id(2) * SB_CHAINS
    uo = uo_ref[...]
    row = lax.broadcasted_iota(jnp.int32, (t, t), 0)
    col = lax.broadcasted_iota(jnp.int32, (t, t), 1)
    mask = col < row

    def visit(c, kb, diag):
        start = pl.multiple_of(jnp.maximum(kb, 0) * t, t)
        q = q_ref[0, 0, c * t:(c + 1) * t, :]
        k = k_ref[0, 0, pl.ds(start, t), :]
        v = v_ref[0, 0, pl.ds(start, t), :]
        z = _dot_nt(q, k)
        neg_log_rest = jnp.maximum(z, 0.0) + jnp.log(1.0 + jnp.exp(-jnp.abs(z)))
        log_beta = z - neg_log_rest
        if diag:
            neg_log_rest = jnp.where(mask, neg_log_rest, 0.0)
        sr = _dot(neg_log_rest.astype(BF16), uo)
        if diag:
            w = jnp.where(mask, jnp.exp(log_beta - sr[:, :t]), 0.0)
            carry = -sr[:, t:]
            acc_ref[c] = _dot(w.astype(BF16), v)
        else:
            prev = carry_ref[c]
            w = jnp.exp(log_beta - sr[:, :t] + jnp.concatenate([prev] * (t // LANES), axis=1))
            w = jnp.where(kb >= 0, w, 0.0)
            carry = prev - sr[:, t:]
            acc_ref[c] += _dot(w.astype(BF16), v)
        carry_ref[c] = carry
        return jnp.max(carry)

    def unfinished(j, maxima):
        need = [jnp.logical_and(first + c - j >= 0, maxima[c] > SB_STOP) for c in range(SB_CHAINS)]
        return functools.reduce(jnp.logical_or, need).astype(jnp.int32)

    maxima = [visit(c, first + c, True) for c in range(SB_CHAINS)]

    def body(state):
        j, _ = state
        maxima = [visit(c, first + c - j, False) for c in range(SB_CHAINS)]
        return j + 1, unfinished(j + 1, maxima)

    lax.while_loop(lambda state: state[1] > 0, body, (jnp.int32(1), unfinished(1, maxima)))
    for c in range(SB_CHAINS):
        o_ref[0, c * t:(c + 1) * t, :] = acc_ref[c].astype(o_ref.dtype)


def _sb_attention(qn, kn, vn):
    bsz, _, seq, _ = qn.shape
    t = SB_TILE
    tq = t * SB_CHAINS
    assert seq % tq == 0
    r = jnp.arange(t)
    upper = (r[:, None] > r[None, :])
    uo = jnp.concatenate([upper, jnp.ones((t, LANES), bool)], axis=1).astype(BF16)
    kv_spec = pl.BlockSpec((1, 1, seq, SB_HEAD_DIM), lambda b, h, i: (b, h, 0, 0))
    return pl.pallas_call(
        _sb_body,
        grid=(bsz, SB_HEADS, seq // tq),
        in_specs=[pl.BlockSpec((1, 1, tq, SB_HEAD_DIM), lambda b, h, i: (b, h, i, 0)),
                  kv_spec, kv_spec, _const_spec((t, t + LANES))],
        out_specs=pl.BlockSpec((1, tq, SB_HEAD_DIM), lambda b, h, i: (b, i, h)),
        out_shape=jax.ShapeDtypeStruct((bsz, seq, SB_WIDTH), BF16),
        scratch_shapes=[pltpu.VMEM((SB_CHAINS, t, LANES), F32), pltpu.VMEM((SB_CHAINS, t, SB_HEAD_DIM), F32)],
        compiler_params=_params("parallel", "parallel", "arbitrary"),
        name="sb_attention",
    )(qn, kn, vn, uo)


def _ssd_body(xbcs_ref, zs_ref, dts_ref, cw_ref, cb_ref, dtb_ref, alog_ref, dskip_ref, ng_ref, e_ref, tri_ref,
              shift_ref, os_ref, bufs_ref, sts_ref, ys_ref):
    tail = SSD_CONV_TAIL

    @pl.when(pl.program_id(0) == 0)
    def _():
        bufs_ref[:, 0:tail, :] = jnp.zeros((bufs_ref.shape[0], tail, SSD_CONV_DIM), bufs_ref.dtype)
        sts_ref[...] = jnp.zeros_like(sts_ref)

    for b in range(xbcs_ref.shape[0]):
        _ssd_chunk(xbcs_ref.at[b], zs_ref.at[b], dts_ref.at[b], cw_ref, cb_ref, dtb_ref, alog_ref, dskip_ref, ng_ref,
                   e_ref, tri_ref, shift_ref, os_ref.at[b], bufs_ref.at[b], sts_ref.at[b], ys_ref.at[b])


def _ssd_chunk(xbc_ref, z_ref, dt_ref, cw_ref, cb_ref, dtb_ref, alog_ref, dskip_ref, ng_ref, e_ref, tri_ref,
               shift_ref, o_ref, buf_ref, st_ref, y_ref):
    L = SSD_CHUNK
    G, R, P, N = SSD_GROUPS, SSD_HEADS_PER_GROUP, SSD_HEAD_DIM, SSD_STATE
    tail = SSD_CONV_TAIL
    buf_ref[tail:tail + L, :] = xbc_ref[...]

    def conv_silu_piece(lo, hi):
        delayed = _dot(shift_ref[...], buf_ref[:, lo:hi])
        acc = cb_ref[:, lo:hi] + cw_ref[SSD_CONV - 1:SSD_CONV, lo:hi] * xbc_ref[:, lo:hi].astype(F32)
        for k in range(SSD_CONV - 1):
            acc = acc + cw_ref[k:k + 1, lo:hi] * delayed[k * L:(k + 1) * L, :]
        return _silu(acc)

    def conv_silu(lo, hi):
        step = SSD_CONV_PIECE
        return jnp.concatenate([conv_silu_piece(c, c + step) for c in range(lo, hi, step)], axis=1)

    xs = conv_silu(0, SSD_INNER)
    b_in = conv_silu(SSD_INNER, SSD_INNER + G * N)
    c_in = conv_silu(SSD_INNER + G * N, SSD_CONV_DIM)
    buf_ref[0:tail, :] = buf_ref[L:L + tail, :]

    e01 = e_ref[...]
    dt = _softplus(dt_ref[...] + dtb_ref[...])
    da = dt * (-jnp.exp(alog_ref[...]))
    a_cs = _dot_01_by_f32(tri_ref[...], da)
    a_cs_t = a_cs.T
    a_full = _dot_f32_by_01(a_cs, e01)
    dt_full = _dot_f32_by_01(dt, e01)
    a_last = a_full[L - 1:L, :]
    ea_full = jnp.exp(a_full)
    x_dt = xs * dt_full
    xw = (x_dt * jnp.exp(a_last - a_full)).astype(BF16)
    ea_last = jnp.exp(a_last)

    row = lax.broadcasted_iota(jnp.int32, (L, L), 0)
    col = lax.broadcasted_iota(jnp.int32, (L, L), 1)
    causal = col <= row
    lane = lax.broadcasted_iota(jnp.int32, (L, LANES), 1)
    first_head = lane < P

    for g in range(G):
        gs = slice(g * N, (g + 1) * N)
        cg = c_in[:, gs].astype(BF16)
        bg_f32 = b_in[:, gs]
        cb = _dot_nt(cg, bg_f32.astype(BF16))
        for pair in range(R // 2):
            ms = []
            for r in range(2):
                h = g * R + pair * 2 + r
                seg = a_cs[:, h:h + 1] - a_cs_t[h:h + 1, :]
                decay = jnp.exp(jnp.where(causal, seg, -jnp.inf))
                ms.append((cb * decay).astype(BF16))
            c0 = (g * R + pair * 2) * P
            xp = x_dt[:, c0:c0 + LANES]
            rhs = jnp.concatenate([jnp.where(first_head, xp, 0.0).astype(BF16),
                                   jnp.where(first_head, 0.0, xp).astype(BF16)], axis=0)
            y_ref[:, c0:c0 + LANES] = _dot(jnp.concatenate(ms, axis=1), rhs)
        cs = slice(g * R * P, (g + 1) * R * P)
        state = st_ref[g]
        y_ref[:, cs] += _dot(cg, state.astype(BF16)) * ea_full[:, cs]
        st_ref[g] = state * ea_last[:, cs] + _dot(bg_f32.T.astype(BF16), xw[:, cs])

    y = y_ref[...] + xs * dskip_ref[...]
    y = y * _silu(z_ref[...].astype(F32))
    gw = SSD_INNER // G
    for g in range(G):
        cs = slice(g * gw, (g + 1) * gw)
        o_ref[:, cs] = (_rms(y[:, cs]) * ng_ref[:, cs]).astype(o_ref.dtype)


def _ssd(proj, dt_raw, conv_w, conv_b, dt_bias, a_log, d_skip, norm_g, bsz, seq):
    L = SSD_CHUNK
    nc = seq // L
    t = bsz * seq
    pad = LANES - SSD_HEADS
    heads = jnp.arange(LANES)
    cols = jnp.arange(SSD_INNER) // SSD_HEAD_DIM
    e01 = (heads[:, None] == cols[None, :]).astype(BF16)
    r = jnp.arange(L)
    tri = (r[None, :] <= r[:, None]).astype(BF16)
    assert proj.dtype == BF16, "the 0/1 shift product is exact only on bf16 data"
    delay = (SSD_CONV - 1) - jnp.arange((SSD_CONV - 1) * L) // L
    src = SSD_CONV_TAIL + jnp.arange((SSD_CONV - 1) * L) % L - delay
    shift = (src[:, None] == jnp.arange(SSD_CONV_TAIL + L)[None, :]).astype(BF16)
    rowvec = lambda v: v.reshape(1, -1).astype(F32)
    proj3 = proj.reshape(bsz, seq, proj.shape[-1])
    return pl.pallas_call(
        _ssd_body,
        grid=(nc,),
        in_specs=[
            pl.BlockSpec((bsz, L, SSD_CONV_DIM), lambda c: (0, c, P_XBC // SSD_CONV_DIM)),
            pl.BlockSpec((bsz, L, SSD_INNER), lambda c: (0, c, P_Z // SSD_INNER)),
            pl.BlockSpec((bsz, L, LANES), lambda c: (0, c, 0)),
            _const_spec((SSD_CONV, SSD_CONV_DIM)),
            _const_spec((1, SSD_CONV_DIM)),
            _const_spec((1, LANES)),
            _const_spec((1, LANES)),
            _const_spec((1, SSD_INNER)),
            _const_spec((1, SSD_INNER)),
            _const_spec((LANES, SSD_INNER)),
            _const_spec((L, L)),
            _const_spec(((SSD_CONV - 1) * L, SSD_CONV_TAIL + L)),
        ],
        out_specs=pl.BlockSpec((bsz, L, SSD_INNER), lambda c: (0, c, 0)),
        out_shape=jax.ShapeDtypeStruct((bsz, seq, SSD_INNER), BF16),
        scratch_shapes=[
            pltpu.VMEM((bsz, SSD_CONV_TAIL + L, SSD_CONV_DIM), BF16),
            pltpu.VMEM((bsz, SSD_GROUPS, SSD_STATE, SSD_HEADS_PER_GROUP * SSD_HEAD_DIM), F32),
            pltpu.VMEM((bsz, L, SSD_INNER), F32),
        ],
        compiler_params=_params("arbitrary"),
        name="ssd",
    )(proj3, proj3, dt_raw.reshape(bsz, seq, LANES), conv_w.astype(F32), rowvec(conv_b),
      jnp.pad(rowvec(dt_bias), ((0, 0), (0, pad))), jnp.pad(rowvec(a_log), ((0, 0), (0, pad))),
      rowvec(jnp.repeat(d_skip, SSD_HEAD_DIM)), rowvec(norm_g), e01, tri, shift).reshape(t, SSD_INNER)


def _merge_body(x_ref, yssd_ref, ysb_ref, u_ref, halo_ref, g0_ref, g1_ref, g2_ref,
                wssd_ref, wsb_ref, wpool_ref, wout_ref, pw_ref, ps_ref, o_ref, ext_ref, *, seq):
    tm = x_ref.shape[0]
    H = POOL_HALO
    start = (pl.program_id(0) * tm) % seq
    halo = halo_ref[...].astype(F32)
    ext_ref[0:H, :] = jnp.where(start == 0, jnp.zeros_like(halo), halo)
    ext_ref[H:H + tm, :] = u_ref[...].astype(F32)
    pos = start + lax.broadcasted_iota(jnp.int32, (tm, 1), 0)

    mixed = []
    for gi, win in enumerate(POOL_WINDOWS):
        cs = slice(gi * POOL_GROUP_DIM, (gi + 1) * POOL_GROUP_DIM)
        cur = ext_ref[H:H + tm, cs]
        wsum = cur
        for k in range(1, win):
            wsum = wsum + ext_ref[H - k:H - k + tm, cs]
        count = jnp.minimum(pos + 1, win).astype(F32)
        pooled = wsum / count - cur
        mixed.append((_dot(pooled.astype(BF16), pw_ref[gi]) * ps_ref[:, cs]).astype(BF16))
    ypool = jnp.concatenate(mixed, axis=1)
    yssd = yssd_ref[...]
    ysb = ysb_ref[...]
    merged = []
    for lo in range(0, D_MODEL, MERGE_CHUNK):
        cs = slice(lo, lo + MERGE_CHUNK)
        part = jax.nn.sigmoid(g0_ref[:, cs].astype(F32)) * _dot(yssd, wssd_ref[:, cs])
        part += jax.nn.sigmoid(g1_ref[:, cs].astype(F32)) * _dot(ysb, wsb_ref[:, cs])
        part += jax.nn.sigmoid(g2_ref[:, cs].astype(F32)) * _dot(ypool, wpool_ref[:, cs])
        merged.append(part.astype(BF16))
    o_ref[...] = x_ref[...] + _dot(jnp.concatenate(merged, axis=1), wout_ref[...])


def _merge(x2, y_ssd, y_sb, proj, pool_w, pool_scale, w_br_ssd, w_br_sb, w_br_pool, w_out, seq):
    t = x2.shape[0]
    tm = min(512, seq)
    hb = tm // POOL_HALO
    gate = lambda k: pl.BlockSpec((tm, D_MODEL), lambda i: (i, P_GATES // D_MODEL + k))
    return pl.pallas_call(
        functools.partial(_merge_body, seq=seq),
        grid=(t // tm,),
        in_specs=[
            pl.BlockSpec((tm, D_MODEL), lambda i: (i, 0)),
            pl.BlockSpec((tm, SSD_INNER), lambda i: (i, 0)),
            pl.BlockSpec((tm, SB_WIDTH), lambda i: (i, 0)),
            pl.BlockSpec((tm, POOL_WIDTH), lambda i: (i, P_U // POOL_WIDTH)),
            pl.BlockSpec((POOL_HALO, POOL_WIDTH), lambda i: (jnp.maximum(i * hb - 1, 0), P_U // POOL_WIDTH)),
            gate(0), gate(1), gate(2),
            _const_spec((SSD_INNER, D_MODEL)),
            _const_spec((SB_WIDTH, D_MODEL)),
            _const_spec((POOL_WIDTH, D_MODEL)),
            _const_spec((D_MODEL, D_MODEL)),
            _const_spec((POOL_GROUPS, POOL_GROUP_DIM, POOL_GROUP_DIM)),
            _const_spec((1, POOL_WIDTH)),
        ],
        out_specs=pl.BlockSpec((tm, D_MODEL), lambda i: (i, 0)),
        out_shape=jax.ShapeDtypeStruct((t, D_MODEL), F32),
        scratch_shapes=[pltpu.VMEM((tm + POOL_HALO, POOL_WIDTH), F32)],
        compiler_params=_params("parallel"),
        name="merge",
    )(x2, y_ssd, y_sb, proj, proj, proj, proj, proj,
      w_br_ssd.astype(BF16), w_br_sb.astype(BF16), w_br_pool.astype(BF16), w_out.astype(BF16),
      pool_w.astype(BF16), pool_scale.reshape(1, -1).astype(F32))


def _swiglu_into(acc_ref, xb, wg_ref, wu_ref, wd_ref, width):
    for c in range(width // FFN_CHUNK):
        cs = slice(c * FFN_CHUNK, (c + 1) * FFN_CHUNK)
        hidden = _silu(_dot(xb, wg_ref[:, cs])) * _dot(xb, wu_ref[:, cs])
        acc_ref[...] += _dot(hidden.astype(BF16), wd_ref[cs, :])


def _ffn_body(x_ref, g_ref, wg_ref, wu_ref, wd_ref, o_ref):
    x = x_ref[...]
    o_ref[...] = x
    _swiglu_into(o_ref, (_rms(x) * g_ref[...]).astype(BF16), wg_ref, wu_ref, wd_ref, FFN_DENSE)


def _ffn_dense(x2, g, w_gate, w_up, w_down):
    t = x2.shape[0]
    tm = min(512, t)
    return pl.pallas_call(
        _ffn_body,
        grid=(t // tm,),
        in_specs=[
            pl.BlockSpec((tm, D_MODEL), lambda i: (i, 0)),
            _const_spec((1, D_MODEL)),
            _const_spec((D_MODEL, FFN_DENSE)),
            _const_spec((D_MODEL, FFN_DENSE)),
            _const_spec((FFN_DENSE, D_MODEL)),
        ],
        out_specs=pl.BlockSpec((tm, D_MODEL), lambda i: (i, 0)),
        out_shape=jax.ShapeDtypeStruct((t, D_MODEL), F32),
        compiler_params=_params("parallel"),
        name="ffn_dense",
    )(x2, g, w_gate.astype(BF16), w_up.astype(BF16), w_down.astype(BF16))


def _router_body(x_ref, g_ref, rw_ref, h_ref, idx_ref, wt_ref):
    h = _rms(x_ref[...]) * g_ref[...]
    h_ref[...] = h
    rw = rw_ref[...]
    h_hi = h.astype(BF16)
    h_lo = (h - h_hi.astype(F32)).astype(BF16)
    w_hi = rw.astype(BF16)
    w_lo = (rw - w_hi.astype(F32)).astype(BF16)
    logits = _dot(h_hi, w_hi) + (_dot(h_hi, w_lo) + _dot(h_lo, w_hi))
    lane = lax.broadcasted_iota(jnp.int32, logits.shape, 1)
    lane_f = lane.astype(F32)
    logits = jnp.where(lane < N_EXPERTS, logits, -jnp.inf)
    m1 = jnp.max(logits, axis=-1, keepdims=True)
    i1 = jnp.min(jnp.where(logits == m1, lane_f, float(LANES)), axis=-1, keepdims=True)
    rest = jnp.where(lane_f == i1, -jnp.inf, logits)
    m2 = jnp.max(rest, axis=-1, keepdims=True)
    i2 = jnp.min(jnp.where(rest == m2, lane_f, float(LANES)), axis=-1, keepdims=True)
    e = jnp.exp(m2 - m1)
    w1 = 1.0 / (1.0 + e)
    idx = jnp.where(lane == 0, i1, jnp.where(lane == 1, i2, 0.0))
    idx_ref[...] = idx.T[0:idx_ref.shape[0], :].astype(jnp.int32)
    wt_ref[...] = jnp.where(lane == 0, w1, jnp.where(lane == 1, e * w1, 0.0))


def _router(x2, g, router_w):
    t = x2.shape[0]
    tm = min(512, t)
    rw = jnp.pad(router_w.astype(F32), ((0, 0), (0, LANES - N_EXPERTS)))
    row = pl.BlockSpec((tm, D_MODEL), lambda i: (i, 0))
    small = pl.BlockSpec((tm, LANES), lambda i: (i, 0))
    return pl.pallas_call(
        _router_body,
        grid=(t // tm,),
        in_specs=[row, _const_spec((1, D_MODEL)), _const_spec((D_MODEL, LANES))],
        out_specs=[row, pl.BlockSpec((SUBLANES, tm), lambda i: (0, i)), small],
        out_shape=[jax.ShapeDtypeStruct((t, D_MODEL), F32),
                   jax.ShapeDtypeStruct((SUBLANES, t), jnp.int32),
                   jax.ShapeDtypeStruct((t, LANES), F32)],
        compiler_params=_params("parallel"),
        name="router",
    )(x2, g, rw)


def _dispatch_body(pends_ref, counts_ref, slot_ref, h_ref, xb_hbm, zero_ref, sem):
    tm = h_ref.shape[0]

    def zero_block(start):
        fill = pltpu.make_async_copy(zero_ref, xb_hbm.at[pl.ds(pl.multiple_of(start, MOE_BLOCK), MOE_BLOCK)], sem)
        fill.start()
        fill.wait()

    @pl.when(pl.program_id(0) == 0)
    def _():
        zero_ref[...] = jnp.zeros_like(zero_ref)
        for e in range(N_EXPERTS):
            pl.when(counts_ref[e] > 0)(functools.partial(zero_block, pends_ref[e] - MOE_BLOCK))
        for j in range(N_EXPERTS):
            start = pends_ref[N_EXPERTS - 1] + j * MOE_BLOCK
            pl.when(start < xb_hbm.shape[0])(functools.partial(zero_block, start))

    for k in range(TOP_K):
        for r in range(tm):
            pltpu.make_async_copy(h_ref.at[pl.ds(r, 1)], xb_hbm.at[pl.ds(slot_ref[0, 0, k * tm + r], 1)], sem).start()
    for k in range(TOP_K):
        pltpu.make_async_copy(h_ref, xb_hbm.at[pl.ds(0, tm)], sem).wait()


def _dispatch(h, slot_tiles, pends, counts, cap):
    t = h.shape[0]
    nt = slot_tiles.shape[0]
    tm = t // nt
    grid_spec = pltpu.PrefetchScalarGridSpec(
        num_scalar_prefetch=2,
        grid=(nt,),
        in_specs=[
            pl.BlockSpec((1, 1, TOP_K * tm), lambda i, pe, co: (i, 0, 0), memory_space=pltpu.SMEM),
            pl.BlockSpec((tm, D_MODEL), lambda i, pe, co: (i, 0)),
        ],
        out_specs=pl.BlockSpec(memory_space=pl.ANY),
        scratch_shapes=[pltpu.VMEM((MOE_BLOCK, D_MODEL), F32), pltpu.SemaphoreType.DMA],
    )
    return pl.pallas_call(
        _dispatch_body,
        grid_spec=grid_spec,
        out_shape=jax.ShapeDtypeStruct((cap, D_MODEL), F32),
        compiler_params=_params("arbitrary"),
        name="dispatch",
    )(pends.astype(jnp.int32), counts.astype(jnp.int32), slot_tiles, h)


def _experts_body(bexp_ref, nused_ref, xb_ref, wg_ref, wu_ref, wd_ref, o_ref):
    del bexp_ref
    o_ref[...] = jnp.zeros_like(o_ref)

    @pl.when(pl.program_id(0) < nused_ref[0])
    def _():
        _swiglu_into(o_ref, xb_ref[...].astype(BF16), wg_ref, wu_ref, wd_ref, FFN_EXPERT)


def _experts(xb, block_exp, n_used, e_gate, e_up, e_down):
    n_blocks = block_exp.shape[0]
    wspec = lambda shape: pl.BlockSpec((None,) + shape, lambda i, be, nu: (be[i], 0, 0))
    rows = pl.BlockSpec((MOE_BLOCK, D_MODEL), lambda i, be, nu: (i, 0))
    used_rows = pl.BlockSpec((MOE_BLOCK, D_MODEL), lambda i, be, nu: (jnp.minimum(i, nu[0] - 1), 0))
    grid_spec = pltpu.PrefetchScalarGridSpec(
        num_scalar_prefetch=2,
        grid=(n_blocks,),
        in_specs=[used_rows, wspec((D_MODEL, FFN_EXPERT)), wspec((D_MODEL, FFN_EXPERT)),
                  wspec((FFN_EXPERT, D_MODEL))],
        out_specs=rows,
    )
    return pl.pallas_call(
        _experts_body,
        grid_spec=grid_spec,
        out_shape=jax.ShapeDtypeStruct((n_blocks * MOE_BLOCK, D_MODEL), F32),
        compiler_params=_params("arbitrary"),
        name="experts",
    )(block_exp, n_used, xb, e_gate.astype(BF16), e_up.astype(BF16), e_down.astype(BF16))


def _combine_body(slot_ref, next_ref, x_ref, wt_ref, yb_hbm, o_ref, buf_ref, sem):
    tm = x_ref.shape[0]
    n = TOP_K * tm
    i = pl.program_id(0)

    def fetch(idx_ref, b):
        for r in range(n):
            pltpu.make_async_copy(yb_hbm.at[pl.ds(idx_ref[0, 0, r], 1)], buf_ref.at[b, pl.ds(r, 1)], sem.at[b]).start()

    @pl.when(i == 0)
    def _():
        fetch(slot_ref, 0)

    for cur in range(2):
        @pl.when(i % 2 == cur)
        def _():
            @pl.when(i + 1 < pl.num_programs(0))
            def _():
                fetch(next_ref, 1 - cur)

            pltpu.make_async_copy(yb_hbm.at[pl.ds(0, n)], buf_ref.at[cur], sem.at[cur]).wait()
            wt = wt_ref[...]
            o_ref[...] = x_ref[...] + (wt[:, 0:1] * buf_ref[cur, 0:tm, :] + wt[:, 1:2] * buf_ref[cur, tm:n, :])


def _combine(x2, wt, slot_tiles, yb):
    t = x2.shape[0]
    nt = slot_tiles.shape[0]
    tm = t // nt
    slots = lambda shift: pl.BlockSpec((1, 1, TOP_K * tm), lambda i: (jnp.minimum(i + shift, nt - 1), 0, 0),
                                       memory_space=pltpu.SMEM)
    return pl.pallas_call(
        _combine_body,
        grid=(nt,),
        in_specs=[
            slots(0), slots(1),
            pl.BlockSpec((tm, D_MODEL), lambda i: (i, 0)),
            pl.BlockSpec((tm, LANES), lambda i: (i, 0)),
            pl.BlockSpec(memory_space=pl.ANY),
        ],
        out_specs=pl.BlockSpec((tm, D_MODEL), lambda i: (i, 0)),
        out_shape=jax.ShapeDtypeStruct((t, D_MODEL), F32),
        scratch_shapes=[pltpu.VMEM((2, TOP_K * tm, D_MODEL), F32), pltpu.SemaphoreType.DMA((2,))],
        compiler_params=_params("arbitrary"),
        name="combine",
    )(slot_tiles, slot_tiles, x2, wt, yb)


def _moe(x2, g, router_w, e_gate, e_up, e_down):
    t = x2.shape[0]
    h, idx, wt = _router(x2, g, router_w)
    n_assign = t * TOP_K
    cap = -(-n_assign // MOE_BLOCK) * MOE_BLOCK + N_EXPERTS * MOE_BLOCK
    n_blocks = cap // MOE_BLOCK
    expert_flat = idx[:TOP_K].reshape(-1)
    onehot = (expert_flat[None, :] == jnp.arange(N_EXPERTS, dtype=jnp.int32)[:, None]).astype(jnp.int32)
    running = jnp.cumsum(onehot, axis=1)
    counts = running[:, -1]
    rank = jnp.sum(running * onehot, axis=0) - 1
    padded = (counts + MOE_BLOCK - 1) // MOE_BLOCK * MOE_BLOCK
    pends = jnp.cumsum(padded)
    pstarts = pends - padded
    dest = (jnp.sum(pstarts[:, None] * onehot, axis=0) + rank).astype(jnp.int32)
    block_exp = jnp.minimum(jnp.searchsorted(pends, jnp.arange(n_blocks) * MOE_BLOCK, side="right"),
                            N_EXPERTS - 1).astype(jnp.int32)
    n_used = (pends[-1:] // MOE_BLOCK).astype(jnp.int32)
    def slot_tiles(tile):
        tm = min(tile, t)
        return dest.reshape(TOP_K, t // tm, tm).transpose(1, 0, 2).reshape(t // tm, 1, TOP_K * tm)

    xb = _dispatch(h, slot_tiles(DISPATCH_TILE), pends, counts, cap)
    yb = _experts(xb, block_exp, n_used, e_gate, e_up, e_down)
    return _combine(x2, wt, slot_tiles(COMBINE_TILE), yb)


def _permute_w_in(w):
    w = w.astype(BF16)
    main = jnp.concatenate([w[:, COL_Z:COL_XBC], w[:, :COL_Z], w[:, COL_POOL:], w[:, COL_V:COL_POOL]], axis=1)
    qkv = w[:, COL_DT:COL_V]
    dt = jnp.pad(w[:, COL_XBC:COL_DT], ((0, 0), (0, LANES - SSD_HEADS)))
    return main, qkv, dt


def kernel(x, mix_norm_g, w_in, conv_w, conv_b, dt_bias, a_log, d_skip, ssd_norm_g, q_norm_g, k_norm_g, pool_w,
           pool_scale, w_br_ssd, w_br_sb, w_br_pool, w_out, ffn_norm_g, ffn_w_gate, ffn_w_up, ffn_w_down,
           router_w, moe_w_gate, moe_w_up, moe_w_down):
    bsz, seq, d = x.shape
    depth = w_in.shape[0]
    x2 = x.reshape(bsz * seq, d)
    rowvec = lambda v: v.reshape(1, -1).astype(F32)
    for layer in range(depth):
        w_main, w_qkv, w_dt = _permute_w_in(w_in[layer])
        proj, dt_raw, qn, kn, vn = _inproj(x2, rowvec(mix_norm_g[layer]), w_main, w_qkv, w_dt,
                                           rowvec(q_norm_g[layer]), rowvec(k_norm_g[layer]), bsz, seq)
        y_sb = _sb_attention(qn, kn, vn).reshape(bsz * seq, SB_WIDTH)
        y_ssd = _ssd(proj, dt_raw, conv_w[layer], conv_b[layer], dt_bias[layer], a_log[layer], d_skip[layer],
                     ssd_norm_g[layer], bsz, seq)
        x2 = _merge(x2, y_ssd, y_sb, proj, pool_w[layer], pool_scale[layer], w_br_ssd[layer], w_br_sb[layer],
                    w_br_pool[layer], w_out[layer], seq)
        i = layer // 2
        if layer % 2 == 0:
            x2 = _ffn_dense(x2, rowvec(ffn_norm_g[layer]), ffn_w_gate[i], ffn_w_up[i], ffn_w_down[i])
        else:
            x2 = _moe(x2, rowvec(ffn_norm_g[layer]), router_w[i], moe_w_gate[i], moe_w_up[i], moe_w_down[i])
    return x2.reshape(bsz, seq, d)
```

```python
import functools
import math

import jax
import jax.numpy as jnp
from jax import lax
from jax.experimental import pallas as pl
from jax.experimental.pallas import tpu as pltpu

F32 = jnp.float32
BF16 = jnp.bfloat16

D_MODEL = 1024
EPS = 1e-6

SSD_INNER = 1024
SSD_HEAD_DIM = 64
SSD_HEADS = 16
SSD_GROUPS = 4
SSD_HEADS_PER_GROUP = 4
SSD_STATE = 128
SSD_CONV = 4
SSD_CHUNK = 128
SSD_CONV_DIM = SSD_INNER + 2 * SSD_GROUPS * SSD_STATE
SSD_CONV_TAIL = 16
SSD_CONV_PIECE = 512

SB_HEADS = 4
SB_HEAD_DIM = 128
SB_WIDTH = SB_HEADS * SB_HEAD_DIM
SB_TILE = 256
SB_CHAINS = 32

POOL_WINDOWS = (2, 4, 8, 16)
POOL_GROUPS = 4
POOL_WIDTH = 512
POOL_GROUP_DIM = 128
POOL_HALO = 16

N_BRANCHES = 3
FFN_DENSE = 2816
N_EXPERTS = 8
TOP_K = 2
FFN_EXPERT = 1792
MOE_BLOCK = 512
DISPATCH_TILE = 2048
COMBINE_TILE = 256

COL_Z = SSD_INNER
COL_XBC = COL_Z + SSD_CONV_DIM
COL_DT = COL_XBC + SSD_HEADS
COL_Q = COL_DT + SB_WIDTH
COL_K = COL_Q + SB_WIDTH
COL_V = COL_K + SB_WIDTH
COL_POOL = COL_V + POOL_WIDTH

P_XBC = 0
P_Z = 2048
P_GATES = 3072
P_U = 6144
P_WIDTH = 6656

PROJ_DTYPE = BF16
LANES = 128
SUBLANES = 8
VMEM_LIMIT = 56 * 1024 * 1024
FFN_CHUNK = 256
INPROJ_CHUNK = 1024
MERGE_CHUNK = 256
SB_STOP = -110.0


def _params(*sem):
    return pltpu.CompilerParams(dimension_semantics=sem, vmem_limit_bytes=VMEM_LIMIT)


def _const_spec(shape):
    nd = len(shape)
    return pl.BlockSpec(shape, lambda *_: (0,) * nd, pipeline_mode=pl.Buffered(1))


def _split3(x):
    a = x.astype(BF16)
    r = x - a.astype(F32)
    b = r.astype(BF16)
    c = (r - b.astype(F32)).astype(BF16)
    return a, b, c


def _dot(a, b):
    return jnp.dot(a, b, preferred_element_type=F32)


def _dot_nt(a, b):
    return lax.dot_general(a, b, (((1,), (1,)), ((), ())), preferred_element_type=F32)


def _dot_f32_by_01(x, e01):
    a, b, c = _split3(x)
    return _dot(a, e01) + _dot(b, e01) + _dot(c, e01)


def _dot_01_by_f32(t01, x):
    a, b, c = _split3(x)
    return _dot(t01, a) + _dot(t01, b) + _dot(t01, c)


def _rms(x):
    return x * lax.rsqrt(jnp.mean(x * x, axis=-1, keepdims=True) + EPS)


def _silu(x):
    return x * jax.nn.sigmoid(x)


def _softplus(x):
    return jnp.maximum(x, 0.0) + jnp.log1p(jnp.exp(-jnp.abs(x)))


def _inproj_body(x_ref, g_ref, w_ref, wqkv_ref, wdt_ref, qg_ref, kg_ref, o_ref, dt_ref, qo_ref, ko_ref, vo_ref):
    xb = (_rms(x_ref[...]) * g_ref[...]).astype(BF16)
    dt_ref[...] = _dot(xb, wdt_ref[...])
    for lo in range(0, P_WIDTH, INPROJ_CHUNK):
        cs = slice(lo, min(lo + INPROJ_CHUNK, P_WIDTH))
        o_ref[:, cs] = _dot(xb, w_ref[:, cs]).astype(o_ref.dtype)
    scale = 1.0 / math.sqrt(SB_HEAD_DIM)
    q = _dot(xb, wqkv_ref[:, 0:SB_WIDTH])
    k = _dot(xb, wqkv_ref[:, SB_WIDTH:2 * SB_WIDTH])
    v = _dot(xb, wqkv_ref[:, 2 * SB_WIDTH:3 * SB_WIDTH])
    for h in range(SB_HEADS):
        sl = slice(h * SB_HEAD_DIM, (h + 1) * SB_HEAD_DIM)
        qo_ref[0, h] = ((_rms(q[:, sl]) * qg_ref[...]) * scale).astype(BF16)
        ko_ref[0, h] = (_rms(k[:, sl]) * kg_ref[...]).astype(BF16)
        vo_ref[0, h] = v[:, sl].astype(BF16)


def _inproj(x2, g, w_main, w_qkv, w_dt, qg, kg, bsz, seq):
    t = x2.shape[0]
    tm = min(512, seq)
    ns = seq // tm
    head_spec = pl.BlockSpec((1, SB_HEADS, tm, SB_HEAD_DIM), lambda i: (i // ns, 0, i % ns, 0))
    head_shape = jax.ShapeDtypeStruct((bsz, SB_HEADS, seq, SB_HEAD_DIM), BF16)
    return pl.pallas_call(
        _inproj_body,
        grid=(t // tm,),
        in_specs=[
            pl.BlockSpec((tm, D_MODEL), lambda i: (i, 0)),
            _const_spec((1, D_MODEL)),
            _const_spec((D_MODEL, P_WIDTH)),
            _const_spec((D_MODEL, 3 * SB_WIDTH)),
            _const_spec((D_MODEL, LANES)),
            _const_spec((1, SB_HEAD_DIM)),
            _const_spec((1, SB_HEAD_DIM)),
        ],
        out_specs=[
            pl.BlockSpec((tm, P_WIDTH), lambda i: (i, 0)),
            pl.BlockSpec((tm, LANES), lambda i: (i, 0)),
            head_spec, head_spec, head_spec,
        ],
        out_shape=[
            jax.ShapeDtypeStruct((t, P_WIDTH), PROJ_DTYPE),
            jax.ShapeDtypeStruct((t, LANES), F32),
            head_shape, head_shape, head_shape,
        ],
        compiler_params=_params("parallel"),
        name="inproj",
    )(x2, g, w_main, w_qkv, w_dt, qg, kg)


def _sb_body(q_ref, k_ref, v_ref, uo_ref, o_ref, carry_ref, acc_ref):
    t = SB_TILE
    first = pl.program_id(2) * SB_CHAINS
    uo = uo_ref[...]
    row = lax.broadcasted_iota(jnp.int32, (t, t), 0)
    col = lax.broadcasted_iota(jnp.int32, (t, t), 1)
    mask = col < row

    def visit(c, kb, diag):
        start = pl.multiple_of(jnp.maximum(kb, 0) * t, t)
        q = q_ref[0, 0, c * t:(c + 1) * t, :]
        k = k_ref[0, 0, pl.ds(start, t), :]
        v = v_ref[0, 0, pl.ds(start, t), :]
        z = _dot_nt(q, k)
        neg_log_rest = jnp.maximum(z, 0.0) + jnp.log(1.0 + jnp.exp(-jnp.abs(z)))
        log_beta = z - neg_log_rest
        if diag:
            neg_log_rest = jnp.where(mask, neg_log_rest, 0.0)
        sr = _dot(neg_log_rest.astype(BF16), uo)
        if diag:
            w = jnp.where(mask, jnp.exp(log_beta - sr[:, :t]), 0.0)
            carry = -sr[:, t:]
            acc_ref[c] = _dot(w.astype(BF16), v)
        else:
            prev = carry_ref[c]
            w = jnp.exp(log_beta - sr[:, :t] + jnp.concatenate([prev] * (t // LANES), axis=1))
            w = jnp.where(kb >= 0, w, 0.0)
            carry = prev - sr[:, t:]
            acc_ref[c] += _dot(w.astype(BF16), v)
        carry_ref[c] = carry
        return jnp.max(carry)

    def unfinished(j, maxima):
        need = [jnp.logical_and(first + c - j >= 0, maxima[c] > SB_STOP) for c in range(SB_CHAINS)]
        return functools.reduce(jnp.logical_or, need).astype(jnp.int32)

    maxima = [visit(c, first + c, True) for c in range(SB_CHAINS)]

    def body(state):
        j, _ = state
        maxima = [visit(c, first + c - j, False) for c in range(SB_CHAINS)]
        return j + 1, unfinished(j + 1, maxima)

    lax.while_loop(lambda state: state[1] > 0, body, (jnp.int32(1), unfinished(1, maxima)))
    for c in range(SB_CHAINS):
        o_ref[0, c * t:(c + 1) * t, :] = acc_ref[c].astype(o_ref.dtype)


def _sb_attention(qn, kn, vn):
    bsz, _, seq, _ = qn.shape
    t = SB_TILE
    tq = t * SB_CHAINS
    assert seq % tq == 0
    r = jnp.arange(t)
    upper = (r[:, None] > r[None, :])
    uo = jnp.concatenate([upper, jnp.ones((t, LANES), bool)], axis=1).astype(BF16)
    kv_spec = pl.BlockSpec((1, 1, seq, SB_HEAD_DIM), lambda b, h, i: (b, h, 0, 0))
    return pl.pallas_call(
        _sb_body,
        grid=(bsz, SB_HEADS, seq // tq),
        in_specs=[pl.BlockSpec((1, 1, tq, SB_HEAD_DIM), lambda b, h, i: (b, h, i, 0)),
                  kv_spec, kv_spec, _const_spec((t, t + LANES))],
        out_specs=pl.BlockSpec((1, tq, SB_HEAD_DIM), lambda b, h, i: (b, i, h)),
        out_shape=jax.ShapeDtypeStruct((bsz, seq, SB_WIDTH), BF16),
        scratch_shapes=[pltpu.VMEM((SB_CHAINS, t, LANES), F32), pltpu.VMEM((SB_CHAINS, t, SB_HEAD_DIM), F32)],
        compiler_params=_params("parallel", "parallel", "arbitrary"),
        name="sb_attention",
    )(qn, kn, vn, uo)


def _ssd_body(xbcs_ref, zs_ref, dts_ref, cw_ref, cb_ref, dtb_ref, alog_ref, dskip_ref, ng_ref, e_ref, tri_ref,
              shift_ref, os_ref, bufs_ref, sts_ref, ys_ref):
    tail = SSD_CONV_TAIL

    @pl.when(pl.program_id(0) == 0)
    def _():
        bufs_ref[:, 0:tail, :] = jnp.zeros((bufs_ref.shape[0], tail, SSD_CONV_DIM), bufs_ref.dtype)
        sts_ref[...] = jnp.zeros_like(sts_ref)

    for b in range(xbcs_ref.shape[0]):
        _ssd_chunk(xbcs_ref.at[b], zs_ref.at[b], dts_ref.at[b], cw_ref, cb_ref, dtb_ref, alog_ref, dskip_ref, ng_ref,
                   e_ref, tri_ref, shift_ref, os_ref.at[b], bufs_ref.at[b], sts_ref.at[b], ys_ref.at[b])


def _ssd_chunk(xbc_ref, z_ref, dt_ref, cw_ref, cb_ref, dtb_ref, alog_ref, dskip_ref, ng_ref, e_ref, tri_ref,
               shift_ref, o_ref, buf_ref, st_ref, y_ref):
    L = SSD_CHUNK
    G, R, P, N = SSD_GROUPS, SSD_HEADS_PER_GROUP, SSD_HEAD_DIM, SSD_STATE
    tail = SSD_CONV_TAIL
    buf_ref[tail:tail + L, :] = xbc_ref[...]

    def conv_silu_piece(lo, hi):
        delayed = _dot(shift_ref[...], buf_ref[:, lo:hi])
        acc = cb_ref[:, lo:hi] + cw_ref[SSD_CONV - 1:SSD_CONV, lo:hi] * xbc_ref[:, lo:hi].astype(F32)
        for k in range(SSD_CONV - 1):
            acc = acc + cw_ref[k:k + 1, lo:hi] * delayed[k * L:(k + 1) * L, :]
        return _silu(acc)

    def conv_silu(lo, hi):
        step = SSD_CONV_PIECE
        return jnp.concatenate([conv_silu_piece(c, c + step) for c in range(lo, hi, step)], axis=1)

    xs = conv_silu(0, SSD_INNER)
    b_in = conv_silu(SSD_INNER, SSD_INNER + G * N)
    c_in = conv_silu(SSD_INNER + G * N, SSD_CONV_DIM)
    buf_ref[0:tail, :] = buf_ref[L:L + tail, :]

    e01 = e_ref[...]
    dt = _softplus(dt_ref[...] + dtb_ref[...])
    da = dt * (-jnp.exp(alog_ref[...]))
    a_cs = _dot_01_by_f32(tri_ref[...], da)
    a_cs_t = a_cs.T
    a_full = _dot_f32_by_01(a_cs, e01)
    dt_full = _dot_f32_by_01(dt, e01)
    a_last = a_full[L - 1:L, :]
    ea_full = jnp.exp(a_full)
    x_dt = xs * dt_full
    xw = (x_dt * jnp.exp(a_last - a_full)).astype(BF16)
    ea_last = jnp.exp(a_last)

    row = lax.broadcasted_iota(jnp.int32, (L, L), 0)
    col = lax.broadcasted_iota(jnp.int32, (L, L), 1)
    causal = col <= row
    lane = lax.broadcasted_iota(jnp.int32, (L, LANES), 1)
    first_head = lane < P

    for g in range(G):
        gs = slice(g * N, (g + 1) * N)
        cg = c_in[:, gs].astype(BF16)
        bg_f32 = b_in[:, gs]
        cb = _dot_nt(cg, bg_f32.astype(BF16))
        for pair in range(R // 2):
            ms = []
            for r in range(2):
                h = g * R + pair * 2 + r
                seg = a_cs[:, h:h + 1] - a_cs_t[h:h + 1, :]
                decay = jnp.exp(jnp.where(causal, seg, -jnp.inf))
                ms.append((cb * decay).astype(BF16))
            c0 = (g * R + pair * 2) * P
            xp = x_dt[:, c0:c0 + LANES]
            rhs = jnp.concatenate([jnp.where(first_head, xp, 0.0).astype(BF16),
                                   jnp.where(first_head, 0.0, xp).astype(BF16)], axis=0)
            y_ref[:, c0:c0 + LANES] = _dot(jnp.concatenate(ms, axis=1), rhs)
        cs = slice(g * R * P, (g + 1) * R * P)
        state = st_ref[g]
        y_ref[:, cs] += _dot(cg, state.astype(BF16)) * ea_full[:, cs]
        st_ref[g] = state * ea_last[:, cs] + _dot(bg_f32.T.astype(BF16), xw[:, cs])

    y = y_ref[...] + xs * dskip_ref[...]
    y = y * _silu(z_ref[...].astype(F32))
    gw = SSD_INNER // G
    for g in range(G):
        cs = slice(g * gw, (g + 1) * gw)
        o_ref[:, cs] = (_rms(y[:, cs]) * ng_ref[:, cs]).astype(o_ref.dtype)


def _ssd(proj, dt_raw, conv_w, conv_b, dt_bias, a_log, d_skip, norm_g, bsz, seq):
    L = SSD_CHUNK
    nc = seq // L
    t = bsz * seq
    pad = LANES - SSD_HEADS
    heads = jnp.arange(LANES)
    cols = jnp.arange(SSD_INNER) // SSD_HEAD_DIM
    e01 = (heads[:, None] == cols[None, :]).astype(BF16)
    r = jnp.arange(L)
    tri = (r[None, :] <= r[:, None]).astype(BF16)
    assert proj.dtype == BF16, "the 0/1 shift product is exact only on bf16 data"
    delay = (SSD_CONV - 1) - jnp.arange((SSD_CONV - 1) * L) // L
    src = SSD_CONV_TAIL + jnp.arange((SSD_CONV - 1) * L) % L - delay
    shift = (src[:, None] == jnp.arange(SSD_CONV_TAIL + L)[None, :]).astype(BF16)
    rowvec = lambda v: v.reshape(1, -1).astype(F32)
    proj3 = proj.reshape(bsz, seq, proj.shape[-1])
    return pl.pallas_call(
        _ssd_body,
        grid=(nc,),
        in_specs=[
            pl.BlockSpec((bsz, L, SSD_CONV_DIM), lambda c: (0, c, P_XBC // SSD_CONV_DIM)),
            pl.BlockSpec((bsz, L, SSD_INNER), lambda c: (0, c, P_Z // SSD_INNER)),
            pl.BlockSpec((bsz, L, LANES), lambda c: (0, c, 0)),
            _const_spec((SSD_CONV, SSD_CONV_DIM)),
            _const_spec((1, SSD_CONV_DIM)),
            _const_spec((1, LANES)),
            _const_spec((1, LANES)),
            _const_spec((1, SSD_INNER)),
            _const_spec((1, SSD_INNER)),
            _const_spec((LANES, SSD_INNER)),
            _const_spec((L, L)),
            _const_spec(((SSD_CONV - 1) * L, SSD_CONV_TAIL + L)),
        ],
        out_specs=pl.BlockSpec((bsz, L, SSD_INNER), lambda c: (0, c, 0)),
        out_shape=jax.ShapeDtypeStruct((bsz, seq, SSD_INNER), BF16),
        scratch_shapes=[
            pltpu.VMEM((bsz, SSD_CONV_TAIL + L, SSD_CONV_DIM), BF16),
            pltpu.VMEM((bsz, SSD_GROUPS, SSD_STATE, SSD_HEADS_PER_GROUP * SSD_HEAD_DIM), F32),
            pltpu.VMEM((bsz, L, SSD_INNER), F32),
        ],
        compiler_params=_params("arbitrary"),
        name="ssd",
    )(proj3, proj3, dt_raw.reshape(bsz, seq, LANES), conv_w.astype(F32), rowvec(conv_b),
      jnp.pad(rowvec(dt_bias), ((0, 0), (0, pad))), jnp.pad(rowvec(a_log), ((0, 0), (0, pad))),
      rowvec(jnp.repeat(d_skip, SSD_HEAD_DIM)), rowvec(norm_g), e01, tri, shift).reshape(t, SSD_INNER)


def _merge_body(x_ref, yssd_ref, ysb_ref, u_ref, halo_ref, g0_ref, g1_ref, g2_ref,
                wssd_ref, wsb_ref, wpool_ref, wout_ref, pw_ref, ps_ref, o_ref, ext_ref, *, seq):
    tm = x_ref.shape[0]
    H = POOL_HALO
    start = (pl.program_id(0) * tm) % seq
    halo = halo_ref[...].astype(F32)
    ext_ref[0:H, :] = jnp.where(start == 0, jnp.zeros_like(halo), halo)
    ext_ref[H:H + tm, :] = u_ref[...].astype(F32)
    pos = start + lax.broadcasted_iota(jnp.int32, (tm, 1), 0)

    mixed = []
    for gi, win in enumerate(POOL_WINDOWS):
        cs = slice(gi * POOL_GROUP_DIM, (gi + 1) * POOL_GROUP_DIM)
        cur = ext_ref[H:H + tm, cs]
        wsum = cur
        for k in range(1, win):
            wsum = wsum + ext_ref[H - k:H - k + tm, cs]
        count = jnp.minimum(pos + 1, win).astype(F32)
        pooled = wsum / count - cur
        mixed.append((_dot(pooled.astype(BF16), pw_ref[gi]) * ps_ref[:, cs]).astype(BF16))
    ypool = jnp.concatenate(mixed, axis=1)
    yssd = yssd_ref[...]
    ysb = ysb_ref[...]
    merged = []
    for lo in range(0, D_MODEL, MERGE_CHUNK):
        cs = slice(lo, lo + MERGE_CHUNK)
        part = jax.nn.sigmoid(g0_ref[:, cs].astype(F32)) * _dot(yssd, wssd_ref[:, cs])
        part += jax.nn.sigmoid(g1_ref[:, cs].astype(F32)) * _dot(ysb, wsb_ref[:, cs])
        part += jax.nn.sigmoid(g2_ref[:, cs].astype(F32)) * _dot(ypool, wpool_ref[:, cs])
        merged.append(part.astype(BF16))
    o_ref[...] = x_ref[...] + _dot(jnp.concatenate(merged, axis=1), wout_ref[...])


def _merge(x2, y_ssd, y_sb, proj, pool_w, pool_scale, w_br_ssd, w_br_sb, w_br_pool, w_out, seq):
    t = x2.shape[0]
    tm = min(512, seq)
    hb = tm // POOL_HALO
    gate = lambda k: pl.BlockSpec((tm, D_MODEL), lambda i: (i, P_GATES // D_MODEL + k))
    return pl.pallas_call(
        functools.partial(_merge_body, seq=seq),
        grid=(t // tm,),
        in_specs=[
            pl.BlockSpec((tm, D_MODEL), lambda i: (i, 0)),
            pl.BlockSpec((tm, SSD_INNER), lambda i: (i, 0)),
            pl.BlockSpec((tm, SB_WIDTH), lambda i: (i, 0)),
            pl.BlockSpec((tm, POOL_WIDTH), lambda i: (i, P_U // POOL_WIDTH)),
            pl.BlockSpec((POOL_HALO, POOL_WIDTH), lambda i: (jnp.maximum(i * hb - 1, 0), P_U // POOL_WIDTH)),
            gate(0), gate(1), gate(2),
            _const_spec((SSD_INNER, D_MODEL)),
            _const_spec((SB_WIDTH, D_MODEL)),
            _const_spec((POOL_WIDTH, D_MODEL)),
            _const_spec((D_MODEL, D_MODEL)),
            _const_spec((POOL_GROUPS, POOL_GROUP_DIM, POOL_GROUP_DIM)),
            _const_spec((1, POOL_WIDTH)),
        ],
        out_specs=pl.BlockSpec((tm, D_MODEL), lambda i: (i, 0)),
        out_shape=jax.ShapeDtypeStruct((t, D_MODEL), F32),
        scratch_shapes=[pltpu.VMEM((tm + POOL_HALO, POOL_WIDTH), F32)],
        compiler_params=_params("parallel"),
        name="merge",
    )(x2, y_ssd, y_sb, proj, proj, proj, proj, proj,
      w_br_ssd.astype(BF16), w_br_sb.astype(BF16), w_br_pool.astype(BF16), w_out.astype(BF16),
      pool_w.astype(BF16), pool_scale.reshape(1, -1).astype(F32))


def _swiglu_into(acc_ref, xb, wg_ref, wu_ref, wd_ref, width):
    for c in range(width // FFN_CHUNK):
        cs = slice(c * FFN_CHUNK, (c + 1) * FFN_CHUNK)
        hidden = _silu(_dot(xb, wg_ref[:, cs])) * _dot(xb, wu_ref[:, cs])
        acc_ref[...] += _dot(hidden.astype(BF16), wd_ref[cs, :])


def _ffn_body(x_ref, g_ref, wg_ref, wu_ref, wd_ref, o_ref):
    x = x_ref[...]
    o_ref[...] = x
    _swiglu_into(o_ref, (_rms(x) * g_ref[...]).astype(BF16), wg_ref, wu_ref, wd_ref, FFN_DENSE)


def _ffn_dense(x2, g, w_gate, w_up, w_down):
    t = x2.shape[0]
    tm = min(512, t)
    return pl.pallas_call(
        _ffn_body,
        grid=(t // tm,),
        in_specs=[
            pl.BlockSpec((tm, D_MODEL), lambda i: (i, 0)),
            _const_spec((1, D_MODEL)),
            _const_spec((D_MODEL, FFN_DENSE)),
            _const_spec((D_MODEL, FFN_DENSE)),
            _const_spec((FFN_DENSE, D_MODEL)),
        ],
        out_specs=pl.BlockSpec((tm, D_MODEL), lambda i: (i, 0)),
        out_shape=jax.ShapeDtypeStruct((t, D_MODEL), F32),
        compiler_params=_params("parallel"),
        name="ffn_dense",
    )(x2, g, w_gate.astype(BF16), w_up.astype(BF16), w_down.astype(BF16))


def _router_body(x_ref, g_ref, rw_ref, h_ref, idx_ref, wt_ref):
    h = _rms(x_ref[...]) * g_ref[...]
    h_ref[...] = h
    rw = rw_ref[...]
    h_hi = h.astype(BF16)
    h_lo = (h - h_hi.astype(F32)).astype(BF16)
    w_hi = rw.astype(BF16)
    w_lo = (rw - w_hi.astype(F32)).astype(BF16)
    logits = _dot(h_hi, w_hi) + (_dot(h_hi, w_lo) + _dot(h_lo, w_hi))
    lane = lax.broadcasted_iota(jnp.int32, logits.shape, 1)
    lane_f = lane.astype(F32)
    logits = jnp.where(lane < N_EXPERTS, logits, -jnp.inf)
    m1 = jnp.max(logits, axis=-1, keepdims=True)
    i1 = jnp.min(jnp.where(logits == m1, lane_f, float(LANES)), axis=-1, keepdims=True)
    rest = jnp.where(lane_f == i1, -jnp.inf, logits)
    m2 = jnp.max(rest, axis=-1, keepdims=True)
    i2 = jnp.min(jnp.where(rest == m2, lane_f, float(LANES)), axis=-1, keepdims=True)
    e = jnp.exp(m2 - m1)
    w1 = 1.0 / (1.0 + e)
    idx = jnp.where(lane == 0, i1, jnp.where(lane == 1, i2, 0.0))
    idx_ref[...] = idx.T[0:idx_ref.shape[0], :].astype(jnp.int32)
    wt_ref[...] = jnp.where(lane == 0, w1, jnp.where(lane == 1, e * w1, 0.0))


def _router(x2, g, router_w):
    t = x2.shape[0]
    tm = min(512, t)
    rw = jnp.pad(router_w.astype(F32), ((0, 0), (0, LANES - N_EXPERTS)))
    row = pl.BlockSpec((tm, D_MODEL), lambda i: (i, 0))
    small = pl.BlockSpec((tm, LANES), lambda i: (i, 0))
    return pl.pallas_call(
        _router_body,
        grid=(t // tm,),
        in_specs=[row, _const_spec((1, D_MODEL)), _const_spec((D_MODEL, LANES))],
        out_specs=[row, pl.BlockSpec((SUBLANES, tm), lambda i: (0, i)), small],
        out_shape=[jax.ShapeDtypeStruct((t, D_MODEL), F32),
                   jax.ShapeDtypeStruct((SUBLANES, t), jnp.int32),
                   jax.ShapeDtypeStruct((t, LANES), F32)],
        compiler_params=_params("parallel"),
        name="router",
    )(x2, g, rw)


def _dispatch_body(pends_ref, counts_ref, slot_ref, h_ref, xb_hbm, zero_ref, sem):
    tm = h_ref.shape[0]

    def zero_block(start):
        fill = pltpu.make_async_copy(zero_ref, xb_hbm.at[pl.ds(pl.multiple_of(start, MOE_BLOCK), MOE_BLOCK)], sem)
        fill.start()
        fill.wait()

    @pl.when(pl.program_id(0) == 0)
    def _():
        zero_ref[...] = jnp.zeros_like(zero_ref)
        for e in range(N_EXPERTS):
            pl.when(counts_ref[e] > 0)(functools.partial(zero_block, pends_ref[e] - MOE_BLOCK))
        for j in range(N_EXPERTS):
            start = pends_ref[N_EXPERTS - 1] + j * MOE_BLOCK
            pl.when(start < xb_hbm.shape[0])(functools.partial(zero_block, start))

    for k in range(TOP_K):
        for r in range(tm):
            pltpu.make_async_copy(h_ref.at[pl.ds(r, 1)], xb_hbm.at[pl.ds(slot_ref[0, 0, k * tm + r], 1)],
                                  sem).start(priority=r % 2)
    for k in range(TOP_K):
        pltpu.make_async_copy(h_ref, xb_hbm.at[pl.ds(0, tm)], sem).wait()


def _dispatch(h, slot_tiles, pends, counts, cap):
    t = h.shape[0]
    nt = slot_tiles.shape[0]
    tm = t // nt
    grid_spec = pltpu.PrefetchScalarGridSpec(
        num_scalar_prefetch=2,
        grid=(nt,),
        in_specs=[
            pl.BlockSpec((1, 1, TOP_K * tm), lambda i, pe, co: (i, 0, 0), memory_space=pltpu.SMEM),
            pl.BlockSpec((tm, D_MODEL), lambda i, pe, co: (i, 0)),
        ],
        out_specs=pl.BlockSpec(memory_space=pl.ANY),
        scratch_shapes=[pltpu.VMEM((MOE_BLOCK, D_MODEL), F32), pltpu.SemaphoreType.DMA],
    )
    return pl.pallas_call(
        _dispatch_body,
        grid_spec=grid_spec,
        out_shape=jax.ShapeDtypeStruct((cap, D_MODEL), F32),
        compiler_params=_params("arbitrary"),
        name="dispatch",
    )(pends.astype(jnp.int32), counts.astype(jnp.int32), slot_tiles, h)


def _experts_body(bexp_ref, nused_ref, xb_ref, wg_ref, wu_ref, wd_ref, o_ref):
    del bexp_ref
    o_ref[...] = jnp.zeros_like(o_ref)

    @pl.when(pl.program_id(0) < nused_ref[0])
    def _():
        _swiglu_into(o_ref, xb_ref[...].astype(BF16), wg_ref, wu_ref, wd_ref, FFN_EXPERT)


def _experts(xb, block_exp, n_used, e_gate, e_up, e_down):
    n_blocks = block_exp.shape[0]
    wspec = lambda shape: pl.BlockSpec((None,) + shape, lambda i, be, nu: (be[i], 0, 0))
    rows = pl.BlockSpec((MOE_BLOCK, D_MODEL), lambda i, be, nu: (i, 0))
    used_rows = pl.BlockSpec((MOE_BLOCK, D_MODEL), lambda i, be, nu: (jnp.minimum(i, nu[0] - 1), 0))
    grid_spec = pltpu.PrefetchScalarGridSpec(
        num_scalar_prefetch=2,
        grid=(n_blocks,),
        in_specs=[used_rows, wspec((D_MODEL, FFN_EXPERT)), wspec((D_MODEL, FFN_EXPERT)),
                  wspec((FFN_EXPERT, D_MODEL))],
        out_specs=rows,
    )
    return pl.pallas_call(
        _experts_body,
        grid_spec=grid_spec,
        out_shape=jax.ShapeDtypeStruct((n_blocks * MOE_BLOCK, D_MODEL), F32),
        compiler_params=_params("arbitrary"),
        name="experts",
    )(block_exp, n_used, xb, e_gate.astype(BF16), e_up.astype(BF16), e_down.astype(BF16))


def _combine_body(slot_ref, next_ref, x_ref, wt_ref, yb_hbm, o_ref, buf_ref, sem):
    tm = x_ref.shape[0]
    n = TOP_K * tm
    i = pl.program_id(0)

    def fetch(idx_ref, b):
        for r in range(n):
            pltpu.make_async_copy(yb_hbm.at[pl.ds(idx_ref[0, 0, r], 1)], buf_ref.at[b, pl.ds(r, 1)],
                                  sem.at[b]).start(priority=r % 2)

    @pl.when(i == 0)
    def _():
        fetch(slot_ref, 0)

    for cur in range(2):
        @pl.when(i % 2 == cur)
        def _():
            @pl.when(i + 1 < pl.num_programs(0))
            def _():
                fetch(next_ref, 1 - cur)

            pltpu.make_async_copy(yb_hbm.at[pl.ds(0, n)], buf_ref.at[cur], sem.at[cur]).wait()
            wt = wt_ref[...]
            o_ref[...] = x_ref[...] + (wt[:, 0:1] * buf_ref[cur, 0:tm, :] + wt[:, 1:2] * buf_ref[cur, tm:n, :])


def _combine(x2, wt, slot_tiles, yb):
    t = x2.shape[0]
    nt = slot_tiles.shape[0]
    tm = t // nt
    slots = lambda shift: pl.BlockSpec((1, 1, TOP_K * tm), lambda i: (jnp.minimum(i + shift, nt - 1), 0, 0),
                                       memory_space=pltpu.SMEM)
    return pl.pallas_call(
        _combine_body,
        grid=(nt,),
        in_specs=[
            slots(0), slots(1),
            pl.BlockSpec((tm, D_MODEL), lambda i: (i, 0)),
            pl.BlockSpec((tm, LANES), lambda i: (i, 0)),
            pl.BlockSpec(memory_space=pl.ANY),
        ],
        out_specs=pl.BlockSpec((tm, D_MODEL), lambda i: (i, 0)),
        out_shape=jax.ShapeDtypeStruct((t, D_MODEL), F32),
        scratch_shapes=[pltpu.VMEM((2, TOP_K * tm, D_MODEL), F32), pltpu.SemaphoreType.DMA((2,))],
        compiler_params=_params("arbitrary"),
        name="combine",
    )(slot_tiles, slot_tiles, x2, wt, yb)


def _moe(x2, g, router_w, e_gate, e_up, e_down):
    t = x2.shape[0]
    h, idx, wt = _router(x2, g, router_w)
    n_assign = t * TOP_K
    cap = -(-n_assign // MOE_BLOCK) * MOE_BLOCK + N_EXPERTS * MOE_BLOCK
    n_blocks = cap // MOE_BLOCK
    expert_flat = idx[:TOP_K].reshape(-1)
    onehot = (expert_flat[None, :] == jnp.arange(N_EXPERTS, dtype=jnp.int32)[:, None]).astype(jnp.int32)
    running = jnp.cumsum(onehot, axis=1)
    counts = running[:, -1]
    rank = jnp.sum(running * onehot, axis=0) - 1
    padded = (counts + MOE_BLOCK - 1) // MOE_BLOCK * MOE_BLOCK
    pends = jnp.cumsum(padded)
    pstarts = pends - padded
    dest = (jnp.sum(pstarts[:, None] * onehot, axis=0) + rank).astype(jnp.int32)
    block_exp = jnp.minimum(jnp.searchsorted(pends, jnp.arange(n_blocks) * MOE_BLOCK, side="right"),
                            N_EXPERTS - 1).astype(jnp.int32)
    n_used = (pends[-1:] // MOE_BLOCK).astype(jnp.int32)
    def slot_tiles(tile):
        tm = min(tile, t)
        return dest.reshape(TOP_K, t // tm, tm).transpose(1, 0, 2).reshape(t // tm, 1, TOP_K * tm)

    xb = _dispatch(h, slot_tiles(DISPATCH_TILE), pends, counts, cap)
    yb = _experts(xb, block_exp, n_used, e_gate, e_up, e_down)
    return _combine(x2, wt, slot_tiles(COMBINE_TILE), yb)


def _permute_w_in(w):
    w = w.astype(BF16)
    main = jnp.concatenate([w[:, COL_Z:COL_XBC], w[:, :COL_Z], w[:, COL_POOL:], w[:, COL_V:COL_POOL]], axis=1)
    qkv = w[:, COL_DT:COL_V]
    dt = jnp.pad(w[:, COL_XBC:COL_DT], ((0, 0), (0, LANES - SSD_HEADS)))
    return main, qkv, dt


def kernel(x, mix_norm_g, w_in, conv_w, conv_b, dt_bias, a_log, d_skip, ssd_norm_g, q_norm_g, k_norm_g, pool_w,
           pool_scale, w_br_ssd, w_br_sb, w_br_pool, w_out, ffn_norm_g, ffn_w_gate, ffn_w_up, ffn_w_down,
           router_w, moe_w_gate, moe_w_up, moe_w_down):
    bsz, seq, d = x.shape
    depth = w_in.shape[0]
    x2 = x.reshape(bsz * seq, d)
    rowvec = lambda v: v.reshape(1, -1).astype(F32)
    for layer in range(depth):
        w_main, w_qkv, w_dt = _permute_w_in(w_in[layer])
        proj, dt_raw, qn, kn, vn = _inproj(x2, rowvec(mix_norm_g[layer]), w_main, w_qkv, w_dt,
                                           rowvec(q_norm_g[layer]), rowvec(k_norm_g[layer]), bsz, seq)
        y_sb = _sb_attention(qn, kn, vn).reshape(bsz * seq, SB_WIDTH)
        y_ssd = _ssd(proj, dt_raw, conv_w[layer], conv_b[layer], dt_bias[layer], a_log[layer], d_skip[layer],
                     ssd_norm_g[layer], bsz, seq)
        x2 = _merge(x2, y_ssd, y_sb, proj, pool_w[layer], pool_scale[layer], w_br_ssd[layer], w_br_sb[layer],
                    w_br_pool[layer], w_out[layer], seq)
        i = layer // 2
        if layer % 2 == 0:
            x2 = _ffn_dense(x2, rowvec(ffn_norm_g[layer]), ffn_w_gate[i], ffn_w_up[i], ffn_w_down[i])
        else:
            x2 = _moe(x2, rowvec(ffn_norm_g[layer]), router_w[i], moe_w_gate[i], moe_w_up[i], moe_w_down[i])
    return x2.reshape(bsz, seq, d)
```
